```python
import math
import jax
import jax.numpy as jnp
from jax import lax
import numpy as np

D_MODEL = 1024
BATCH = 2
SEQ = 8192
DEPTH = 4
DEC_BATCH = 128
DEC_SEQ = 1
PAST_LEN = 8192
PAGE_SIZE = 128

D_MIX = D_MODEL
HEAD_DIM = 64
A_WIDTH = D_MIX // 4
A_HEADS = A_WIDTH // HEAD_DIM
A_DECAY_LORA = 64
A_AAA_LORA = 64
A_GATE_LORA = 128
A_PROJ = 3 * A_WIDTH + A_DECAY_LORA + A_AAA_LORA + A_GATE_LORA
B_WIDTH = D_MIX // 4
B_HEADS = B_WIDTH // HEAD_DIM
B_KV_HEADS = 2
B_GROUP = B_HEADS // B_KV_HEADS
B_PROJ = B_WIDTH + 2 * B_KV_HEADS * HEAD_DIM
WINDOW = 128
BLOCK = 128
C_WIDTH = D_MIX - A_WIDTH - B_WIDTH
C_HEADS = C_WIDTH // HEAD_DIM
C_GROUPS = 2
C_HPG = C_HEADS // C_GROUPS
D_STATE = 128
SSM_CONV = 4
CHUNK = 128
CONV_DIM = C_WIDTH + 2 * C_GROUPS * D_STATE
C_PROJ = C_WIDTH + CONV_DIM + C_HEADS
PROJ = A_PROJ + B_PROJ + C_PROJ
D_FF = ((8 * D_MODEL // 3 + 127) // 128) * 128
FFN_CONV = 3
NORM_EPS = 1e-6
GN_EPS = 64e-5

kernel_name = "hymba_rwkv7_swa_mamba2_convglu_step"

F32 = jnp.float32


def rms_norm(x, g):
    xf = x.astype(F32)
    y = xf * lax.rsqrt(jnp.mean(xf * xf, axis=-1, keepdims=True) + NORM_EPS)
    return (y * g.astype(F32)).astype(x.dtype)


def causal_dwconv(u, buf, w, b):
    k_w = w.shape[0]
    t = u.shape[1]
    full = jnp.concatenate([buf.astype(u.dtype), u], axis=1)
    y = b
    for j in range(k_w):
        y = y + full[:, j:j + t] * w[j]
    return y, full[:, t:]


def rwkv7_mixer(pa, shift_prev, wkv0, lp):
    n, t, _ = pa.shape
    prev = jnp.concatenate([shift_prev[:, None].astype(pa.dtype), pa[:, :-1]], axis=1)
    xs = pa + (prev - pa) * lp['rwkv_mu']
    cuts = [A_WIDTH, 2 * A_WIDTH, 3 * A_WIDTH, 3 * A_WIDTH + A_DECAY_LORA,
            3 * A_WIDTH + A_DECAY_LORA + A_AAA_LORA]
    r, k, v, lw, la, lg = jnp.split(xs, cuts, axis=-1)
    w_log = -jax.nn.softplus(-(lp['rwkv_w0'] + jnp.tanh(lw) @ lp['rwkv_w2'])) - 0.5
    a = jax.nn.sigmoid(lp['rwkv_a0'] + la @ lp['rwkv_a2'])
    g = jax.nn.sigmoid(lg) @ lp['rwkv_g2']

    def heads(z):
        return z.reshape(n, t, A_HEADS, HEAD_DIM).astype(F32)

    kk = heads(k * lp['rwkv_k_k'])
    kk = kk / jnp.maximum(jnp.sqrt(jnp.sum(kk * kk, axis=-1, keepdims=True)), 1e-12)
    k = k * (1 + (a - 1) * lp['rwkv_k_a'])
    rh, kh, vh, ah = heads(r), heads(k), heads(v), heads(a)
    decay = jnp.exp(-jnp.exp(heads(w_log)))

    def step(s, inp):
        r_t, k_t, v_t, d_t, a_t, b_t = inp
        sa = jnp.einsum('nhvk,nhk->nhv', s, a_t)
        s = s * d_t[:, :, None, :] + sa[..., None] * b_t[:, :, None, :] + v_t[..., None] * k_t[:, :, None, :]
        return s, jnp.einsum('nhvk,nhk->nhv', s, r_t)

    def tm(z):
        return jnp.moveaxis(z, 1, 0)

    s_fin, y = lax.scan(step, wkv0.astype(F32),
                        (tm(rh), tm(kh), tm(vh), tm(decay), tm(-kk), tm(kk * ah)))
    y = jnp.moveaxis(y, 0, 1)
    mean = jnp.mean(y, axis=-1, keepdims=True)
    var = jnp.mean(jnp.square(y - mean), axis=-1, keepdims=True)
    y = ((y - mean) * lax.rsqrt(var + GN_EPS)).reshape(n, t, A_WIDTH)
    y = y * lp['rwkv_ln_g'].astype(F32) + lp['rwkv_ln_b'].astype(F32)
    bonus = jnp.sum(rh * kh * lp['rwkv_r_k'].astype(F32), axis=-1, keepdims=True) * vh
    y = (y + bonus.reshape(n, t, A_WIDTH)) * g.astype(F32)
    return y.astype(pa.dtype), pa[:, -1], s_fin.astype(wkv0.dtype)


def sink_weights(s, mask, sinks):
    sink = sinks.astype(F32).reshape(B_KV_HEADS, B_GROUP, 1, 1)
    s = jnp.where(mask, s, -jnp.inf)
    m = jnp.maximum(jnp.max(s, axis=-1, keepdims=True), sink)
    p = jnp.exp(s - m)
    return p / (jnp.sum(p, axis=-1, keepdims=True) + jnp.exp(sink - m))


def swa_mixer(pb, cache_k, cache_v, lp, decode):
    n, t, _ = pb.shape
    q, k, v = jnp.split(pb, [B_WIDTH, B_WIDTH + B_KV_HEADS * HEAD_DIM], axis=-1)
    q = rms_norm(q.reshape(n, t, B_KV_HEADS, B_GROUP, HEAD_DIM), lp['attn_q_norm_g'])
    k = rms_norm(k.reshape(n, t, B_KV_HEADS, HEAD_DIM), lp['attn_k_norm_g'])
    v = v.reshape(n, t, B_KV_HEADS, HEAD_DIM)
    scale = HEAD_DIM ** -0.5
    if decode:
        kf = jnp.concatenate([cache_k.astype(k.dtype), k], axis=1)
        vf = jnp.concatenate([cache_v.astype(v.dtype), v], axis=1)
        s = jnp.einsum('ntkgd,nskd->nkgts', q, kf, preferred_element_type=F32) * scale
        rel = jnp.arange(t)[:, None] + WINDOW - jnp.arange(WINDOW + t)[None, :]
        p = sink_weights(s, (rel >= 0) & (rel <= WINDOW), lp['attn_sinks'])
        o = jnp.einsum('nkgts,nskd->ntkgd', p, vf.astype(F32))
        new_k, new_v = kf[:, t:], vf[:, t:]
    else:
        nb = t // BLOCK
        qb = q.reshape(n, nb, BLOCK, B_KV_HEADS, B_GROUP, HEAD_DIM)
        kb = k.reshape(n, nb, BLOCK, B_KV_HEADS, HEAD_DIM)
        vb = v.reshape(n, nb, BLOCK, B_KV_HEADS, HEAD_DIM)

        def with_prev(z):
            zp = jnp.pad(z, ((0, 0), (1, 0), (0, 0), (0, 0), (0, 0)))[:, :-1]
            return jnp.concatenate([zp, z], axis=2)

        kc, vc = with_prev(kb), with_prev(vb)
        s = jnp.einsum('nbqkgd,nbskd->nbkgqs', qb, kc, preferred_element_type=F32) * scale
        rel = jnp.arange(BLOCK)[:, None] + BLOCK - jnp.arange(2 * BLOCK)[None, :]
        band = (rel >= 0) & (rel <= WINDOW)
        has_prev = (jnp.arange(nb)[:, None, None] > 0) | (jnp.arange(2 * BLOCK) >= BLOCK)[None, None, :]
        mask = (band[None] & has_prev)[None, :, None, None]
        p = sink_weights(s, mask, lp['attn_sinks'])
        o = jnp.einsum('nbkgqs,nbskd->nbqkgd', p, vc.astype(F32))
        new_k, new_v = k[:, -WINDOW:], v[:, -WINDOW:]
    return o.reshape(n, t, B_WIDTH).astype(pb.dtype), new_k, new_v


def ssd_chunked(xdt, da, bm, cm, h0):
    n, t = xdt.shape[:2]
    nc = t // CHUNK
    x = xdt.reshape(n, nc, CHUNK, C_GROUPS, C_HPG, HEAD_DIM)
    acum = jnp.cumsum(da.reshape(n, nc, CHUNK, C_GROUPS, C_HPG), axis=2)
    b = bm.reshape(n, nc, CHUNK, C_GROUPS, D_STATE)
    c = cm.reshape(n, nc, CHUNK, C_GROUPS, D_STATE)
    causal = jnp.tril(jnp.ones((CHUNK, CHUNK), dtype=bool))[:, :, None, None]
    seg = acum[:, :, :, None] - acum[:, :, None, :]
    lmat = jnp.exp(jnp.where(causal, seg, -jnp.inf))
    cb = jnp.einsum('nclgd,ncmgd->nclmg', c, b)
    y_diag = jnp.einsum('nclmg,nclmge,ncmgep->nclgep', cb, lmat, x)
    decay_to_end = jnp.exp(acum[:, :, -1:] - acum)
    states = jnp.einsum('ncmgd,ncmge,ncmgep->ncgepd', b, decay_to_end, x)
    chunk_decay = jnp.exp(acum[:, :, -1])

    def step(h, inp):
        st, dec = inp
        return h * dec[..., None, None] + st, h

    h_init = h0.reshape(n, C_GROUPS, C_HPG, HEAD_DIM, D_STATE)
    h_fin, h_prev = lax.scan(step, h_init, (jnp.moveaxis(states, 1, 0), jnp.moveaxis(chunk_decay, 1, 0)))
    h_prev = jnp.moveaxis(h_prev, 0, 1)
    y_off = jnp.einsum('nclgd,ncgepd,nclge->nclgep', c, h_prev, jnp.exp(acum))
    y = (y_diag + y_off).reshape(n, t, C_HEADS, HEAD_DIM)
    return y, h_fin.reshape(n, C_HEADS, HEAD_DIM, D_STATE)


def ssd_recurrent(xdt, da, bm, cm, h0):
    bh = jnp.repeat(bm, C_HPG, axis=2)
    ch = jnp.repeat(cm, C_HPG, axis=2)

    def step(h, inp):
        x_t, a_t, b_t, c_t = inp
        h = h * jnp.exp(a_t)[..., None, None] + x_t[..., None] * b_t[:, :, None, :]
        return h, jnp.einsum('nhpd,nhd->nhp', h, c_t)

    h_fin, y = lax.scan(step, h0, (jnp.moveaxis(xdt, 1, 0), jnp.moveaxis(da, 1, 0),
                                   jnp.moveaxis(bh, 1, 0), jnp.moveaxis(ch, 1, 0)))
    return jnp.moveaxis(y, 0, 1), h_fin


def mamba2_mixer(pc, conv_buf, h0, lp, decode):
    n, t, _ = pc.shape
    z, xbc, dt = jnp.split(pc, [C_WIDTH, C_WIDTH + CONV_DIM], axis=-1)
    xbc, new_buf = causal_dwconv(xbc, conv_buf, lp['ssm_conv_w'], lp['ssm_conv_b'])
    xbc = jax.nn.silu(xbc)
    x, bm, cm = jnp.split(xbc, [C_WIDTH, C_WIDTH + C_GROUPS * D_STATE], axis=-1)
    x = x.reshape(n, t, C_HEADS, HEAD_DIM).astype(F32)
    bm = bm.reshape(n, t, C_GROUPS, D_STATE).astype(F32)
    cm = cm.reshape(n, t, C_GROUPS, D_STATE).astype(F32)
    dt = jax.nn.softplus(dt.astype(F32) + lp['ssm_dt_bias'].astype(F32))
    a = -jnp.exp(lp['ssm_a_log'].astype(F32))
    ssd = ssd_recurrent if decode else ssd_chunked
    y, h_fin = ssd(x * dt[..., None], dt * a, bm, cm, h0.astype(F32))
    y = (y + lp['ssm_d'].astype(F32)[:, None] * x).reshape(n, t, C_WIDTH)
    y = rms_norm(y * jax.nn.silu(z.astype(F32)), lp['ssm_norm_g'])
    return y.astype(pc.dtype), new_buf, h_fin.astype(h0.dtype)


def conv_glu(h, buf, lp):
    gate, val = jnp.split(h @ lp['ffn_w_up'], 2, axis=-1)
    gate, new_buf = causal_dwconv(gate, buf, lp['ffn_conv_w'], lp['ffn_conv_b'])
    return (jax.nn.silu(gate) * val) @ lp['ffn_w_down'], new_buf


def trunk_layer(x, c, shift_prev, wkv0, swa_k, swa_v, ssm_conv_buf, ssm_h0, ffn_buf, lp, decode):
    mod = jax.nn.silu(c) @ lp['ada_w'] + lp['ada_b']
    sh1, sc1, g1, sh2, sc2, g2 = jnp.split(mod[:, None, :], 6, axis=-1)
    h = rms_norm(x, lp['norm_mix_g']) * (1 + sc1) + sh1
    pa, pb, pc = jnp.split(h @ lp['w_in'], [A_PROJ, A_PROJ + B_PROJ], axis=-1)
    ya, new_shift, new_wkv = rwkv7_mixer(pa, shift_prev, wkv0, lp)
    yb, new_k, new_v = swa_mixer(pb, swa_k, swa_v, lp, decode)
    yc, new_conv, new_ssm = mamba2_mixer(pc, ssm_conv_buf, ssm_h0, lp, decode)
    x = x + g1 * (jnp.concatenate([ya, yb, yc], axis=-1) @ lp['w_out'])
    h = rms_norm(x, lp['norm_ffn_g']) * (1 + sc2) + sh2
    f, new_ffn = conv_glu(h, ffn_buf, lp)
    x = x + g2 * f
    return x, (new_shift, new_wkv, new_k, new_v, new_conv, new_ssm, new_ffn)


def setup_inputs(seed: int = 0) -> dict:
    key = jax.random.key(seed)
    ks = iter(jax.random.split(key, 64))

    def nrm(shape, s=1.0):
        return s * jax.random.normal(next(ks), shape, F32)

    def uni(shape, lo, hi):
        return jax.random.uniform(next(ks), shape, F32, lo, hi)

    L = DEPTH
    dt0 = jnp.exp(uni((L, C_HEADS), math.log(1e-3), math.log(1e-1)))
    return {
        'x_prompt': nrm((BATCH, SEQ, D_MODEL)),
        'x_sample': nrm((DEC_BATCH, DEC_SEQ, D_MODEL)),
        'c_prompt': nrm((BATCH, D_MODEL)),
        'c_sample': nrm((DEC_BATCH, D_MODEL)),
        'state_rwkv_shift': nrm((L, DEC_BATCH, A_PROJ)),
        'state_rwkv_wkv': nrm((L, DEC_BATCH, A_HEADS, HEAD_DIM, HEAD_DIM), 0.3),
        'cache_swa_k': nrm((L, DEC_BATCH, WINDOW, B_KV_HEADS, HEAD_DIM)),
        'cache_swa_v': nrm((L, DEC_BATCH, WINDOW, B_KV_HEADS, HEAD_DIM)),
        'state_ssm_conv': nrm((L, DEC_BATCH, SSM_CONV - 1, CONV_DIM)),
        'state_ssm': nrm((L, DEC_BATCH, C_HEADS, HEAD_DIM, D_STATE), 0.1),
        'state_ffn_conv': nrm((L, DEC_BATCH, FFN_CONV - 1, D_FF)),
        'ada_w': nrm((L, D_MODEL, 6 * D_MODEL), D_MODEL ** -0.5),
        'ada_b': nrm((L, 6 * D_MODEL), 0.02),
        'norm_mix_g': 1.0 + nrm((L, D_MODEL), 0.02),
        'norm_ffn_g': 1.0 + nrm((L, D_MODEL), 0.02),
        'w_in': nrm((L, D_MODEL, PROJ), D_MODEL ** -0.5),
        'w_out': nrm((L, D_MIX, D_MODEL), D_MIX ** -0.5),
        'rwkv_mu': uni((L, A_PROJ), 0.0, 1.0),
        'rwkv_w0': uni((L, A_WIDTH), -6.5, -1.5),
        'rwkv_w2': nrm((L, A_DECAY_LORA, A_WIDTH), 0.1 * A_DECAY_LORA ** -0.5),
        'rwkv_a0': nrm((L, A_WIDTH), 0.1),
        'rwkv_a2': nrm((L, A_AAA_LORA, A_WIDTH), 0.5 * A_AAA_LORA ** -0.5),
        'rwkv_g2': nrm((L, A_GATE_LORA, A_WIDTH), A_GATE_LORA ** -0.5),
        'rwkv_k_k': 0.85 + nrm((L, A_WIDTH), 0.05),
        'rwkv_k_a': 1.0 + nrm((L, A_WIDTH), 0.05),
        'rwkv_r_k': nrm((L, A_HEADS, HEAD_DIM), 0.1),
        'rwkv_ln_g': 1.0 + nrm((L, A_WIDTH), 0.02),
        'rwkv_ln_b': nrm((L, A_WIDTH), 0.02),
        'attn_q_norm_g': 1.0 + nrm((L, HEAD_DIM), 0.02),
        'attn_k_norm_g': 1.0 + nrm((L, HEAD_DIM), 0.02),
        'attn_sinks': nrm((L, B_HEADS), 0.5),
        'ssm_conv_w': nrm((L, SSM_CONV, CONV_DIM), SSM_CONV ** -0.5),
        'ssm_conv_b': nrm((L, CONV_DIM), 0.02),
        'ssm_dt_bias': dt0 + jnp.log(-jnp.expm1(-dt0)),
        'ssm_a_log': jnp.log(uni((L, C_HEADS), 1.0, 16.0)),
        'ssm_d': 1.0 + nrm((L, C_HEADS), 0.02),
        'ssm_norm_g': 1.0 + nrm((L, C_WIDTH), 0.02),
        'ffn_w_up': nrm((L, D_MODEL, 2 * D_FF), D_MODEL ** -0.5),
        'ffn_conv_w': nrm((L, FFN_CONV, D_FF), FFN_CONV ** -0.5),
        'ffn_conv_b': nrm((L, D_FF), 0.02),
        'ffn_w_down': nrm((L, D_FF, D_MODEL), D_FF ** -0.5),
    }


def reference(x_prompt, x_sample, c_prompt, c_sample,
              state_rwkv_shift, state_rwkv_wkv, cache_swa_k, cache_swa_v,
              state_ssm_conv, state_ssm, state_ffn_conv,
              ada_w, ada_b, norm_mix_g, norm_ffn_g, w_in, w_out,
              rwkv_mu, rwkv_w0, rwkv_w2, rwkv_a0, rwkv_a2, rwkv_g2,
              rwkv_k_k, rwkv_k_a, rwkv_r_k, rwkv_ln_g, rwkv_ln_b,
              attn_q_norm_g, attn_k_norm_g, attn_sinks,
              ssm_conv_w, ssm_conv_b, ssm_dt_bias, ssm_a_log, ssm_d, ssm_norm_g,
              ffn_w_up, ffn_conv_w, ffn_conv_b, ffn_w_down):
    nb = x_prompt.shape[0]
    fdt = x_prompt.dtype
    zp_shift = jnp.zeros((nb, A_PROJ), fdt)
    zp_wkv = jnp.zeros((nb, A_HEADS, HEAD_DIM, HEAD_DIM), F32)
    zp_sconv = jnp.zeros((nb, SSM_CONV - 1, CONV_DIM), fdt)
    zp_ssm = jnp.zeros((nb, C_HEADS, HEAD_DIM, D_STATE), F32)
    zp_ffn = jnp.zeros((nb, FFN_CONV - 1, D_FF), fdt)
    prompt_new = [[] for _ in range(7)]
    sample_new = [[] for _ in range(7)]
    yp, ys = x_prompt, x_sample
    for i in range(DEPTH):
        lp = {
            'ada_w': ada_w[i], 'ada_b': ada_b[i], 'norm_mix_g': norm_mix_g[i], 'norm_ffn_g': norm_ffn_g[i],
            'w_in': w_in[i], 'w_out': w_out[i],
            'rwkv_mu': rwkv_mu[i], 'rwkv_w0': rwkv_w0[i], 'rwkv_w2': rwkv_w2[i], 'rwkv_a0': rwkv_a0[i],
            'rwkv_a2': rwkv_a2[i], 'rwkv_g2': rwkv_g2[i], 'rwkv_k_k': rwkv_k_k[i], 'rwkv_k_a': rwkv_k_a[i],
            'rwkv_r_k': rwkv_r_k[i], 'rwkv_ln_g': rwkv_ln_g[i], 'rwkv_ln_b': rwkv_ln_b[i],
            'attn_q_norm_g': attn_q_norm_g[i], 'attn_k_norm_g': attn_k_norm_g[i], 'attn_sinks': attn_sinks[i],
            'ssm_conv_w': ssm_conv_w[i], 'ssm_conv_b': ssm_conv_b[i], 'ssm_dt_bias': ssm_dt_bias[i],
            'ssm_a_log': ssm_a_log[i], 'ssm_d': ssm_d[i], 'ssm_norm_g': ssm_norm_g[i],
            'ffn_w_up': ffn_w_up[i], 'ffn_conv_w': ffn_conv_w[i], 'ffn_conv_b': ffn_conv_b[i],
            'ffn_w_down': ffn_w_down[i],
        }
        yp, new_p = trunk_layer(yp, c_prompt, zp_shift, zp_wkv, None, None, zp_sconv, zp_ssm, zp_ffn,
                                lp, False)
        ys, new_s = trunk_layer(ys, c_sample, state_rwkv_shift[i], state_rwkv_wkv[i], cache_swa_k[i],
                                cache_swa_v[i], state_ssm_conv[i], state_ssm[i], state_ffn_conv[i], lp, True)
        for lst, arr in zip(prompt_new, new_p):
            lst.append(arr)
        for lst, arr in zip(sample_new, new_s):
            lst.append(arr)
    p_shift, p_wkv, p_k, p_v, p_conv, p_ssm, p_ffn = [jnp.stack(l) for l in prompt_new]
    s_shift, s_wkv, s_k, s_v, s_conv, s_ssm, s_ffn = [jnp.stack(l) for l in sample_new]
    return (yp, ys, p_shift, p_wkv, p_k, p_v, p_conv, p_ssm, p_ffn,
            s_shift, s_wkv, s_k, s_v, s_conv, s_ssm, s_ffn)
```

```python
import functools

import jax
import jax.numpy as jnp
from jax import lax
from jax.experimental import pallas as pl
from jax.experimental.pallas import tpu as pltpu

F32 = jnp.float32
BF16 = jnp.bfloat16

D_MODEL = 1024
HEAD_DIM = 64
A_WIDTH = 256
A_HEADS = A_WIDTH // HEAD_DIM
A_DECAY_LORA = 64
A_AAA_LORA = 64
A_GATE_LORA = 128
A_PROJ = 3 * A_WIDTH + A_DECAY_LORA + A_AAA_LORA + A_GATE_LORA
B_WIDTH = 256
B_HEADS = B_WIDTH // HEAD_DIM
B_KV_HEADS = 2
B_GROUP = B_HEADS // B_KV_HEADS
B_KV_WIDTH = B_KV_HEADS * HEAD_DIM
B_PROJ = B_WIDTH + 2 * B_KV_WIDTH
WINDOW = 128
C_WIDTH = 512
C_HEADS = C_WIDTH // HEAD_DIM
C_GROUPS = 2
D_STATE = 128
SSM_CONV = 4
CONV_DIM = C_WIDTH + 2 * C_GROUPS * D_STATE
C_PROJ = C_WIDTH + CONV_DIM + C_HEADS
PROJ = A_PROJ + B_PROJ + C_PROJ
D_FF = 2816
FFN_CONV = 3
NORM_EPS = 1e-6
GN_EPS = 64e-5
N_MOD = 6

LANES = 128
SUBLANES = 8
VMEM_LIMIT_BYTES = 56 * 1024 * 1024

DT_OFF = A_PROJ + B_PROJ + C_WIDTH + CONV_DIM
PROJ_PAD = DT_OFF + LANES
Z_OFF = A_PROJ + B_PROJ
XBC_OFF = Z_OFF + C_WIDTH

RWKV_CHUNK = 64
INV_BLOCK = 8
SSD_CHUNK = 128
SWA_BLOCK = 128
DEC_BLOCK = 8
FF_CHUNK = D_FF // 2


def _mm(a, b):
    return jnp.dot(a.astype(BF16), b.astype(BF16), preferred_element_type=F32)


def _mm_nt(a, b):
    return lax.dot_general(a.astype(BF16), b.astype(BF16), (((1,), (1,)), ((), ())), preferred_element_type=F32)


def _mm_tn(a, b):
    return lax.dot_general(a.astype(BF16), b.astype(BF16), (((0,), (0,)), ((), ())), preferred_element_type=F32)


def _split3(x):
    hi = x.astype(BF16)
    r1 = x - hi.astype(F32)
    mid = r1.astype(BF16)
    lo = (r1 - mid.astype(F32)).astype(BF16)
    return hi, mid, lo


def _mm_sel_left(sel, x):
    hi, mid, lo = _split3(x)
    s = sel.astype(BF16)
    d = lambda p: jnp.dot(s, p, preferred_element_type=F32)
    return (d(lo) + d(mid)) + d(hi)


def _mm_sel_right(x, sel):
    hi, mid, lo = _split3(x)
    s = sel.astype(BF16)
    d = lambda p: jnp.dot(p, s, preferred_element_type=F32)
    return (d(lo) + d(mid)) + d(hi)


def _mm_nt_sel_left(sel, x):
    hi, mid, lo = _split3(x)
    s = sel.astype(BF16)
    d = lambda p: lax.dot_general(s, p, (((1,), (1,)), ((), ())), preferred_element_type=F32)
    return (d(lo) + d(mid)) + d(hi)


def _sigmoid(x):
    return 1.0 / (1.0 + jnp.exp(-x))


def _silu(x):
    return x * _sigmoid(x)


def _softplus(x):
    return jnp.maximum(x, 0.0) + jnp.log(1.0 + jnp.exp(-jnp.abs(x)))


def _iota(shape, dim):
    return lax.broadcasted_iota(jnp.int32, shape, dim)


def _eye(n):
    return (_iota((n, n), 0) == _iota((n, n), 1)).astype(F32)


def _tril_ones(n):
    return (_iota((n, n), 0) >= _iota((n, n), 1)).astype(F32)


def _group_sum(x, group):
    gid = _iota(x.shape, x.ndim - 1) // group
    out = jnp.zeros_like(x)
    for h in range(x.shape[-1] // group):
        m = gid == h
        s = jnp.sum(jnp.where(m, x, 0.0), axis=-1, keepdims=True)
        out = jnp.where(m, s, out)
    return out


def _rms_norm(x, g):
    return x * lax.rsqrt(jnp.mean(x * x, axis=-1, keepdims=True) + NORM_EPS) * g


def _head_rms_norm(x, g):
    return x * lax.rsqrt(_group_sum(x * x, HEAD_DIM) * (1.0 / HEAD_DIM) + NORM_EPS) * g


def _shift_rows(x, k, tail):
    rolled = pltpu.roll(x, k, axis=0)
    row = _iota((SUBLANES, x.shape[1]), 0)
    top = jnp.where(row < k, pltpu.roll(tail, k, axis=0), rolled[0:SUBLANES])
    return jnp.concatenate([top, rolled[SUBLANES:]], axis=0)


def _ada_kernel(c_ref, w_ref, b_ref, o_ref):
    o_ref[...] = _mm(_silu(c_ref[...]), w_ref[...]) + b_ref[...]


def _ada(c_all, ada_w, ada_b):
    depth = ada_w.shape[0]
    rows = c_all.shape[0]
    width = N_MOD * D_MODEL
    tn = 1536
    return pl.pallas_call(
        _ada_kernel,
        grid=(depth, width // tn),
        in_specs=[
            pl.BlockSpec((rows, D_MODEL), lambda l, j: (0, 0)),
            pl.BlockSpec((None, D_MODEL, tn), lambda l, j: (l, 0, j)),
            pl.BlockSpec((None, 1, tn), lambda l, j: (l, 0, j)),
        ],
        out_specs=pl.BlockSpec((None, rows, tn), lambda l, j: (l, 0, j)),
        out_shape=jax.ShapeDtypeStruct((depth, rows, width), F32),
        compiler_params=pltpu.CompilerParams(
            dimension_semantics=("arbitrary", "arbitrary"), vmem_limit_bytes=VMEM_LIMIT_BYTES),
        name="ada_mod",
    )(c_all, ada_w, ada_b.reshape(depth, 1, width))


def _in_proj_kernel(x_ref, sh_ref, sc_ref, g_ref, w_ref, o_ref):
    h = _rms_norm(x_ref[...], g_ref[...]) * (1.0 + sc_ref[...]) + sh_ref[...]
    o_ref[...] = _mm(h, w_ref[...])


def _mod_spec(layer, per_row, rows_per_seq_tiles, col):
    if per_row:
        return pl.BlockSpec((None, per_row, D_MODEL), lambda i: (layer, 0, col))
    return pl.BlockSpec((None, None, 1, D_MODEL), lambda i: (layer, i // rows_per_seq_tiles, 0, col))


def _layer_spec(layer, tail, single_buffer=False):
    idx = lambda *_: (layer,) + (0,) * len(tail)
    if single_buffer:
        return pl.BlockSpec((None,) + tuple(tail), idx, pipeline_mode=pl.Buffered(1))
    return pl.BlockSpec((None,) + tuple(tail), idx)


def _in_proj(x, mod, norm_g, w_in, layer, tm, per_row, tiles_per_seq):
    rows = x.shape[0]
    return pl.pallas_call(
        _in_proj_kernel,
        grid=(rows // tm,),
        in_specs=[
            pl.BlockSpec((tm, D_MODEL), lambda i: (i, 0)),
            _mod_spec(layer, per_row, tiles_per_seq, 0),
            _mod_spec(layer, per_row, tiles_per_seq, 1),
            _layer_spec(layer, (1, D_MODEL)),
            _layer_spec(layer, (D_MODEL, PROJ_PAD), single_buffer=True),
        ],
        out_specs=pl.BlockSpec((tm, PROJ_PAD), lambda i: (i, 0)),
        out_shape=jax.ShapeDtypeStruct((rows, PROJ_PAD), F32),
        compiler_params=pltpu.CompilerParams(
            dimension_semantics=("arbitrary",), vmem_limit_bytes=VMEM_LIMIT_BYTES),
        name="in_proj",
    )(x, mod, mod, norm_g, w_in)


_RWKV_PARAMS = ("mu", "w0", "w2", "a0", "a2", "g2", "k_k", "k_a", "r_k", "ln_g", "ln_b")


def _rwkv_prep(xs, p):
    r = xs[:, 0:A_WIDTH]
    k = xs[:, A_WIDTH:2 * A_WIDTH]
    v = xs[:, 2 * A_WIDTH:3 * A_WIDTH]
    lora_in = xs[:, 3 * A_WIDTH:3 * A_WIDTH + LANES]
    lg = xs[:, 3 * A_WIDTH + LANES:A_PROJ]
    w_log = -_softplus(-(p["w0"] + _mm(jnp.tanh(lora_in), p["w2"]))) - 0.5
    log_decay = -jnp.exp(w_log)
    a = _sigmoid(p["a0"] + _mm(lora_in, p["a2"]))
    g = _mm(_sigmoid(lg), p["g2"])
    kk = k * p["k_k"]
    kk = kk / jnp.maximum(jnp.sqrt(_group_sum(kk * kk, HEAD_DIM)), 1e-12)
    k = k * (1.0 + (a - 1.0) * p["k_a"])
    return r, k, v, log_decay, -kk, kk * a, g


def _rwkv_post(y, r, k, v, g, p):
    mean = _group_sum(y, HEAD_DIM) * (1.0 / HEAD_DIM)
    c = y - mean
    var = _group_sum(c * c, HEAD_DIM) * (1.0 / HEAD_DIM)
    yn = c * lax.rsqrt(var + GN_EPS) * p["ln_g"] + p["ln_b"]
    bonus = _group_sum(r * k * p["r_k"], HEAD_DIM) * v
    return (yn + bonus) * g


def _block_diag(x, mask):
    return jnp.where(mask, jnp.concatenate([x] * (x.shape[1] // x.shape[0]), axis=0), jnp.zeros((), x.dtype))


def _rwkv_chunk(r, k, v, logd, av, bv, state, cst):
    bd_mask, strict, incl, eye_cat, tril, level_masks = cst
    cum = _mm_sel_left(tril, logd)
    excl = cum - logd
    cum_last = cum[RWKV_CHUNK - 1:RWKV_CHUNK, :]
    e_neg = jnp.exp(-cum)
    e_end = jnp.exp(cum_last - cum)
    at = (av * jnp.exp(excl)).astype(BF16)
    rt = (r * jnp.exp(cum)).astype(BF16)
    bt = (bv * e_neg).astype(BF16)
    kt = (k * e_neg).astype(BF16)
    bh = (bv * e_end).astype(BF16)
    kh = (k * e_end).astype(BF16)
    vb = v.astype(BF16)
    bd = lambda x: _block_diag(x, bd_mask)

    lhs = jnp.concatenate([at, rt], axis=0)
    rhs = jnp.concatenate([bd(bt), bd(kt)], axis=0)
    amat = lax.dot_general(lhs, rhs, (((1,), (1,)), ((), ())), preferred_element_type=F32)
    n_ab = jnp.where(strict, amat[0:RWKV_CHUNK, 0:A_WIDTH], 0.0)
    n_ak = jnp.where(strict, amat[0:RWKV_CHUNK, A_WIDTH:], 0.0).astype(BF16)
    n_rb = jnp.where(incl, amat[RWKV_CHUNK:, 0:A_WIDTH], 0.0).astype(BF16)
    n_rk = jnp.where(incl, amat[RWKV_CHUNK:, A_WIDTH:], 0.0).astype(BF16)

    inv = _unit_lower_inverse(n_ab, eye_cat, level_masks, bd)

    sb = state.astype(BF16)
    bdv = bd(vb)
    nt = lambda a, b: lax.dot_general(a, b, (((1,), (1,)), ((), ())), preferred_element_type=F32)
    w = nt(at, sb) + jnp.dot(n_ak, bdv, preferred_element_type=F32)
    u = jnp.dot(inv.astype(BF16), bd(w.astype(BF16)), preferred_element_type=F32)
    ub = u.astype(BF16)
    y = (nt(rt, sb) + jnp.dot(n_rb, bd(ub), preferred_element_type=F32)
         + jnp.dot(n_rk, bdv, preferred_element_type=F32))
    upd = lax.dot_general(jnp.concatenate([ub, vb], axis=0), jnp.concatenate([bh, kh], axis=0),
                          (((0,), (0,)), ((), ())), preferred_element_type=F32)
    new_state = state * jnp.exp(cum_last) + jnp.where(bd_mask, upd, 0.0)
    return y, new_state


def _mm_cat2(a, b, bd):
    a_hi = a.astype(BF16)
    a_lo = (a - a_hi.astype(F32)).astype(BF16)
    b_hi = b.astype(BF16)
    b_lo = (b - b_hi.astype(F32)).astype(BF16)
    rows = a.shape[0]
    both = jnp.dot(jnp.concatenate([a_hi, a_lo], axis=0), bd(b_hi), preferred_element_type=F32)
    return (both[rows:] + jnp.dot(a_hi, bd(b_lo), preferred_element_type=F32)) + both[:rows]


def _unit_lower_inverse(n, eye_cat, level_masks, bd):
    m = jnp.where(level_masks[0], n, 0.0)
    inv = eye_cat + m
    for _ in range(INV_BLOCK.bit_length() - 2):
        m = _mm_cat2(m, m, bd)
        inv = inv + _mm_cat2(inv, m, bd)
    for mask in level_masks[1:]:
        off = jnp.where(mask, n, 0.0).astype(BF16)
        inv_b = inv.astype(BF16)
        g = jnp.dot(inv_b, bd(off), preferred_element_type=F32)
        inv = inv + jnp.dot(g.astype(BF16), bd(inv_b), preferred_element_type=F32)
    return inv


def _rwkv_consts():
    w = A_WIDTH
    bd_mask = (_iota((w, w), 0) // HEAD_DIM) == (_iota((w, w), 1) // HEAD_DIM)
    t = _iota((RWKV_CHUNK, w), 0)
    s = _iota((RWKV_CHUNK, w), 1) % RWKV_CHUNK
    level_masks = [t // INV_BLOCK == s // INV_BLOCK]
    size = INV_BLOCK
    while size < RWKV_CHUNK:
        level_masks.append((t // (2 * size) == s // (2 * size)) & (t // size != s // size))
        size *= 2
    return bd_mask, s < t, s <= t, (s == t).astype(F32), _tril_ones(RWKV_CHUNK), level_masks


def _load_params(refs):
    return {name: ref[...] for name, ref in zip(_RWKV_PARAMS, refs)}


def _rwkv_prompt_kernel(pa_ref, *rest):
    prm_refs = rest[:len(_RWKV_PARAMS)]
    ya_ref, shift_ref, wkv_ref, carry_ref, state_ref, r_s, k_s, v_s, w_s, a_s, b_s, y_s = rest[len(_RWKV_PARAMS):]
    i = pl.program_id(0)
    nseq, tb, _ = pa_ref.shape

    @pl.when(i == 0)
    def _():
        carry_ref[...] = jnp.zeros_like(carry_ref)
        state_ref[...] = jnp.zeros_like(state_ref)

    p = _load_params(prm_refs)
    gates = []
    for n in range(nseq):
        pa = pa_ref[n]
        row = _iota(pa.shape, 0)
        prev = jnp.where(row == 0, carry_ref[n], pltpu.roll(pa, 1, axis=0))
        last = pa[tb - 1:tb, :]
        carry_ref[n] = last
        shift_ref[n] = last
        r, k, v, logd, av, bv, g = _rwkv_prep(pa + (prev - pa) * p["mu"], p)
        r_s[n], k_s[n], v_s[n], w_s[n], a_s[n], b_s[n] = r, k, v, logd, av, bv
        gates.append(g)

    cst = _rwkv_consts()

    def chunk_body(c, carry):
        rows = pl.ds(pl.multiple_of(c * RWKV_CHUNK, RWKV_CHUNK), RWKV_CHUNK)
        for n in range(nseq):
            y, new_state = _rwkv_chunk(r_s[n, rows, :], k_s[n, rows, :], v_s[n, rows, :], w_s[n, rows, :],
                                       a_s[n, rows, :], b_s[n, rows, :], state_ref[n], cst)
            y_s[n, rows, :] = y
            state_ref[n] = new_state
        return carry

    lax.fori_loop(0, tb // RWKV_CHUNK, chunk_body, 0)

    for n in range(nseq):
        ya_ref[n] = _rwkv_post(y_s[n], r_s[n], k_s[n], v_s[n], gates[n], p)

    @pl.when(i == pl.num_programs(0) - 1)
    def _():
        for n in range(nseq):
            for h in range(A_HEADS):
                sl = slice(h * HEAD_DIM, (h + 1) * HEAD_DIM)
                wkv_ref[n, h] = state_ref[n, sl, sl]


def _rwkv_param_specs(layer):
    shapes = {"mu": (1, A_PROJ), "w2": (LANES, A_WIDTH), "a2": (LANES, A_WIDTH), "g2": (A_GATE_LORA, A_WIDTH)}
    return [_layer_spec(layer, shapes.get(name, (1, A_WIDTH))) for name in _RWKV_PARAMS]


def _rwkv_prompt(proj, prm, layer, tb):
    nseq, t, _ = proj.shape
    seq_block = lambda w: pltpu.VMEM((nseq, tb, w), F32)
    return pl.pallas_call(
        _rwkv_prompt_kernel,
        grid=(t // tb,),
        in_specs=[pl.BlockSpec((nseq, tb, A_PROJ), lambda i: (0, i, 0))] + _rwkv_param_specs(layer),
        out_specs=[
            pl.BlockSpec((nseq, tb, A_WIDTH), lambda i: (0, i, 0)),
            pl.BlockSpec((nseq, 1, A_PROJ), lambda i: (0, 0, 0)),
            pl.BlockSpec((nseq, A_HEADS, HEAD_DIM, HEAD_DIM), lambda i: (0, 0, 0, 0)),
        ],
        out_shape=[
            jax.ShapeDtypeStruct((nseq, t, A_WIDTH), F32),
            jax.ShapeDtypeStruct((nseq, 1, A_PROJ), F32),
            jax.ShapeDtypeStruct((nseq, A_HEADS, HEAD_DIM, HEAD_DIM), F32),
        ],
        scratch_shapes=[pltpu.VMEM((nseq, 1, A_PROJ), F32), pltpu.VMEM((nseq, A_WIDTH, A_WIDTH), F32)]
        + [seq_block(A_WIDTH) for _ in range(7)],
        compiler_params=pltpu.CompilerParams(
            dimension_semantics=("arbitrary",), vmem_limit_bytes=VMEM_LIMIT_BYTES),
        name="rwkv_prompt",
    )(proj, *[prm[name] for name in _RWKV_PARAMS])


def _rwkv_decode_kernel(pa_ref, shift_ref, wkv0_ref, *rest):
    prm_refs = rest[:len(_RWKV_PARAMS)]
    ya_ref, wkv_ref = rest[len(_RWKV_PARAMS):]
    p = _load_params(prm_refs)
    pa = pa_ref[...]
    nb = pa.shape[0]
    r, k, v, logd, av, bv, g = _rwkv_prep(pa + (shift_ref[...] - pa) * p["mu"], p)
    decay = jnp.exp(logd)
    v_t = _mm_nt_sel_left(_eye(A_WIDTH), v)
    lane = _iota((A_WIDTH, nb), 1)
    y_t = jnp.zeros((A_WIDTH, nb), F32)
    for j in range(nb):
        cols = []
        for h in range(A_HEADS):
            sl = slice(h * HEAD_DIM, (h + 1) * HEAD_DIM)
            s = wkv0_ref[j, h]
            u = jnp.sum(s * av[j:j + 1, sl], axis=1, keepdims=True)
            s = s * decay[j:j + 1, sl] + u * bv[j:j + 1, sl] + v_t[sl, j:j + 1] * k[j:j + 1, sl]
            wkv_ref[j, h] = s
            cols.append(jnp.sum(s * r[j:j + 1, sl], axis=1, keepdims=True))
        y_t = jnp.where(lane == j, jnp.concatenate(cols, axis=0), y_t)
    y = _mm_nt_sel_left(_eye(nb), y_t)
    ya_ref[...] = _rwkv_post(y, r, k, v, g, p)


def _rwkv_decode(proj, shift_prev, wkv0, prm, layer):
    nd = proj.shape[0]
    nb = DEC_BLOCK
    return pl.pallas_call(
        _rwkv_decode_kernel,
        grid=(nd // nb,),
        in_specs=[
            pl.BlockSpec((nb, A_PROJ), lambda i: (i, 0)),
            pl.BlockSpec((None, nb, A_PROJ), lambda i: (layer, i, 0)),
            pl.BlockSpec((None, nb, A_HEADS, HEAD_DIM, HEAD_DIM), lambda i: (layer, i, 0, 0, 0)),
        ] + _rwkv_param_specs(layer),
        out_specs=[
            pl.BlockSpec((nb, A_WIDTH), lambda i: (i, 0)),
            pl.BlockSpec((nb, A_HEADS, HEAD_DIM, HEAD_DIM), lambda i: (i, 0, 0, 0)),
        ],
        out_shape=[
            jax.ShapeDtypeStruct((nd, A_WIDTH), F32),
            jax.ShapeDtypeStruct((nd, A_HEADS, HEAD_DIM, HEAD_DIM), F32),
        ],
        compiler_params=pltpu.CompilerParams(
            dimension_semantics=("arbitrary",), vmem_limit_bytes=VMEM_LIMIT_BYTES),
        name="rwkv_decode",
    )(proj, shift_prev, wkv0, *[prm[name] for name in _RWKV_PARAMS])


def _place_q_heads(qn):
    lane_half = _iota((qn.shape[0], B_KV_WIDTH), 1) // HEAD_DIM
    placed = []
    for h in range(B_HEADS):
        kh, g = divmod(h, B_GROUP)
        chunk = qn[:, kh * B_KV_WIDTH:(kh + 1) * B_KV_WIDTH]
        if g != kh:
            chunk = pltpu.roll(chunk, HEAD_DIM, axis=1)
        placed.append(jnp.where(lane_half == kh, chunk, 0.0))
    return placed


def _gather_o_heads(o_heads):
    lane_half = _iota(o_heads[0].shape, 1) // HEAD_DIM
    chunks = []
    for kh in range(B_KV_HEADS):
        parts = []
        for g in range(B_GROUP):
            o = o_heads[kh * B_GROUP + g]
            parts.append(o if g == kh else pltpu.roll(o, HEAD_DIM, axis=1))
        chunks.append(jnp.where(lane_half == 0, parts[0], parts[1]))
    return jnp.concatenate(chunks, axis=1)


def _swa_prompt_kernel(layer, sinks_ref, pb_ref, gq_ref, gk_ref, yb_ref, nk_ref, nv_ref, kprev_ref, vprev_ref):
    j = pl.program_id(1)
    tq = pb_ref.shape[0]
    blk = SWA_BLOCK

    @pl.when(j == 0)
    def _():
        kprev_ref[...] = jnp.zeros_like(kprev_ref)
        vprev_ref[...] = jnp.zeros_like(vprev_ref)

    scale = HEAD_DIM ** -0.5
    rows = _iota((B_GROUP * blk, 2 * blk), 0)
    cols = _iota((B_GROUP * blk, 2 * blk), 1)
    qi = rows % blk
    band = (cols >= qi) & (cols <= qi + WINDOW)
    for sb in range(tq // blk):
        x = pb_ref[sb * blk:(sb + 1) * blk, :]
        qn = _head_rms_norm(x[:, 0:B_WIDTH], gq_ref[...])
        kn = _head_rms_norm(x[:, B_WIDTH:B_WIDTH + B_KV_WIDTH], gk_ref[...])
        v = x[:, B_WIDTH + B_KV_WIDTH:B_PROJ]
        kcat = jnp.concatenate([kprev_ref[...], kn], axis=0)
        vcat = jnp.concatenate([vprev_ref[...], v], axis=0)
        first_key = jnp.where(j * (tq // blk) + sb > 0, 0, blk)
        mask = band & (cols >= first_key)
        placed = _place_q_heads(qn)
        outs = [None] * B_HEADS
        for kh in range(B_KV_HEADS):
            q2 = jnp.concatenate(placed[kh * B_GROUP:(kh + 1) * B_GROUP], axis=0)
            s = jnp.where(mask, _mm_nt(q2, kcat) * scale, -jnp.inf)
            sink = jnp.where(rows[:, 0:1] < blk, sinks_ref[layer, kh * B_GROUP], sinks_ref[layer, kh * B_GROUP + 1])
            m = jnp.maximum(jnp.max(s, axis=-1, keepdims=True), sink)
            pr = jnp.exp(s - m)
            den = jnp.sum(pr, axis=-1, keepdims=True) + jnp.exp(sink - m)
            o = _mm(pr, vcat) / den
            for g in range(B_GROUP):
                outs[kh * B_GROUP + g] = o[g * blk:(g + 1) * blk]
        yb_ref[sb * blk:(sb + 1) * blk, :] = _gather_o_heads(outs)
        kprev_ref[...] = kn
        vprev_ref[...] = v
        if sb == tq // blk - 1:
            nk_ref[...] = kn
            nv_ref[...] = v


def _swa_prompt(proj, gq, gk, sinks, layer, tq):
    nseq, t, _ = proj.shape
    col = A_PROJ // B_PROJ
    return pl.pallas_call(
        functools.partial(_swa_prompt_kernel, layer),
        grid=(nseq, t // tq),
        in_specs=[
            pl.BlockSpec(memory_space=pltpu.SMEM),
            pl.BlockSpec((None, tq, B_PROJ), lambda n, j: (n, j, col)),
            _layer_spec(layer, (1, B_WIDTH)),
            _layer_spec(layer, (1, B_KV_WIDTH)),
        ],
        out_specs=[
            pl.BlockSpec((None, tq, B_WIDTH), lambda n, j: (n, j, 0)),
            pl.BlockSpec((None, WINDOW, B_KV_WIDTH), lambda n, j: (n, 0, 0)),
            pl.BlockSpec((None, WINDOW, B_KV_WIDTH), lambda n, j: (n, 0, 0)),
        ],
        out_shape=[
            jax.ShapeDtypeStruct((nseq, t, B_WIDTH), F32),
            jax.ShapeDtypeStruct((nseq, WINDOW, B_KV_WIDTH), F32),
            jax.ShapeDtypeStruct((nseq, WINDOW, B_KV_WIDTH), F32),
        ],
        scratch_shapes=[pltpu.VMEM((SWA_BLOCK, B_KV_WIDTH), F32), pltpu.VMEM((SWA_BLOCK, B_KV_WIDTH), F32)],
        compiler_params=pltpu.CompilerParams(
            dimension_semantics=("arbitrary", "arbitrary"), vmem_limit_bytes=VMEM_LIMIT_BYTES),
        name="swa_prompt",
    )(sinks, proj, gq, gk)


def _swa_decode_kernel(layer, sinks_ref, pb_ref, ck_ref, cv_ref, gq_ref, gk_ref, yb_ref, nk_ref, nv_ref):
    x = pb_ref[...]
    nb = x.shape[0]
    qn = _head_rms_norm(x[:, 0:B_WIDTH], gq_ref[...])
    kn = _head_rms_norm(x[:, B_WIDTH:B_WIDTH + B_KV_WIDTH], gk_ref[...])
    v = x[:, B_WIDTH + B_KV_WIDTH:B_PROJ]
    placed = _place_q_heads(qn)
    scale = HEAD_DIM ** -0.5
    head = _iota((B_HEADS, 1), 0)
    sink = jnp.zeros((B_HEADS, 1), F32)
    for h in range(B_HEADS):
        sink = jnp.where(head == h, sinks_ref[layer, h], sink)
    last_row = _iota((WINDOW, B_KV_WIDTH), 0) == WINDOW - 1
    o_rows = []
    for j in range(nb):
        kc = ck_ref[j]
        vc = cv_ref[j]
        q4 = jnp.concatenate([pl_h[j:j + 1] for pl_h in placed], axis=0)
        s = _mm_nt(q4, kc) * scale
        s_new = jnp.sum(q4 * kn[j:j + 1], axis=-1, keepdims=True) * scale
        m = jnp.maximum(jnp.maximum(jnp.max(s, axis=-1, keepdims=True), s_new), sink)
        pr = jnp.exp(s - m)
        p_new = jnp.exp(s_new - m)
        den = jnp.sum(pr, axis=-1, keepdims=True) + p_new + jnp.exp(sink - m)
        o_rows.append((_mm(pr, vc) + p_new * v[j:j + 1]) / den)
        nk_ref[j] = jnp.where(last_row, kn[j:j + 1], pltpu.roll(kc, WINDOW - 1, axis=0))
        nv_ref[j] = jnp.where(last_row, v[j:j + 1], pltpu.roll(vc, WINDOW - 1, axis=0))
    outs = [jnp.concatenate([o[h:h + 1] for o in o_rows], axis=0) for h in range(B_HEADS)]
    yb_ref[...] = _gather_o_heads(outs)


def _swa_decode(proj, cache_k, cache_v, gq, gk, sinks, layer):
    nd = proj.shape[0]
    nb = DEC_BLOCK
    col = A_PROJ // B_PROJ
    cache_spec = pl.BlockSpec((None, nb, WINDOW, B_KV_WIDTH), lambda i: (layer, i, 0, 0))
    new_spec = pl.BlockSpec((nb, WINDOW, B_KV_WIDTH), lambda i: (i, 0, 0))
    return pl.pallas_call(
        functools.partial(_swa_decode_kernel, layer),
        grid=(nd // nb,),
        in_specs=[
            pl.BlockSpec(memory_space=pltpu.SMEM),
            pl.BlockSpec((nb, B_PROJ), lambda i: (i, col)),
            cache_spec, cache_spec,
            _layer_spec(layer, (1, B_WIDTH)),
            _layer_spec(layer, (1, B_KV_WIDTH)),
        ],
        out_specs=[pl.BlockSpec((nb, B_WIDTH), lambda i: (i, 0)), new_spec, new_spec],
        out_shape=[
            jax.ShapeDtypeStruct((nd, B_WIDTH), F32),
            jax.ShapeDtypeStruct((nd, WINDOW, B_KV_WIDTH), F32),
            jax.ShapeDtypeStruct((nd, WINDOW, B_KV_WIDTH), F32),
        ],
        compiler_params=pltpu.CompilerParams(
            dimension_semantics=("arbitrary",), vmem_limit_bytes=VMEM_LIMIT_BYTES),
        name="swa_decode",
    )(sinks, proj, cache_k, cache_v, gq, gk)


def _head_expand(width_per_head):
    n = C_HEADS * width_per_head
    return (_iota((LANES, n), 0) == _iota((LANES, n), 1) // width_per_head).astype(F32)


def _ssd_prompt_kernel(z_ref, xbc_ref, dt_ref, cw_ref, cb_ref, dtb_ref, aneg_ref, dskip_ref, ng_ref,
                       yc_ref, tail_ref, hout_ref, carry_ref, h_ref, xdt_s, b_s, c_s, da_s, y_s):
    j = pl.program_id(1)
    tb = xbc_ref.shape[0]

    @pl.when(j == 0)
    def _():
        carry_ref[...] = jnp.zeros_like(carry_ref)
        h_ref[...] = jnp.zeros_like(h_ref)

    xbc = xbc_ref[...]
    tail = carry_ref[...]
    cw = cw_ref[...]
    conv = cb_ref[...] + cw[SSM_CONV - 1:SSM_CONV] * xbc
    for k in range(1, SSM_CONV):
        conv = conv + cw[SSM_CONV - 1 - k:SSM_CONV - k] * _shift_rows(xbc, k, tail)
    last_rows = xbc[tb - SUBLANES:tb, :]
    carry_ref[...] = last_rows
    tail_ref[...] = last_rows
    act = _silu(conv)
    x = act[:, 0:C_WIDTH]
    dt = _softplus(dt_ref[...] + dtb_ref[...])
    xdt_s[...] = x * _mm_sel_right(dt, _head_expand(HEAD_DIM))
    b_s[...] = act[:, C_WIDTH:C_WIDTH + C_GROUPS * D_STATE]
    c_s[...] = act[:, C_WIDTH + C_GROUPS * D_STATE:CONV_DIM]
    da_s[...] = dt * aneg_ref[...]

    ch = SSD_CHUNK
    tril = _tril_ones(ch)
    causal = _iota((ch, ch), 0) >= _iota((ch, ch), 1)
    lane_lo = _iota((ch, 2 * HEAD_DIM), 1) < HEAD_DIM
    row_lo = _iota((2 * HEAD_DIM, D_STATE), 0) < HEAD_DIM
    heads_per_group = C_HEADS // C_GROUPS

    def chunk_body(c, carry):
        rows = pl.ds(pl.multiple_of(c * ch, ch), ch)
        acum = _mm_sel_left(tril, da_s[rows, :])
        acum_t = acum.T
        a_last = acum[ch - 1:ch, :]
        xdt = xdt_s[rows, :]
        cbs = []
        for g in range(C_GROUPS):
            gs = slice(g * D_STATE, (g + 1) * D_STATE)
            cbs.append(_mm_nt(c_s[rows, gs], b_s[rows, gs]))
        pairs = []
        for pr in range(C_HEADS // 2):
            g = (2 * pr) // heads_per_group
            gs = slice(g * D_STATE, (g + 1) * D_STATE)
            ps = slice(pr * 2 * HEAD_DIM, (pr + 1) * 2 * HEAD_DIM)
            xp = xdt[:, ps]
            per_head = []
            for e in (2 * pr, 2 * pr + 1):
                col = acum[:, e:e + 1]
                seg = col - acum_t[e:e + 1, :]
                lmat = jnp.exp(jnp.where(causal, seg, -jnp.inf))
                last = a_last[:, e:e + 1]
                per_head.append((_mm(cbs[g] * lmat, xp), jnp.exp(last - col), jnp.exp(col), jnp.exp(last)))
            (yd0, dte0, ea0, cd0), (yd1, dte1, ea1, cd1) = per_head
            hp = h_ref[ps, :]
            y_off = _mm_nt(c_s[rows, gs], hp) * jnp.where(lane_lo, ea0, ea1)
            states = _mm_tn(xp * jnp.where(lane_lo, dte0, dte1), b_s[rows, gs])
            h_ref[ps, :] = hp * jnp.where(row_lo, cd0, cd1) + states
            pairs.append(jnp.where(lane_lo, yd0, yd1) + y_off)
        y_s[rows, :] = jnp.concatenate(pairs, axis=1)
        return carry

    lax.fori_loop(0, tb // ch, chunk_body, 0)

    y = (y_s[...] + dskip_ref[...] * x) * _silu(z_ref[...])
    yc_ref[...] = _rms_norm(y, ng_ref[...])
    hout_ref[...] = h_ref[...]


def _ssd_prompt(proj, prm, layer, tb):
    nseq, t, _ = proj.shape
    return pl.pallas_call(
        _ssd_prompt_kernel,
        grid=(nseq, t // tb),
        in_specs=[
            pl.BlockSpec((None, tb, C_WIDTH), lambda n, j: (n, j, Z_OFF // C_WIDTH)),
            pl.BlockSpec((None, tb, CONV_DIM), lambda n, j: (n, j, XBC_OFF // CONV_DIM)),
            pl.BlockSpec((None, tb, LANES), lambda n, j: (n, j, DT_OFF // LANES)),
            _layer_spec(layer, (SSM_CONV, CONV_DIM)),
            _layer_spec(layer, (1, CONV_DIM)),
            _layer_spec(layer, (1, LANES)),
            _layer_spec(layer, (1, LANES)),
            _layer_spec(layer, (1, C_WIDTH)),
            _layer_spec(layer, (1, C_WIDTH)),
        ],
        out_specs=[
            pl.BlockSpec((None, tb, C_WIDTH), lambda n, j: (n, j, 0)),
            pl.BlockSpec((None, SUBLANES, CONV_DIM), lambda n, j: (n, 0, 0)),
            pl.BlockSpec((None, C_WIDTH, D_STATE), lambda n, j: (n, 0, 0)),
        ],
        out_shape=[
            jax.ShapeDtypeStruct((nseq, t, C_WIDTH), F32),
            jax.ShapeDtypeStruct((nseq, SUBLANES, CONV_DIM), F32),
            jax.ShapeDtypeStruct((nseq, C_WIDTH, D_STATE), F32),
        ],
        scratch_shapes=[
            pltpu.VMEM((SUBLANES, CONV_DIM), F32),
            pltpu.VMEM((C_WIDTH, D_STATE), F32),
            pltpu.VMEM((tb, C_WIDTH), F32),
            pltpu.VMEM((tb, C_GROUPS * D_STATE), F32),
            pltpu.VMEM((tb, C_GROUPS * D_STATE), F32),
            pltpu.VMEM((tb, LANES), F32),
            pltpu.VMEM((tb, C_WIDTH), F32),
        ],
        compiler_params=pltpu.CompilerParams(
            dimension_semantics=("arbitrary", "arbitrary"), vmem_limit_bytes=VMEM_LIMIT_BYTES),
        name="ssd_prompt",
    )(proj, proj, proj, prm["conv_w"], prm["conv_b"], prm["dt_bias"], prm["a_neg"], prm["d_skip"], prm["norm_g"])


def _ssd_decode_kernel(z_ref, xbc_ref, dt_ref, buf_ref, h0_ref, cw_ref, cb_ref, dtb_ref, aneg_ref, dskip_ref, ng_ref,
                       yc_ref, nbuf_ref, hout_ref):
    xbc = xbc_ref[...]
    nb = xbc.shape[0]
    cw = cw_ref[...]
    conv = cb_ref[...] + cw[SSM_CONV - 1:SSM_CONV] * xbc
    for k in range(SSM_CONV - 1):
        conv = conv + cw[k:k + 1] * buf_ref[k]
        nbuf_ref[k] = buf_ref[k + 1] if k + 1 < SSM_CONV - 1 else xbc
    act = _silu(conv)
    x = act[:, 0:C_WIDTH]
    bm = act[:, C_WIDTH:C_WIDTH + C_GROUPS * D_STATE]
    cm = act[:, C_WIDTH + C_GROUPS * D_STATE:CONV_DIM]
    dt = _softplus(dt_ref[...] + dtb_ref[...])
    decay = _mm_sel_right(jnp.exp(dt * aneg_ref[...]), _head_expand(D_STATE))
    xdt = x * _mm_sel_right(dt, _head_expand(HEAD_DIM))
    eye = _eye(LANES)
    xdt_t = [_mm_nt_sel_left(eye, xdt[:, c * LANES:(c + 1) * LANES]) for c in range(C_WIDTH // LANES)]
    lane = _iota((C_WIDTH, nb), 1)
    heads_per_group = C_HEADS // C_GROUPS
    heads_per_tile = LANES // HEAD_DIM
    y_t = jnp.zeros((C_WIDTH, nb), F32)
    for j in range(nb):
        cols = []
        for e in range(C_HEADS):
            gs = slice((e // heads_per_group) * D_STATE, (e // heads_per_group + 1) * D_STATE)
            hs = slice(e * HEAD_DIM, (e + 1) * HEAD_DIM)
            sub = slice((e % heads_per_tile) * HEAD_DIM, (e % heads_per_tile + 1) * HEAD_DIM)
            h = (h0_ref[j, hs, :] * decay[j:j + 1, e * D_STATE:(e + 1) * D_STATE]
                 + xdt_t[e // heads_per_tile][sub, j:j + 1] * bm[j:j + 1, gs])
            hout_ref[j, hs, :] = h
            cols.append(jnp.sum(h * cm[j:j + 1, gs], axis=1, keepdims=True))
        y_t = jnp.where(lane == j, jnp.concatenate(cols, axis=0), y_t)
    y = _mm_nt_sel_left(_eye(nb), y_t)
    y = (y + dskip_ref[...] * x) * _silu(z_ref[...])
    yc_ref[...] = _rms_norm(y, ng_ref[...])


def _ssd_decode(proj, conv_buf_t, h0, prm, layer):
    nd = proj.shape[0]
    nb = DEC_BLOCK
    return pl.pallas_call(
        _ssd_decode_kernel,
        grid=(nd // nb,),
        in_specs=[
            pl.BlockSpec((nb, C_WIDTH), lambda i: (i, Z_OFF // C_WIDTH)),
            pl.BlockSpec((nb, CONV_DIM), lambda i: (i, XBC_OFF // CONV_DIM)),
            pl.BlockSpec((nb, LANES), lambda i: (i, DT_OFF // LANES)),
            pl.BlockSpec((None, SSM_CONV - 1, nb, CONV_DIM), lambda i: (layer, 0, i, 0)),
            pl.BlockSpec((None, nb, C_WIDTH, D_STATE), lambda i: (layer, i, 0, 0)),
            _layer_spec(layer, (SSM_CONV, CONV_DIM)),
            _layer_spec(layer, (1, CONV_DIM)),
            _layer_spec(layer, (1, LANES)),
            _layer_spec(layer, (1, LANES)),
            _layer_spec(layer, (1, C_WIDTH)),
            _layer_spec(layer, (1, C_WIDTH)),
        ],
        out_specs=[
            pl.BlockSpec((nb, C_WIDTH), lambda i: (i, 0)),
            pl.BlockSpec((SSM_CONV - 1, nb, CONV_DIM), lambda i: (0, i, 0)),
            pl.BlockSpec((nb, C_WIDTH, D_STATE), lambda i: (i, 0, 0)),
        ],
        out_shape=[
            jax.ShapeDtypeStruct((nd, C_WIDTH), F32),
            jax.ShapeDtypeStruct((SSM_CONV - 1, nd, CONV_DIM), F32),
            jax.ShapeDtypeStruct((nd, C_WIDTH, D_STATE), F32),
        ],
        compiler_params=pltpu.CompilerParams(
            dimension_semantics=("arbitrary",), vmem_limit_bytes=VMEM_LIMIT_BYTES),
        name="ssd_decode",
    )(proj, proj, proj, conv_buf_t, h0, prm["conv_w"], prm["conv_b"], prm["dt_bias"], prm["a_neg"],
      prm["d_skip"], prm["norm_g"])


def _out_ffn_core(x, mix, g1, sh2, sc2, g2, gn, wo_ref, wu_ref, cw_ref, cb_ref, wd_ref, conv_fn):
    x1 = x + g1 * _mm(mix, wo_ref[...])
    h = (_rms_norm(x1, gn) * (1.0 + sc2) + sh2).astype(BF16)
    acc = jnp.zeros(x.shape, F32)
    for c in range(D_FF // FF_CHUNK):
        cs = slice(c * FF_CHUNK, (c + 1) * FF_CHUNK)
        gate = jnp.dot(h, wu_ref[:, cs], preferred_element_type=F32)
        val = jnp.dot(h, wu_ref[:, D_FF + c * FF_CHUNK:D_FF + (c + 1) * FF_CHUNK], preferred_element_type=F32)
        gate = conv_fn(gate, cs, cw_ref[:, cs], cb_ref[:, cs])
        acc = acc + _mm(_silu(gate) * val, wd_ref[cs, :])
    return x1 + g2 * acc


def _out_ffn_prompt_kernel(tiles_per_seq, x_ref, ya_ref, yb_ref, yc_ref, g1_ref, sh2_ref, sc2_ref, g2_ref, gn_ref,
                           wo_ref, wu_ref, cw_ref, cb_ref, wd_ref, o_ref, tail_ref, carry_ref):
    i = pl.program_id(0)
    tm = x_ref.shape[0]

    @pl.when(i % tiles_per_seq == 0)
    def _():
        carry_ref[...] = jnp.zeros_like(carry_ref)

    def conv_fn(gate, cs, cw, cb):
        tail = carry_ref[:, cs]
        out = cb + cw[FFN_CONV - 1:FFN_CONV] * gate
        for k in range(1, FFN_CONV):
            out = out + cw[FFN_CONV - 1 - k:FFN_CONV - k] * _shift_rows(gate, k, tail)
        last_rows = gate[tm - SUBLANES:tm, :]
        carry_ref[:, cs] = last_rows
        tail_ref[:, cs] = last_rows
        return out

    mix = jnp.concatenate([ya_ref[...], yb_ref[...], yc_ref[...]], axis=1)
    o_ref[...] = _out_ffn_core(x_ref[...], mix, g1_ref[...], sh2_ref[...], sc2_ref[...], g2_ref[...], gn_ref[...],
                               wo_ref, wu_ref, cw_ref, cb_ref, wd_ref, conv_fn)


def _out_ffn_decode_kernel(x_ref, ya_ref, yb_ref, yc_ref, g1_ref, sh2_ref, sc2_ref, g2_ref, gn_ref,
                           wo_ref, wu_ref, cw_ref, cb_ref, wd_ref, buf_ref, o_ref, nbuf_ref):
    def conv_fn(gate, cs, cw, cb):
        out = cb + cw[FFN_CONV - 1:FFN_CONV] * gate
        for k in range(FFN_CONV - 1):
            out = out + cw[k:k + 1] * buf_ref[k, :, cs]
            nbuf_ref[k, :, cs] = buf_ref[k + 1, :, cs] if k + 1 < FFN_CONV - 1 else gate
        return out

    mix = jnp.concatenate([ya_ref[...], yb_ref[...], yc_ref[...]], axis=1)
    o_ref[...] = _out_ffn_core(x_ref[...], mix, g1_ref[...], sh2_ref[...], sc2_ref[...], g2_ref[...], gn_ref[...],
                               wo_ref, wu_ref, cw_ref, cb_ref, wd_ref, conv_fn)


def _out_ffn(x, ya, yb, yc, mod, prm, layer, tm, per_row, tiles_per_seq, conv_buf_t=None):
    rows = x.shape[0]
    nseq = rows // (tm * tiles_per_seq) if not per_row else None
    row_spec = lambda w: pl.BlockSpec((tm, w), lambda i: (i, 0))
    in_specs = [row_spec(D_MODEL), row_spec(A_WIDTH), row_spec(B_WIDTH), row_spec(C_WIDTH)]
    in_specs += [_mod_spec(layer, per_row, tiles_per_seq, c) for c in (2, 3, 4, 5)]
    in_specs += [
        _layer_spec(layer, (1, D_MODEL)),
        _layer_spec(layer, (D_MODEL, D_MODEL), single_buffer=True),
        _layer_spec(layer, (D_MODEL, 2 * D_FF), single_buffer=True),
        _layer_spec(layer, (FFN_CONV, D_FF)),
        _layer_spec(layer, (1, D_FF)),
        _layer_spec(layer, (D_FF, D_MODEL), single_buffer=True),
    ]
    args = [x, ya, yb, yc, mod, mod, mod, mod, prm["norm_ffn_g"], prm["w_out"], prm["w_up"], prm["ffn_conv_w"],
            prm["ffn_conv_b"], prm["w_down"]]
    params = pltpu.CompilerParams(dimension_semantics=("arbitrary",), vmem_limit_bytes=VMEM_LIMIT_BYTES)
    if per_row:
        buf_shape = (FFN_CONV - 1, rows, D_FF)
        return pl.pallas_call(
            _out_ffn_decode_kernel,
            grid=(rows // tm,),
            in_specs=in_specs + [pl.BlockSpec((None,) + buf_shape, lambda i: (layer, 0, i, 0))],
            out_specs=[row_spec(D_MODEL), pl.BlockSpec(buf_shape, lambda i: (0, i, 0))],
            out_shape=[jax.ShapeDtypeStruct((rows, D_MODEL), F32), jax.ShapeDtypeStruct(buf_shape, F32)],
            compiler_params=params,
            name="out_ffn_decode",
        )(*args, conv_buf_t)
    return pl.pallas_call(
        functools.partial(_out_ffn_prompt_kernel, tiles_per_seq),
        grid=(rows // tm,),
        in_specs=in_specs,
        out_specs=[row_spec(D_MODEL), pl.BlockSpec((None, SUBLANES, D_FF), lambda i: (i // tiles_per_seq, 0, 0))],
        out_shape=[jax.ShapeDtypeStruct((rows, D_MODEL), F32), jax.ShapeDtypeStruct((nseq, SUBLANES, D_FF), F32)],
        scratch_shapes=[pltpu.VMEM((SUBLANES, D_FF), F32)],
        compiler_params=params,
        name="out_ffn_prompt",
    )(*args)


def _row_param(p, width=None):
    if width is not None and p.shape[-1] != width:
        p = jnp.pad(p, ((0, 0), (0, width - p.shape[-1])))
    return p.reshape(p.shape[0], 1, p.shape[-1])


def _largest_tile(total, cap, quantum):
    tile = min(total, cap)
    while total % tile or tile % quantum:
        tile -= quantum
    return tile


def kernel(x_prompt, x_sample, c_prompt, c_sample, state_rwkv_shift, state_rwkv_wkv, cache_swa_k, cache_swa_v, state_ssm_conv, state_ssm, state_ffn_conv, ada_w, ada_b, norm_mix_g, norm_ffn_g, w_in, w_out, rwkv_mu, rwkv_w0, rwkv_w2, rwkv_a0, rwkv_a2, rwkv_g2, rwkv_k_k, rwkv_k_a, rwkv_r_k, rwkv_ln_g, rwkv_ln_b, attn_q_norm_g, attn_k_norm_g, attn_sinks, ssm_conv_w, ssm_conv_b, ssm_dt_bias, ssm_a_log, ssm_d, ssm_norm_g, ffn_w_up, ffn_conv_w, ffn_conv_b, ffn_w_down):
    depth = w_in.shape[0]
    nseq, t, _ = x_prompt.shape
    nd = x_sample.shape[0]
    assert x_sample.shape[1] == 1 and nd % DEC_BLOCK == 0 and t % SSD_CHUNK == 0

    tm = _largest_tile(t, 512, SUBLANES)
    tiles_per_seq = t // tm
    tb_rwkv = _largest_tile(t, 256, RWKV_CHUNK)
    tq_swa = _largest_tile(t, 512, SWA_BLOCK)
    tb_ssd = _largest_tile(t, 512, SSD_CHUNK)

    w_in_p = jnp.pad(w_in, ((0, 0), (0, 0), (0, PROJ_PAD - PROJ))).astype(BF16)
    rwkv_prm = {
        "mu": _row_param(rwkv_mu), "w0": _row_param(rwkv_w0), "a0": _row_param(rwkv_a0),
        "w2": jnp.pad(rwkv_w2, ((0, 0), (0, LANES - A_DECAY_LORA), (0, 0))),
        "a2": jnp.pad(rwkv_a2, ((0, 0), (A_DECAY_LORA, 0), (0, 0))),
        "g2": rwkv_g2,
        "k_k": _row_param(rwkv_k_k), "k_a": _row_param(rwkv_k_a),
        "r_k": _row_param(rwkv_r_k.reshape(depth, A_WIDTH)),
        "ln_g": _row_param(rwkv_ln_g), "ln_b": _row_param(rwkv_ln_b),
    }
    gq = _row_param(jnp.tile(attn_q_norm_g, (1, B_HEADS)))
    gk = _row_param(jnp.tile(attn_k_norm_g, (1, B_KV_HEADS)))
    ssd_prm = {
        "conv_w": ssm_conv_w, "conv_b": _row_param(ssm_conv_b),
        "dt_bias": _row_param(ssm_dt_bias, LANES),
        "a_neg": _row_param(-jnp.exp(ssm_a_log), LANES),
        "d_skip": _row_param(jnp.repeat(ssm_d, HEAD_DIM, axis=1)),
        "norm_g": _row_param(ssm_norm_g),
    }
    ffn_prm = {
        "norm_ffn_g": _row_param(norm_ffn_g), "w_out": w_out.astype(BF16), "w_up": ffn_w_up.astype(BF16),
        "ffn_conv_w": ffn_conv_w, "ffn_conv_b": _row_param(ffn_conv_b), "w_down": ffn_w_down.astype(BF16),
    }
    norm_mix = _row_param(norm_mix_g)

    c_all = jnp.concatenate([jnp.pad(c_prompt, ((0, SUBLANES - nseq), (0, 0))), c_sample], axis=0)
    mod = _ada(c_all, ada_w, ada_b)
    mod_p = mod[:, 0:nseq].reshape(depth, nseq, 1, N_MOD * D_MODEL)
    mod_s = mod[:, SUBLANES:]

    cache_k = cache_swa_k.reshape(depth, nd, WINDOW, B_KV_WIDTH)
    cache_v = cache_swa_v.reshape(depth, nd, WINDOW, B_KV_WIDTH)
    ssm_conv_t = jnp.swapaxes(state_ssm_conv, 1, 2)
    ssm_h0 = state_ssm.reshape(depth, nd, C_WIDTH, D_STATE)
    ffn_conv_t = jnp.swapaxes(state_ffn_conv, 1, 2)

    xp = x_prompt.reshape(nseq * t, D_MODEL)
    xs = x_sample.reshape(nd, D_MODEL)
    prompt_new = [[] for _ in range(7)]
    sample_new = [[] for _ in range(7)]
    for l in range(depth):
        proj = _in_proj(xp, mod_p, norm_mix, w_in_p, l, tm, 0, tiles_per_seq).reshape(nseq, t, PROJ_PAD)
        ya, p_shift, p_wkv = _rwkv_prompt(proj, rwkv_prm, l, tb_rwkv)
        yb, p_k, p_v = _swa_prompt(proj, gq, gk, attn_sinks, l, tq_swa)
        yc, p_conv, p_ssm = _ssd_prompt(proj, ssd_prm, l, tb_ssd)
        xp, p_ffn = _out_ffn(xp, ya.reshape(nseq * t, A_WIDTH), yb.reshape(nseq * t, B_WIDTH),
                             yc.reshape(nseq * t, C_WIDTH), mod_p, ffn_prm, l, tm, 0, tiles_per_seq)
        new_p = (p_shift.reshape(nseq, A_PROJ), p_wkv,
                 p_k.reshape(nseq, WINDOW, B_KV_HEADS, HEAD_DIM), p_v.reshape(nseq, WINDOW, B_KV_HEADS, HEAD_DIM),
                 p_conv[:, SUBLANES - (SSM_CONV - 1):], p_ssm.reshape(nseq, C_HEADS, HEAD_DIM, D_STATE),
                 p_ffn[:, SUBLANES - (FFN_CONV - 1):])
        proj_s = _in_proj(xs, mod_s, norm_mix, w_in_p, l, nd, nd, 1)
        ya, s_wkv = _rwkv_decode(proj_s, state_rwkv_shift, state_rwkv_wkv, rwkv_prm, l)
        yb, s_k, s_v = _swa_decode(proj_s, cache_k, cache_v, gq, gk, attn_sinks, l)
        yc, s_conv_t, s_ssm = _ssd_decode(proj_s, ssm_conv_t, ssm_h0, ssd_prm, l)
        xs, s_ffn_t = _out_ffn(xs, ya, yb, yc, mod_s, ffn_prm, l, nd, nd, 1, ffn_conv_t)
        new_s = (proj_s[:, 0:A_PROJ], s_wkv,
                 s_k.reshape(nd, WINDOW, B_KV_HEADS, HEAD_DIM), s_v.reshape(nd, WINDOW, B_KV_HEADS, HEAD_DIM),
                 jnp.swapaxes(s_conv_t, 0, 1), s_ssm.reshape(nd, C_HEADS, HEAD_DIM, D_STATE),
                 jnp.swapaxes(s_ffn_t, 0, 1))
        for lst, arr in zip(prompt_new, new_p):
            lst.append(arr)
        for lst, arr in zip(sample_new, new_s):
            lst.append(arr)
    outs_p = [jnp.stack(lst) for lst in prompt_new]
    outs_s = [jnp.stack(lst) for lst in sample_new]
    return (xp.reshape(nseq, t, D_MODEL), xs.reshape(nd, 1, D_MODEL), *outs_p, *outs_s)
```

```python
import functools

import jax
import jax.numpy as jnp
from jax import lax
from jax.experimental import pallas as pl
from jax.experimental.pallas import tpu as pltpu

F32 = jnp.float32
BF16 = jnp.bfloat16

D_MODEL = 1024
HEAD_DIM = 64
A_WIDTH = 256
A_HEADS = A_WIDTH // HEAD_DIM
A_DECAY_LORA = 64
A_AAA_LORA = 64
A_GATE_LORA = 128
A_PROJ = 3 * A_WIDTH + A_DECAY_LORA + A_AAA_LORA + A_GATE_LORA
B_WIDTH = 256
B_HEADS = B_WIDTH // HEAD_DIM
B_KV_HEADS = 2
B_GROUP = B_HEADS // B_KV_HEADS
B_KV_WIDTH = B_KV_HEADS * HEAD_DIM
B_PROJ = B_WIDTH + 2 * B_KV_WIDTH
WINDOW = 128
C_WIDTH = 512
C_HEADS = C_WIDTH // HEAD_DIM
C_GROUPS = 2
D_STATE = 128
SSM_CONV = 4
CONV_DIM = C_WIDTH + 2 * C_GROUPS * D_STATE
C_PROJ = C_WIDTH + CONV_DIM + C_HEADS
PROJ = A_PROJ + B_PROJ + C_PROJ
D_FF = 2816
FFN_CONV = 3
NORM_EPS = 1e-6
GN_EPS = 64e-5
N_MOD = 6

LANES = 128
SUBLANES = 8
VMEM_LIMIT_BYTES = 56 * 1024 * 1024

DT_OFF = A_PROJ + B_PROJ + C_WIDTH + CONV_DIM
PROJ_PAD = DT_OFF + LANES
Z_OFF = A_PROJ + B_PROJ
XBC_OFF = Z_OFF + C_WIDTH

RWKV_CHUNK = 64
INV_BLOCK = 8
SSD_CHUNK = 128
SWA_BLOCK = 128
DEC_BLOCK = 8
FF_CHUNK = D_FF // 2


def _mm(a, b):
    return jnp.dot(a.astype(BF16), b.astype(BF16), preferred_element_type=F32)


def _mm_nt(a, b):
    return lax.dot_general(a.astype(BF16), b.astype(BF16), (((1,), (1,)), ((), ())), preferred_element_type=F32)


def _mm_tn(a, b):
    return lax.dot_general(a.astype(BF16), b.astype(BF16), (((0,), (0,)), ((), ())), preferred_element_type=F32)


def _split3(x):
    hi = x.astype(BF16)
    r1 = x - hi.astype(F32)
    mid = r1.astype(BF16)
    lo = (r1 - mid.astype(F32)).astype(BF16)
    return hi, mid, lo


def _mm_sel_left(sel, x):
    hi, mid, lo = _split3(x)
    s = sel.astype(BF16)
    d = lambda p: jnp.dot(s, p, preferred_element_type=F32)
    return (d(lo) + d(mid)) + d(hi)


def _mm_sel_right(x, sel):
    hi, mid, lo = _split3(x)
    s = sel.astype(BF16)
    d = lambda p: jnp.dot(p, s, preferred_element_type=F32)
    return (d(lo) + d(mid)) + d(hi)


def _mm_nt_sel_left(sel, x):
    hi, mid, lo = _split3(x)
    s = sel.astype(BF16)
    d = lambda p: lax.dot_general(s, p, (((1,), (1,)), ((), ())), preferred_element_type=F32)
    return (d(lo) + d(mid)) + d(hi)


def _sigmoid(x):
    return 1.0 / (1.0 + jnp.exp(-x))


def _silu(x):
    return x * _sigmoid(x)


def _softplus(x):
    return jnp.maximum(x, 0.0) + jnp.log(1.0 + jnp.exp(-jnp.abs(x)))


def _iota(shape, dim):
    return lax.broadcasted_iota(jnp.int32, shape, dim)


def _eye(n):
    return (_iota((n, n), 0) == _iota((n, n), 1)).astype(F32)


def _tril_ones(n):
    return (_iota((n, n), 0) >= _iota((n, n), 1)).astype(F32)


def _group_sum(x, group):
    gid = _iota(x.shape, x.ndim - 1) // group
    out = jnp.zeros_like(x)
    for h in range(x.shape[-1] // group):
        m = gid == h
        s = jnp.sum(jnp.where(m, x, 0.0), axis=-1, keepdims=True)
        out = jnp.where(m, s, out)
    return out


def _rms_norm(x, g):
    return x * lax.rsqrt(jnp.mean(x * x, axis=-1, keepdims=True) + NORM_EPS) * g


def _head_rms_norm(x, g):
    return x * lax.rsqrt(_group_sum(x * x, HEAD_DIM) * (1.0 / HEAD_DIM) + NORM_EPS) * g


def _shift_rows(x, k, tail):
    rolled = pltpu.roll(x, k, axis=0)
    row = _iota((SUBLANES, x.shape[1]), 0)
    top = jnp.where(row < k, pltpu.roll(tail, k, axis=0), rolled[0:SUBLANES])
    return jnp.concatenate([top, rolled[SUBLANES:]], axis=0)


def _ada_kernel(c_ref, w_ref, b_ref, o_ref):
    o_ref[...] = _mm(_silu(c_ref[...]), w_ref[...]) + b_ref[...]


def _ada(c_all, ada_w, ada_b):
    depth = ada_w.shape[0]
    rows = c_all.shape[0]
    width = N_MOD * D_MODEL
    tn = 1536
    return pl.pallas_call(
        _ada_kernel,
        grid=(depth, width // tn),
        in_specs=[
            pl.BlockSpec((rows, D_MODEL), lambda l, j: (0, 0)),
            pl.BlockSpec((None, D_MODEL, tn), lambda l, j: (l, 0, j)),
            pl.BlockSpec((None, 1, tn), lambda l, j: (l, 0, j)),
        ],
        out_specs=pl.BlockSpec((None, rows, tn), lambda l, j: (l, 0, j)),
        out_shape=jax.ShapeDtypeStruct((depth, rows, width), F32),
        compiler_params=pltpu.CompilerParams(
            dimension_semantics=("arbitrary", "arbitrary"), vmem_limit_bytes=VMEM_LIMIT_BYTES),
        name="ada_mod",
    )(c_all, ada_w, ada_b.reshape(depth, 1, width))


def _in_proj_kernel(x_ref, sh_ref, sc_ref, g_ref, w_ref, o_ref):
    h = _rms_norm(x_ref[...], g_ref[...]) * (1.0 + sc_ref[...]) + sh_ref[...]
    o_ref[...] = _mm(h, w_ref[...])


def _mod_spec(layer, per_row, rows_per_seq_tiles, col):
    if per_row:
        return pl.BlockSpec((None, per_row, D_MODEL), lambda i: (layer, 0, col))
    return pl.BlockSpec((None, None, 1, D_MODEL), lambda i: (layer, i // rows_per_seq_tiles, 0, col))


def _layer_spec(layer, tail, single_buffer=False):
    idx = lambda *_: (layer,) + (0,) * len(tail)
    if single_buffer:
        return pl.BlockSpec((None,) + tuple(tail), idx, pipeline_mode=pl.Buffered(1))
    return pl.BlockSpec((None,) + tuple(tail), idx)


def _in_proj(x, mod, norm_g, w_in, layer, tm, per_row, tiles_per_seq):
    rows = x.shape[0]
    return pl.pallas_call(
        _in_proj_kernel,
        grid=(rows // tm,),
        in_specs=[
            pl.BlockSpec((tm, D_MODEL), lambda i: (i, 0)),
            _mod_spec(layer, per_row, tiles_per_seq, 0),
            _mod_spec(layer, per_row, tiles_per_seq, 1),
            _layer_spec(layer, (1, D_MODEL)),
            _layer_spec(layer, (D_MODEL, PROJ_PAD), single_buffer=True),
        ],
        out_specs=pl.BlockSpec((tm, PROJ_PAD), lambda i: (i, 0)),
        out_shape=jax.ShapeDtypeStruct((rows, PROJ_PAD), F32),
        compiler_params=pltpu.CompilerParams(
            dimension_semantics=("arbitrary",), vmem_limit_bytes=VMEM_LIMIT_BYTES),
        name="in_proj",
    )(x, mod, mod, norm_g, w_in)


_RWKV_PARAMS = ("mu", "w0", "w2", "a0", "a2", "g2", "k_k", "k_a", "r_k", "ln_g", "ln_b")


def _rwkv_prep(xs, p):
    r = xs[:, 0:A_WIDTH]
    k = xs[:, A_WIDTH:2 * A_WIDTH]
    v = xs[:, 2 * A_WIDTH:3 * A_WIDTH]
    lora_in = xs[:, 3 * A_WIDTH:3 * A_WIDTH + LANES]
    lg = xs[:, 3 * A_WIDTH + LANES:A_PROJ]
    w_log = -_softplus(-(p["w0"] + _mm(jnp.tanh(lora_in), p["w2"]))) - 0.5
    log_decay = -jnp.exp(w_log)
    a = _sigmoid(p["a0"] + _mm(lora_in, p["a2"]))
    g = _mm(_sigmoid(lg), p["g2"])
    kk = k * p["k_k"]
    kk = kk / jnp.maximum(jnp.sqrt(_group_sum(kk * kk, HEAD_DIM)), 1e-12)
    k = k * (1.0 + (a - 1.0) * p["k_a"])
    return r, k, v, log_decay, -kk, kk * a, g


def _rwkv_post(y, r, k, v, g, p):
    mean = _group_sum(y, HEAD_DIM) * (1.0 / HEAD_DIM)
    c = y - mean
    var = _group_sum(c * c, HEAD_DIM) * (1.0 / HEAD_DIM)
    yn = c * lax.rsqrt(var + GN_EPS) * p["ln_g"] + p["ln_b"]
    bonus = _group_sum(r * k * p["r_k"], HEAD_DIM) * v
    return (yn + bonus) * g


def _block_diag(x, mask):
    return jnp.where(mask, jnp.concatenate([x] * (x.shape[1] // x.shape[0]), axis=0), jnp.zeros((), x.dtype))


def _rwkv_chunk_setup(r, k, v, logd, cum, av, bv, cst):
    bd_mask, strict, incl, eye_cat, level_masks = cst
    excl = cum - logd
    cum_last = cum[RWKV_CHUNK - 1:RWKV_CHUNK, :]
    e_neg = jnp.exp(-cum)
    e_end = jnp.exp(cum_last - cum)
    at = (av * jnp.exp(excl)).astype(BF16)
    rt = (r * jnp.exp(cum)).astype(BF16)
    bt = (bv * e_neg).astype(BF16)
    kt = (k * e_neg).astype(BF16)
    vb = v.astype(BF16)
    bd = lambda x: _block_diag(x, bd_mask)

    lhs = jnp.concatenate([at, rt], axis=0)
    rhs = jnp.concatenate([bd(bt), bd(kt)], axis=0)
    amat = lax.dot_general(lhs, rhs, (((1,), (1,)), ((), ())), preferred_element_type=F32)
    yield
    n_ab = jnp.where(strict, amat[0:RWKV_CHUNK, 0:A_WIDTH], 0.0)
    n_ak = jnp.where(strict, amat[0:RWKV_CHUNK, A_WIDTH:], 0.0).astype(BF16)
    n_rb = jnp.where(incl, amat[RWKV_CHUNK:, 0:A_WIDTH], 0.0).astype(BF16)
    n_rk = jnp.where(incl, amat[RWKV_CHUNK:, A_WIDTH:], 0.0).astype(BF16)
    wy0 = jnp.dot(jnp.concatenate([n_ak, n_rk], axis=0), bd(vb), preferred_element_type=F32)
    yield
    inv = yield from _unit_lower_inverse(n_ab, eye_cat, level_masks, bd)
    return dict(
        lhs=lhs, inv=inv, w0=wy0[0:RWKV_CHUNK], y0=wy0[RWKV_CHUNK:], n_rb=n_rb, vb=vb,
        bk_end=jnp.concatenate([(bv * e_end).astype(BF16), (k * e_end).astype(BF16)], axis=0),
        decay_end=jnp.exp(cum_last),
    )


def _rwkv_chunk_apply(c, state, bd_mask):
    bd = lambda x: _block_diag(x, bd_mask)
    both = lax.dot_general(c["lhs"], state.astype(BF16), (((1,), (1,)), ((), ())), preferred_element_type=F32)
    yield
    w = both[0:RWKV_CHUNK] + c["w0"]
    ub = jnp.dot(c["inv"], bd(w.astype(BF16)), preferred_element_type=F32).astype(BF16)
    yield
    y = both[RWKV_CHUNK:] + c["y0"] + jnp.dot(c["n_rb"], bd(ub), preferred_element_type=F32)
    upd = lax.dot_general(jnp.concatenate([ub, c["vb"]], axis=0), c["bk_end"],
                          (((0,), (0,)), ((), ())), preferred_element_type=F32)
    return y, state * c["decay_end"] + jnp.where(bd_mask, upd, 0.0)


def _interleave(generators):
    results = [None] * len(generators)
    active = list(enumerate(generators))
    while active:
        still = []
        for i, gen in active:
            try:
                next(gen)
                still.append((i, gen))
            except StopIteration as done:
                results[i] = done.value
        active = still
    return results


def _split2(x):
    hi = x.astype(BF16)
    return hi, (x - hi.astype(F32)).astype(BF16)


def _mm_cat2(lhs_list, b, bd):
    rows = lhs_list[0].shape[0]
    width = b.shape[1]
    b_hi, b_lo = _split2(b)
    lhs = jnp.concatenate([part for a in lhs_list for part in _split2(a)], axis=0)
    prod = jnp.dot(lhs, jnp.concatenate([bd(b_hi), bd(b_lo)], axis=1), preferred_element_type=F32)
    outs = []
    for i in range(len(lhs_list)):
        hi = prod[2 * i * rows:(2 * i + 1) * rows]
        lo = prod[(2 * i + 1) * rows:(2 * i + 2) * rows]
        outs.append((lo[:, 0:width] + hi[:, width:]) + hi[:, 0:width])
    return outs


def _unit_lower_inverse(n, eye_cat, level_masks, bd):
    m = jnp.where(level_masks[0], n, 0.0)
    inv = eye_cat + m
    (m,) = _mm_cat2([m], m, bd)
    yield
    for step in range(INV_BLOCK.bit_length() - 2):
        if step + 1 < INV_BLOCK.bit_length() - 2:
            delta, m = _mm_cat2([inv, m], m, bd)
        else:
            (delta,) = _mm_cat2([inv], m, bd)
        yield
        inv = inv + delta
    inv = inv.astype(BF16)
    for mask in level_masks[1:]:
        off = jnp.where(mask, n, 0.0).astype(BF16)
        g = jnp.dot(inv, bd(off), preferred_element_type=F32)
        yield
        inv = inv + jnp.dot(g.astype(BF16), bd(inv), preferred_element_type=F32).astype(BF16)
        yield
    return inv


def _rwkv_consts():
    w = A_WIDTH
    bd_mask = (_iota((w, w), 0) // HEAD_DIM) == (_iota((w, w), 1) // HEAD_DIM)
    t = _iota((RWKV_CHUNK, w), 0)
    s = _iota((RWKV_CHUNK, w), 1) % RWKV_CHUNK
    level_masks = [t // INV_BLOCK == s // INV_BLOCK]
    size = INV_BLOCK
    while size < RWKV_CHUNK:
        level_masks.append((t // (2 * size) == s // (2 * size)) & (t // size != s // size))
        size *= 2
    return bd_mask, s < t, s <= t, (s == t).astype(F32), level_masks


def _load_params(refs):
    return {name: ref[...] for name, ref in zip(_RWKV_PARAMS, refs)}


def _rwkv_prompt_kernel(pa_ref, *rest):
    prm_refs = rest[:len(_RWKV_PARAMS)]
    ya_ref, shift_ref, wkv_ref, carry_ref, state_ref = rest[len(_RWKV_PARAMS):]
    i = pl.program_id(0)
    nseq, tb, _ = pa_ref.shape

    @pl.when(i == 0)
    def _():
        carry_ref[...] = jnp.zeros_like(carry_ref)
        state_ref[...] = jnp.zeros_like(state_ref)

    p = _load_params(prm_refs)
    cst = _rwkv_consts()
    n_chunks = tb // RWKV_CHUNK
    rows_all = nseq * tb
    pa = jnp.concatenate([pa_ref[n] for n in range(nseq)], axis=0)
    row = _iota(pa.shape, 0)
    prev = pltpu.roll(pa, 1, axis=0)
    for n in range(nseq):
        prev = jnp.where(row == n * tb, carry_ref[n], prev)
        last = pa_ref[n, tb - 1:tb, :]
        carry_ref[n] = last
        shift_ref[n] = last
    r, k, v, logd, av, bv, g = _rwkv_prep(pa + (prev - pa) * p["mu"], p)
    ri = _iota((rows_all, rows_all), 0)
    ci = _iota((rows_all, rows_all), 1)
    chunk_tril = ((ri // RWKV_CHUNK == ci // RWKV_CHUNK) & (ri >= ci)).astype(F32)
    cum = _mm_sel_left(chunk_tril, logd)
    chunk_rows = [slice(q * RWKV_CHUNK, (q + 1) * RWKV_CHUNK) for q in range(rows_all // RWKV_CHUNK)]
    setups = _interleave([
        _rwkv_chunk_setup(r[rows], k[rows], v[rows], logd[rows], cum[rows], av[rows], bv[rows], cst)
        for rows in chunk_rows])

    states = [state_ref[n] for n in range(nseq)]
    ys = [[None] * n_chunks for _ in range(nseq)]
    for c in range(n_chunks):
        stepped = _interleave([_rwkv_chunk_apply(setups[n * n_chunks + c], states[n], cst[0]) for n in range(nseq)])
        for n in range(nseq):
            ys[n][c], states[n] = stepped[n]
    y = jnp.concatenate([y_c for n in range(nseq) for y_c in ys[n]], axis=0)
    out = _rwkv_post(y, r, k, v, g, p)
    for n in range(nseq):
        state_ref[n] = states[n]
        ya_ref[n] = out[n * tb:(n + 1) * tb]

    @pl.when(i == pl.num_programs(0) - 1)
    def _():
        for n in range(nseq):
            for h in range(A_HEADS):
                sl = slice(h * HEAD_DIM, (h + 1) * HEAD_DIM)
                wkv_ref[n, h] = state_ref[n, sl, sl]


def _rwkv_param_specs(layer):
    shapes = {"mu": (1, A_PROJ), "w2": (LANES, A_WIDTH), "a2": (LANES, A_WIDTH), "g2": (A_GATE_LORA, A_WIDTH)}
    return [_layer_spec(layer, shapes.get(name, (1, A_WIDTH))) for name in _RWKV_PARAMS]


def _rwkv_prompt(proj, prm, layer, tb):
    nseq, t, _ = proj.shape
    return pl.pallas_call(
        _rwkv_prompt_kernel,
        grid=(t // tb,),
        in_specs=[pl.BlockSpec((nseq, tb, A_PROJ), lambda i: (0, i, 0))] + _rwkv_param_specs(layer),
        out_specs=[
            pl.BlockSpec((nseq, tb, A_WIDTH), lambda i: (0, i, 0)),
            pl.BlockSpec((nseq, 1, A_PROJ), lambda i: (0, 0, 0)),
            pl.BlockSpec((nseq, A_HEADS, HEAD_DIM, HEAD_DIM), lambda i: (0, 0, 0, 0)),
        ],
        out_shape=[
            jax.ShapeDtypeStruct((nseq, t, A_WIDTH), F32),
            jax.ShapeDtypeStruct((nseq, 1, A_PROJ), F32),
            jax.ShapeDtypeStruct((nseq, A_HEADS, HEAD_DIM, HEAD_DIM), F32),
        ],
        scratch_shapes=[pltpu.VMEM((nseq, 1, A_PROJ), F32), pltpu.VMEM((nseq, A_WIDTH, A_WIDTH), F32)],
        compiler_params=pltpu.CompilerParams(
            dimension_semantics=("arbitrary",), vmem_limit_bytes=VMEM_LIMIT_BYTES),
        name="rwkv_prompt",
    )(proj, *[prm[name] for name in _RWKV_PARAMS])


def _rwkv_decode_kernel(pa_ref, shift_ref, wkv0_ref, *rest):
    prm_refs = rest[:len(_RWKV_PARAMS)]
    ya_ref, wkv_ref = rest[len(_RWKV_PARAMS):]
    p = _load_params(prm_refs)
    pa = pa_ref[...]
    nb = pa.shape[0]
    r, k, v, logd, av, bv, g = _rwkv_prep(pa + (shift_ref[...] - pa) * p["mu"], p)
    decay = jnp.exp(logd)
    units = [(j, h) for j in range(nb) for h in range(A_HEADS)]
    head = lambda h: slice(h * HEAD_DIM, (h + 1) * HEAD_DIM)
    ones = jnp.ones((HEAD_DIM, HEAD_DIM), BF16)
    row_sums = lambda xs: jnp.dot(jnp.concatenate(xs, axis=0).astype(BF16), ones, preferred_element_type=F32)
    eye = _eye(A_WIDTH)
    v_cols = [_mm_nt_sel_left(eye, jnp.broadcast_to(v[j:j + 1], (HEAD_DIM, A_WIDTH))) for j in range(nb)]
    s0 = [wkv0_ref[j, h] for j, h in units]
    sa = row_sums([s * av[j:j + 1, head(h)] for s, (j, h) in zip(s0, units)])
    sr = []
    for i, (j, h) in enumerate(units):
        rows = slice(i * HEAD_DIM, (i + 1) * HEAD_DIM)
        s = (s0[i] * decay[j:j + 1, head(h)] + sa[rows] * bv[j:j + 1, head(h)]
             + v_cols[j][head(h)] * k[j:j + 1, head(h)])
        wkv_ref[j, h] = s
        sr.append(s * r[j:j + 1, head(h)])
    y_cols = row_sums(sr)
    lane = _iota((A_WIDTH, nb), 1)
    y_t = jnp.zeros((A_WIDTH, nb), F32)
    for j in range(nb):
        y_t = jnp.where(lane == j, y_cols[j * A_WIDTH:(j + 1) * A_WIDTH, 0:nb], y_t)
    y = _mm_nt_sel_left(_eye(nb), y_t)
    ya_ref[...] = _rwkv_post(y, r, k, v, g, p)


def _rwkv_decode(proj, shift_prev, wkv0, prm, layer):
    nd = proj.shape[0]
    nb = DEC_BLOCK
    return pl.pallas_call(
        _rwkv_decode_kernel,
        grid=(nd // nb,),
        in_specs=[
            pl.BlockSpec((nb, A_PROJ), lambda i: (i, 0)),
            pl.BlockSpec((None, nb, A_PROJ), lambda i: (layer, i, 0)),
            pl.BlockSpec((None, nb, A_HEADS, HEAD_DIM, HEAD_DIM), lambda i: (layer, i, 0, 0, 0)),
        ] + _rwkv_param_specs(layer),
        out_specs=[
            pl.BlockSpec((nb, A_WIDTH), lambda i: (i, 0)),
            pl.BlockSpec((nb, A_HEADS, HEAD_DIM, HEAD_DIM), lambda i: (i, 0, 0, 0)),
        ],
        out_shape=[
            jax.ShapeDtypeStruct((nd, A_WIDTH), F32),
            jax.ShapeDtypeStruct((nd, A_HEADS, HEAD_DIM, HEAD_DIM), F32),
        ],
        compiler_params=pltpu.CompilerParams(
            dimension_semantics=("arbitrary",), vmem_limit_bytes=VMEM_LIMIT_BYTES),
        name="rwkv_decode",
    )(proj, shift_prev, wkv0, *[prm[name] for name in _RWKV_PARAMS])


def _place_q_heads(qn):
    lane_half = _iota((qn.shape[0], B_KV_WIDTH), 1) // HEAD_DIM
    placed = []
    for h in range(B_HEADS):
        kh, g = divmod(h, B_GROUP)
        chunk = qn[:, kh * B_KV_WIDTH:(kh + 1) * B_KV_WIDTH]
        if g != kh:
            chunk = pltpu.roll(chunk, HEAD_DIM, axis=1)
        placed.append(jnp.where(lane_half == kh, chunk, 0.0))
    return placed


def _gather_o_heads(o_heads):
    lane_half = _iota(o_heads[0].shape, 1) // HEAD_DIM
    chunks = []
    for kh in range(B_KV_HEADS):
        parts = []
        for g in range(B_GROUP):
            o = o_heads[kh * B_GROUP + g]
            parts.append(o if g == kh else pltpu.roll(o, HEAD_DIM, axis=1))
        chunks.append(jnp.where(lane_half == 0, parts[0], parts[1]))
    return jnp.concatenate(chunks, axis=1)


def _swa_prompt_kernel(layer, sinks_ref, pb_ref, gq_ref, gk_ref, yb_ref, nk_ref, nv_ref, kprev_ref, vprev_ref):
    j = pl.program_id(1)
    tq = pb_ref.shape[0]
    blk = SWA_BLOCK

    @pl.when(j == 0)
    def _():
        kprev_ref[...] = jnp.zeros_like(kprev_ref)
        vprev_ref[...] = jnp.zeros_like(vprev_ref)

    scale = HEAD_DIM ** -0.5
    rows = _iota((B_GROUP * blk, 2 * blk), 0)
    cols = _iota((B_GROUP * blk, 2 * blk), 1)
    qi = rows % blk
    band = (cols >= qi) & (cols <= qi + WINDOW)
    n_sub = tq // blk
    x = pb_ref[...]
    qn = _head_rms_norm(x[:, 0:B_WIDTH], gq_ref[...])
    kn = _head_rms_norm(x[:, B_WIDTH:B_WIDTH + B_KV_WIDTH], gk_ref[...])
    v = x[:, B_WIDTH + B_KV_WIDTH:B_PROJ]
    kfull = jnp.concatenate([kprev_ref[...], kn], axis=0)
    vfull = jnp.concatenate([vprev_ref[...], v], axis=0)
    placed = _place_q_heads(qn)

    def attend(sb, kh):
        q2 = jnp.concatenate([pl_h[sb * blk:(sb + 1) * blk] for pl_h in placed[kh * B_GROUP:(kh + 1) * B_GROUP]],
                             axis=0)
        s = _mm_nt(q2, kfull[sb * blk:(sb + 2) * blk])
        yield
        first_key = jnp.where(j * n_sub + sb > 0, 0, blk)
        s = jnp.where(band & (cols >= first_key), s * scale, -jnp.inf)
        sink = jnp.where(rows[:, 0:1] < blk, sinks_ref[layer, kh * B_GROUP], sinks_ref[layer, kh * B_GROUP + 1])
        m = jnp.maximum(jnp.max(s, axis=-1, keepdims=True), sink)
        pr = jnp.exp(s - m)
        den = jnp.sum(pr, axis=-1, keepdims=True) + jnp.exp(sink - m)
        o = _mm(pr, vfull[sb * blk:(sb + 2) * blk])
        yield
        return o / den

    outs = _interleave([attend(sb, kh) for sb in range(n_sub) for kh in range(B_KV_HEADS)])
    for sb in range(n_sub):
        heads = [outs[sb * B_KV_HEADS + kh][g * blk:(g + 1) * blk]
                 for kh in range(B_KV_HEADS) for g in range(B_GROUP)]
        yb_ref[sb * blk:(sb + 1) * blk, :] = _gather_o_heads(heads)
    kprev_ref[...] = kn[tq - blk:tq]
    vprev_ref[...] = v[tq - blk:tq]
    nk_ref[...] = kn[tq - blk:tq]
    nv_ref[...] = v[tq - blk:tq]


def _swa_prompt(proj, gq, gk, sinks, layer, tq):
    nseq, t, _ = proj.shape
    col = A_PROJ // B_PROJ
    return pl.pallas_call(
        functools.partial(_swa_prompt_kernel, layer),
        grid=(nseq, t // tq),
        in_specs=[
            pl.BlockSpec(memory_space=pltpu.SMEM),
            pl.BlockSpec((None, tq, B_PROJ), lambda n, j: (n, j, col)),
            _layer_spec(layer, (1, B_WIDTH)),
            _layer_spec(layer, (1, B_KV_WIDTH)),
        ],
        out_specs=[
            pl.BlockSpec((None, tq, B_WIDTH), lambda n, j: (n, j, 0)),
            pl.BlockSpec((None, WINDOW, B_KV_WIDTH), lambda n, j: (n, 0, 0)),
            pl.BlockSpec((None, WINDOW, B_KV_WIDTH), lambda n, j: (n, 0, 0)),
        ],
        out_shape=[
            jax.ShapeDtypeStruct((nseq, t, B_WIDTH), F32),
            jax.ShapeDtypeStruct((nseq, WINDOW, B_KV_WIDTH), F32),
            jax.ShapeDtypeStruct((nseq, WINDOW, B_KV_WIDTH), F32),
        ],
        scratch_shapes=[pltpu.VMEM((SWA_BLOCK, B_KV_WIDTH), F32), pltpu.VMEM((SWA_BLOCK, B_KV_WIDTH), F32)],
        compiler_params=pltpu.CompilerParams(
            dimension_semantics=("arbitrary", "arbitrary"), vmem_limit_bytes=VMEM_LIMIT_BYTES),
        name="swa_prompt",
    )(sinks, proj, gq, gk)


def _swa_decode_kernel(layer, sinks_ref, pb_ref, ck_ref, cv_ref, gq_ref, gk_ref, yb_ref, nk_ref, nv_ref):
    x = pb_ref[...]
    nb = x.shape[0]
    qn = _head_rms_norm(x[:, 0:B_WIDTH], gq_ref[...])
    kn = _head_rms_norm(x[:, B_WIDTH:B_WIDTH + B_KV_WIDTH], gk_ref[...])
    v = x[:, B_WIDTH + B_KV_WIDTH:B_PROJ]
    placed = _place_q_heads(qn)
    scale = HEAD_DIM ** -0.5
    head = _iota((B_HEADS, 1), 0)
    sink = jnp.zeros((B_HEADS, 1), F32)
    for h in range(B_HEADS):
        sink = jnp.where(head == h, sinks_ref[layer, h], sink)
    last_row = _iota((WINDOW, B_KV_WIDTH), 0) == WINDOW - 1
    o_rows = []
    for j in range(nb):
        kc = ck_ref[j]
        vc = cv_ref[j]
        q4 = jnp.concatenate([pl_h[j:j + 1] for pl_h in placed], axis=0)
        s = _mm_nt(q4, kc) * scale
        s_new = jnp.sum(q4 * kn[j:j + 1], axis=-1, keepdims=True) * scale
        m = jnp.maximum(jnp.maximum(jnp.max(s, axis=-1, keepdims=True), s_new), sink)
        pr = jnp.exp(s - m)
        p_new = jnp.exp(s_new - m)
        den = jnp.sum(pr, axis=-1, keepdims=True) + p_new + jnp.exp(sink - m)
        o_rows.append((_mm(pr, vc) + p_new * v[j:j + 1]) / den)
        nk_ref[j] = jnp.where(last_row, kn[j:j + 1], pltpu.roll(kc, WINDOW - 1, axis=0))
        nv_ref[j] = jnp.where(last_row, v[j:j + 1], pltpu.roll(vc, WINDOW - 1, axis=0))
    outs = [jnp.concatenate([o[h:h + 1] for o in o_rows], axis=0) for h in range(B_HEADS)]
    yb_ref[...] = _gather_o_heads(outs)


def _swa_decode(proj, cache_k, cache_v, gq, gk, sinks, layer):
    nd = proj.shape[0]
    nb = DEC_BLOCK
    col = A_PROJ // B_PROJ
    cache_spec = pl.BlockSpec((None, nb, WINDOW, B_KV_WIDTH), lambda i: (layer, i, 0, 0))
    new_spec = pl.BlockSpec((nb, WINDOW, B_KV_WIDTH), lambda i: (i, 0, 0))
    return pl.pallas_call(
        functools.partial(_swa_decode_kernel, layer),
        grid=(nd // nb,),
        in_specs=[
            pl.BlockSpec(memory_space=pltpu.SMEM),
            pl.BlockSpec((nb, B_PROJ), lambda i: (i, col)),
            cache_spec, cache_spec,
            _layer_spec(layer, (1, B_WIDTH)),
            _layer_spec(layer, (1, B_KV_WIDTH)),
        ],
        out_specs=[pl.BlockSpec((nb, B_WIDTH), lambda i: (i, 0)), new_spec, new_spec],
        out_shape=[
            jax.ShapeDtypeStruct((nd, B_WIDTH), F32),
            jax.ShapeDtypeStruct((nd, WINDOW, B_KV_WIDTH), F32),
            jax.ShapeDtypeStruct((nd, WINDOW, B_KV_WIDTH), F32),
        ],
        compiler_params=pltpu.CompilerParams(
            dimension_semantics=("arbitrary",), vmem_limit_bytes=VMEM_LIMIT_BYTES),
        name="swa_decode",
    )(sinks, proj, cache_k, cache_v, gq, gk)


def _head_expand(width_per_head):
    n = C_HEADS * width_per_head
    return (_iota((LANES, n), 0) == _iota((LANES, n), 1) // width_per_head).astype(F32)


def _ssd_prompt_kernel(z_ref, xbc_ref, dt_ref, cw_ref, cb_ref, dtb_ref, aneg_ref, dskip_ref, ng_ref,
                       yc_ref, tail_ref, hout_ref, carry_ref, h_ref, xdt_s, b_s, c_s, da_s, y_s):
    j = pl.program_id(1)
    tb = xbc_ref.shape[0]

    @pl.when(j == 0)
    def _():
        carry_ref[...] = jnp.zeros_like(carry_ref)
        h_ref[...] = jnp.zeros_like(h_ref)

    xbc = xbc_ref[...]
    tail = carry_ref[...]
    cw = cw_ref[...]
    conv = cb_ref[...] + cw[SSM_CONV - 1:SSM_CONV] * xbc
    for k in range(1, SSM_CONV):
        conv = conv + cw[SSM_CONV - 1 - k:SSM_CONV - k] * _shift_rows(xbc, k, tail)
    last_rows = xbc[tb - SUBLANES:tb, :]
    carry_ref[...] = last_rows
    tail_ref[...] = last_rows
    act = _silu(conv)
    x = act[:, 0:C_WIDTH]
    dt = _softplus(dt_ref[...] + dtb_ref[...])
    xdt_s[...] = x * _mm_sel_right(dt, _head_expand(HEAD_DIM))
    b_s[...] = act[:, C_WIDTH:C_WIDTH + C_GROUPS * D_STATE]
    c_s[...] = act[:, C_WIDTH + C_GROUPS * D_STATE:CONV_DIM]
    da_s[...] = dt * aneg_ref[...]

    ch = SSD_CHUNK
    tril = _tril_ones(ch)
    causal = _iota((ch, ch), 0) >= _iota((ch, ch), 1)
    lane_lo = _iota((ch, 2 * HEAD_DIM), 1) < HEAD_DIM
    row_lo = _iota((2 * HEAD_DIM, D_STATE), 0) < HEAD_DIM
    heads_per_group = C_HEADS // C_GROUPS

    def chunk_body(c, carry):
        rows = pl.ds(pl.multiple_of(c * ch, ch), ch)
        acum = _mm_sel_left(tril, da_s[rows, :])
        acum_t = acum.T
        a_last = acum[ch - 1:ch, :]
        xdt = xdt_s[rows, :]
        cbs = []
        for g in range(C_GROUPS):
            gs = slice(g * D_STATE, (g + 1) * D_STATE)
            cbs.append(_mm_nt(c_s[rows, gs], b_s[rows, gs]))
        pairs = []
        for pr in range(C_HEADS // 2):
            g = (2 * pr) // heads_per_group
            gs = slice(g * D_STATE, (g + 1) * D_STATE)
            ps = slice(pr * 2 * HEAD_DIM, (pr + 1) * 2 * HEAD_DIM)
            xp = xdt[:, ps]
            per_head = []
            for e in (2 * pr, 2 * pr + 1):
                col = acum[:, e:e + 1]
                seg = col - acum_t[e:e + 1, :]
                lmat = jnp.exp(jnp.where(causal, seg, -jnp.inf))
                last = a_last[:, e:e + 1]
                per_head.append((_mm(cbs[g] * lmat, xp), jnp.exp(last - col), jnp.exp(col), jnp.exp(last)))
            (yd0, dte0, ea0, cd0), (yd1, dte1, ea1, cd1) = per_head
            hp = h_ref[ps, :]
            y_off = _mm_nt(c_s[rows, gs], hp) * jnp.where(lane_lo, ea0, ea1)
            states = _mm_tn(xp * jnp.where(lane_lo, dte0, dte1), b_s[rows, gs])
            h_ref[ps, :] = hp * jnp.where(row_lo, cd0, cd1) + states
            pairs.append(jnp.where(lane_lo, yd0, yd1) + y_off)
        y_s[rows, :] = jnp.concatenate(pairs, axis=1)
        return carry

    lax.fori_loop(0, tb // ch, chunk_body, 0)

    y = (y_s[...] + dskip_ref[...] * x) * _silu(z_ref[...])
    yc_ref[...] = _rms_norm(y, ng_ref[...])
    hout_ref[...] = h_ref[...]


def _ssd_prompt(proj, prm, layer, tb):
    nseq, t, _ = proj.shape
    return pl.pallas_call(
        _ssd_prompt_kernel,
        grid=(nseq, t // tb),
        in_specs=[
            pl.BlockSpec((None, tb, C_WIDTH), lambda n, j: (n, j, Z_OFF // C_WIDTH)),
            pl.BlockSpec((None, tb, CONV_DIM), lambda n, j: (n, j, XBC_OFF // CONV_DIM)),
            pl.BlockSpec((None, tb, LANES), lambda n, j: (n, j, DT_OFF // LANES)),
            _layer_spec(layer, (SSM_CONV, CONV_DIM)),
            _layer_spec(layer, (1, CONV_DIM)),
            _layer_spec(layer, (1, LANES)),
            _layer_spec(layer, (1, LANES)),
            _layer_spec(layer, (1, C_WIDTH)),
            _layer_spec(layer, (1, C_WIDTH)),
        ],
        out_specs=[
            pl.BlockSpec((None, tb, C_WIDTH), lambda n, j: (n, j, 0)),
            pl.BlockSpec((None, SUBLANES, CONV_DIM), lambda n, j: (n, 0, 0)),
            pl.BlockSpec((None, C_WIDTH, D_STATE), lambda n, j: (n, 0, 0)),
        ],
        out_shape=[
            jax.ShapeDtypeStruct((nseq, t, C_WIDTH), F32),
            jax.ShapeDtypeStruct((nseq, SUBLANES, CONV_DIM), F32),
            jax.ShapeDtypeStruct((nseq, C_WIDTH, D_STATE), F32),
        ],
        scratch_shapes=[
            pltpu.VMEM((SUBLANES, CONV_DIM), F32),
            pltpu.VMEM((C_WIDTH, D_STATE), F32),
            pltpu.VMEM((tb, C_WIDTH), F32),
            pltpu.VMEM((tb, C_GROUPS * D_STATE), F32),
            pltpu.VMEM((tb, C_GROUPS * D_STATE), F32),
            pltpu.VMEM((tb, LANES), F32),
            pltpu.VMEM((tb, C_WIDTH), F32),
        ],
        compiler_params=pltpu.CompilerParams(
            dimension_semantics=("arbitrary", "arbitrary"), vmem_limit_bytes=VMEM_LIMIT_BYTES),
        name="ssd_prompt",
    )(proj, proj, proj, prm["conv_w"], prm["conv_b"], prm["dt_bias"], prm["a_neg"], prm["d_skip"], prm["norm_g"])


def _ssd_decode_kernel(z_ref, xbc_ref, dt_ref, buf_ref, h0_ref, cw_ref, cb_ref, dtb_ref, aneg_ref, dskip_ref, ng_ref,
                       yc_ref, nbuf_ref, hout_ref):
    xbc = xbc_ref[...]
    nb = xbc.shape[0]
    cw = cw_ref[...]
    conv = cb_ref[...] + cw[SSM_CONV - 1:SSM_CONV] * xbc
    for k in range(SSM_CONV - 1):
        conv = conv + cw[k:k + 1] * buf_ref[k]
        nbuf_ref[k] = buf_ref[k + 1] if k + 1 < SSM_CONV - 1 else xbc
    act = _silu(conv)
    x = act[:, 0:C_WIDTH]
    bm = act[:, C_WIDTH:C_WIDTH + C_GROUPS * D_STATE]
    cm = act[:, C_WIDTH + C_GROUPS * D_STATE:CONV_DIM]
    dt = _softplus(dt_ref[...] + dtb_ref[...])
    decay = _mm_sel_right(jnp.exp(dt * aneg_ref[...]), _head_expand(D_STATE))
    xdt = x * _mm_sel_right(dt, _head_expand(HEAD_DIM))
    eye = _eye(LANES)
    ones = jnp.ones((D_STATE, D_STATE), BF16)
    heads_per_group = C_HEADS // C_GROUPS
    heads_per_tile = LANES // HEAD_DIM
    hc = []
    for j in range(nb):
        x_cols = [_mm_nt_sel_left(eye, jnp.broadcast_to(xdt[j:j + 1, t * LANES:(t + 1) * LANES], (LANES, LANES)))
                  for t in range(C_WIDTH // LANES)]
        for e in range(C_HEADS):
            gs = slice((e // heads_per_group) * D_STATE, (e // heads_per_group + 1) * D_STATE)
            hs = slice(e * HEAD_DIM, (e + 1) * HEAD_DIM)
            sub = slice((e % heads_per_tile) * HEAD_DIM, (e % heads_per_tile + 1) * HEAD_DIM)
            h = (h0_ref[j, hs, :] * decay[j:j + 1, e * D_STATE:(e + 1) * D_STATE]
                 + x_cols[e // heads_per_tile][sub] * bm[j:j + 1, gs])
            hout_ref[j, hs, :] = h
            hc.append(h * cm[j:j + 1, gs])
    y_cols = jnp.dot(jnp.concatenate(hc, axis=0).astype(BF16), ones, preferred_element_type=F32)
    lane = _iota((C_WIDTH, nb), 1)
    y_t = jnp.zeros((C_WIDTH, nb), F32)
    for j in range(nb):
        y_t = jnp.where(lane == j, y_cols[j * C_WIDTH:(j + 1) * C_WIDTH, 0:nb], y_t)
    y = _mm_nt_sel_left(_eye(nb), y_t)
    y = (y + dskip_ref[...] * x) * _silu(z_ref[...])
    yc_ref[...] = _rms_norm(y, ng_ref[...])


def _ssd_decode(proj, conv_buf_t, h0, prm, layer):
    nd = proj.shape[0]
    nb = DEC_BLOCK
    return pl.pallas_call(
        _ssd_decode_kernel,
        grid=(nd // nb,),
        in_specs=[
            pl.BlockSpec((nb, C_WIDTH), lambda i: (i, Z_OFF // C_WIDTH)),
            pl.BlockSpec((nb, CONV_DIM), lambda i: (i, XBC_OFF // CONV_DIM)),
            pl.BlockSpec((nb, LANES), lambda i: (i, DT_OFF // LANES)),
            pl.BlockSpec((None, SSM_CONV - 1, nb, CONV_DIM), lambda i: (layer, 0, i, 0)),
            pl.BlockSpec((None, nb, C_WIDTH, D_STATE), lambda i: (layer, i, 0, 0)),
            _layer_spec(layer, (SSM_CONV, CONV_DIM)),
            _layer_spec(layer, (1, CONV_DIM)),
            _layer_spec(layer, (1, LANES)),
            _layer_spec(layer, (1, LANES)),
            _layer_spec(layer, (1, C_WIDTH)),
            _layer_spec(layer, (1, C_WIDTH)),
        ],
        out_specs=[
            pl.BlockSpec((nb, C_WIDTH), lambda i: (i, 0)),
            pl.BlockSpec((SSM_CONV - 1, nb, CONV_DIM), lambda i: (0, i, 0)),
            pl.BlockSpec((nb, C_WIDTH, D_STATE), lambda i: (i, 0, 0)),
        ],
        out_shape=[
            jax.ShapeDtypeStruct((nd, C_WIDTH), F32),
            jax.ShapeDtypeStruct((SSM_CONV - 1, nd, CONV_DIM), F32),
            jax.ShapeDtypeStruct((nd, C_WIDTH, D_STATE), F32),
        ],
        compiler_params=pltpu.CompilerParams(
            dimension_semantics=("arbitrary",), vmem_limit_bytes=VMEM_LIMIT_BYTES),
        name="ssd_decode",
    )(proj, proj, proj, conv_buf_t, h0, prm["conv_w"], prm["conv_b"], prm["dt_bias"], prm["a_neg"],
      prm["d_skip"], prm["norm_g"])


def _out_ffn_core(x, mix, g1, sh2, sc2, g2, gn, wo_ref, wu_ref, cw_ref, cb_ref, wd_ref, conv_fn):
    x1 = x + g1 * _mm(mix, wo_ref[...])
    h = (_rms_norm(x1, gn) * (1.0 + sc2) + sh2).astype(BF16)
    acc = jnp.zeros(x.shape, F32)
    for c in range(D_FF // FF_CHUNK):
        cs = slice(c * FF_CHUNK, (c + 1) * FF_CHUNK)
        gate = jnp.dot(h, wu_ref[:, cs], preferred_element_type=F32)
        val = jnp.dot(h, wu_ref[:, D_FF + c * FF_CHUNK:D_FF + (c + 1) * FF_CHUNK], preferred_element_type=F32)
        gate = conv_fn(gate, cs, cw_ref[:, cs], cb_ref[:, cs])
        acc = acc + _mm(_silu(gate) * val, wd_ref[cs, :])
    return x1 + g2 * acc


def _out_ffn_prompt_kernel(tiles_per_seq, x_ref, ya_ref, yb_ref, yc_ref, g1_ref, sh2_ref, sc2_ref, g2_ref, gn_ref,
                           wo_ref, wu_ref, cw_ref, cb_ref, wd_ref, o_ref, tail_ref, carry_ref):
    i = pl.program_id(0)
    tm = x_ref.shape[0]

    @pl.when(i % tiles_per_seq == 0)
    def _():
        carry_ref[...] = jnp.zeros_like(carry_ref)

    def conv_fn(gate, cs, cw, cb):
        tail = carry_ref[:, cs]
        out = cb + cw[FFN_CONV - 1:FFN_CONV] * gate
        for k in range(1, FFN_CONV):
            out = out + cw[FFN_CONV - 1 - k:FFN_CONV - k] * _shift_rows(gate, k, tail)
        last_rows = gate[tm - SUBLANES:tm, :]
        carry_ref[:, cs] = last_rows
        tail_ref[:, cs] = last_rows
        return out

    mix = jnp.concatenate([ya_ref[...], yb_ref[...], yc_ref[...]], axis=1)
    o_ref[...] = _out_ffn_core(x_ref[...], mix, g1_ref[...], sh2_ref[...], sc2_ref[...], g2_ref[...], gn_ref[...],
                               wo_ref, wu_ref, cw_ref, cb_ref, wd_ref, conv_fn)


def _out_ffn_decode_kernel(x_ref, ya_ref, yb_ref, yc_ref, g1_ref, sh2_ref, sc2_ref, g2_ref, gn_ref,
                           wo_ref, wu_ref, cw_ref, cb_ref, wd_ref, buf_ref, o_ref, nbuf_ref):
    def conv_fn(gate, cs, cw, cb):
        out = cb + cw[FFN_CONV - 1:FFN_CONV] * gate
        for k in range(FFN_CONV - 1):
            out = out + cw[k:k + 1] * buf_ref[k, :, cs]
            nbuf_ref[k, :, cs] = buf_ref[k + 1, :, cs] if k + 1 < FFN_CONV - 1 else gate
        return out

    mix = jnp.concatenate([ya_ref[...], yb_ref[...], yc_ref[...]], axis=1)
    o_ref[...] = _out_ffn_core(x_ref[...], mix, g1_ref[...], sh2_ref[...], sc2_ref[...], g2_ref[...], gn_ref[...],
                               wo_ref, wu_ref, cw_ref, cb_ref, wd_ref, conv_fn)


def _out_ffn(x, ya, yb, yc, mod, prm, layer, tm, per_row, tiles_per_seq, conv_buf_t=None):
    rows = x.shape[0]
    nseq = rows // (tm * tiles_per_seq) if not per_row else None
    row_spec = lambda w: pl.BlockSpec((tm, w), lambda i: (i, 0))
    in_specs = [row_spec(D_MODEL), row_spec(A_WIDTH), row_spec(B_WIDTH), row_spec(C_WIDTH)]
    in_specs += [_mod_spec(layer, per_row, tiles_per_seq, c) for c in (2, 3, 4, 5)]
    in_specs += [
        _layer_spec(layer, (1, D_MODEL)),
        _layer_spec(layer, (D_MODEL, D_MODEL), single_buffer=True),
        _layer_spec(layer, (D_MODEL, 2 * D_FF), single_buffer=True),
        _layer_spec(layer, (FFN_CONV, D_FF)),
        _layer_spec(layer, (1, D_FF)),
        _layer_spec(layer, (D_FF, D_MODEL), single_buffer=True),
    ]
    args = [x, ya, yb, yc, mod, mod, mod, mod, prm["norm_ffn_g"], prm["w_out"], prm["w_up"], prm["ffn_conv_w"],
            prm["ffn_conv_b"], prm["w_down"]]
    params = pltpu.CompilerParams(dimension_semantics=("arbitrary",), vmem_limit_bytes=VMEM_LIMIT_BYTES)
    if per_row:
        buf_shape = (FFN_CONV - 1, rows, D_FF)
        return pl.pallas_call(
            _out_ffn_decode_kernel,
            grid=(rows // tm,),
            in_specs=in_specs + [pl.BlockSpec((None,) + buf_shape, lambda i: (layer, 0, i, 0))],
            out_specs=[row_spec(D_MODEL), pl.BlockSpec(buf_shape, lambda i: (0, i, 0))],
            out_shape=[jax.ShapeDtypeStruct((rows, D_MODEL), F32), jax.ShapeDtypeStruct(buf_shape, F32)],
            compiler_params=params,
            name="out_ffn_decode",
        )(*args, conv_buf_t)
    return pl.pallas_call(
        functools.partial(_out_ffn_prompt_kernel, tiles_per_seq),
        grid=(rows // tm,),
        in_specs=in_specs,
        out_specs=[row_spec(D_MODEL), pl.BlockSpec((None, SUBLANES, D_FF), lambda i: (i // tiles_per_seq, 0, 0))],
        out_shape=[jax.ShapeDtypeStruct((rows, D_MODEL), F32), jax.ShapeDtypeStruct((nseq, SUBLANES, D_FF), F32)],
        scratch_shapes=[pltpu.VMEM((SUBLANES, D_FF), F32)],
        compiler_params=params,
        name="out_ffn_prompt",
    )(*args)


def _row_param(p, width=None):
    if width is not None and p.shape[-1] != width:
        p = jnp.pad(p, ((0, 0), (0, width - p.shape[-1])))
    return p.reshape(p.shape[0], 1, p.shape[-1])


def _largest_tile(total, cap, quantum):
    tile = min(total, cap)
    while total % tile or tile % quantum:
        tile -= quantum
    return tile


def kernel(x_prompt, x_sample, c_prompt, c_sample, state_rwkv_shift, state_rwkv_wkv, cache_swa_k, cache_swa_v, state_ssm_conv, state_ssm, state_ffn_conv, ada_w, ada_b, norm_mix_g, norm_ffn_g, w_in, w_out, rwkv_mu, rwkv_w0, rwkv_w2, rwkv_a0, rwkv_a2, rwkv_g2, rwkv_k_k, rwkv_k_a, rwkv_r_k, rwkv_ln_g, rwkv_ln_b, attn_q_norm_g, attn_k_norm_g, attn_sinks, ssm_conv_w, ssm_conv_b, ssm_dt_bias, ssm_a_log, ssm_d, ssm_norm_g, ffn_w_up, ffn_conv_w, ffn_conv_b, ffn_w_down):
    depth = w_in.shape[0]
    nseq, t, _ = x_prompt.shape
    nd = x_sample.shape[0]
    assert x_sample.shape[1] == 1 and nd % DEC_BLOCK == 0 and t % SSD_CHUNK == 0

    tm = _largest_tile(t, 512, SUBLANES)
    tiles_per_seq = t // tm
    tb_rwkv = _largest_tile(t, 256, RWKV_CHUNK)
    tq_swa = _largest_tile(t, 512, SWA_BLOCK)
    tb_ssd = _largest_tile(t, 512, SSD_CHUNK)

    w_in_p = jnp.pad(w_in, ((0, 0), (0, 0), (0, PROJ_PAD - PROJ))).astype(BF16)
    rwkv_prm = {
        "mu": _row_param(rwkv_mu), "w0": _row_param(rwkv_w0), "a0": _row_param(rwkv_a0),
        "w2": jnp.pad(rwkv_w2, ((0, 0), (0, LANES - A_DECAY_LORA), (0, 0))),
        "a2": jnp.pad(rwkv_a2, ((0, 0), (A_DECAY_LORA, 0), (0, 0))),
        "g2": rwkv_g2,
        "k_k": _row_param(rwkv_k_k), "k_a": _row_param(rwkv_k_a),
        "r_k": _row_param(rwkv_r_k.reshape(depth, A_WIDTH)),
        "ln_g": _row_param(rwkv_ln_g), "ln_b": _row_param(rwkv_ln_b),
    }
    gq = _row_param(jnp.tile(attn_q_norm_g, (1, B_HEADS)))
    gk = _row_param(jnp.tile(attn_k_norm_g, (1, B_KV_HEADS)))
    ssd_prm = {
        "conv_w": ssm_conv_w, "conv_b": _row_param(ssm_conv_b),
        "dt_bias": _row_param(ssm_dt_bias, LANES),
        "a_neg": _row_param(-jnp.exp(ssm_a_log), LANES),
        "d_skip": _row_param(jnp.repeat(ssm_d, HEAD_DIM, axis=1)),
        "norm_g": _row_param(ssm_norm_g),
    }
    ffn_prm = {
        "norm_ffn_g": _row_param(norm_ffn_g), "w_out": w_out.astype(BF16), "w_up": ffn_w_up.astype(BF16),
        "ffn_conv_w": ffn_conv_w, "ffn_conv_b": _row_param(ffn_conv_b), "w_down": ffn_w_down.astype(BF16),
    }
    norm_mix = _row_param(norm_mix_g)

    c_all = jnp.concatenate([jnp.pad(c_prompt, ((0, SUBLANES - nseq), (0, 0))), c_sample], axis=0)
    mod = _ada(c_all, ada_w, ada_b)
    mod_p = mod[:, 0:nseq].reshape(depth, nseq, 1, N_MOD * D_MODEL)
    mod_s = mod[:, SUBLANES:]

    cache_k = cache_swa_k.reshape(depth, nd, WINDOW, B_KV_WIDTH)
    cache_v = cache_swa_v.reshape(depth, nd, WINDOW, B_KV_WIDTH)
    ssm_conv_t = jnp.swapaxes(state_ssm_conv, 1, 2)
    ssm_h0 = state_ssm.reshape(depth, nd, C_WIDTH, D_STATE)
    ffn_conv_t = jnp.swapaxes(state_ffn_conv, 1, 2)

    xp = x_prompt.reshape(nseq * t, D_MODEL)
    xs = x_sample.reshape(nd, D_MODEL)
    prompt_new = [[] for _ in range(7)]
    sample_new = [[] for _ in range(7)]
    for l in range(depth):
        proj = _in_proj(xp, mod_p, norm_mix, w_in_p, l, tm, 0, tiles_per_seq).reshape(nseq, t, PROJ_PAD)
        ya, p_shift, p_wkv = _rwkv_prompt(proj, rwkv_prm, l, tb_rwkv)
        yb, p_k, p_v = _swa_prompt(proj, gq, gk, attn_sinks, l, tq_swa)
        yc, p_conv, p_ssm = _ssd_prompt(proj, ssd_prm, l, tb_ssd)
        xp, p_ffn = _out_ffn(xp, ya.reshape(nseq * t, A_WIDTH), yb.reshape(nseq * t, B_WIDTH),
                             yc.reshape(nseq * t, C_WIDTH), mod_p, ffn_prm, l, tm, 0, tiles_per_seq)
        new_p = (p_shift.reshape(nseq, A_PROJ), p_wkv,
                 p_k.reshape(nseq, WINDOW, B_KV_HEADS, HEAD_DIM), p_v.reshape(nseq, WINDOW, B_KV_HEADS, HEAD_DIM),
                 p_conv[:, SUBLANES - (SSM_CONV - 1):], p_ssm.reshape(nseq, C_HEADS, HEAD_DIM, D_STATE),
                 p_ffn[:, SUBLANES - (FFN_CONV - 1):])
        proj_s = _in_proj(xs, mod_s, norm_mix, w_in_p, l, nd, nd, 1)
        ya, s_wkv = _rwkv_decode(proj_s, state_rwkv_shift, state_rwkv_wkv, rwkv_prm, l)
        yb, s_k, s_v = _swa_decode(proj_s, cache_k, cache_v, gq, gk, attn_sinks, l)
        yc, s_conv_t, s_ssm = _ssd_decode(proj_s, ssm_conv_t, ssm_h0, ssd_prm, l)
        xs, s_ffn_t = _out_ffn(xs, ya, yb, yc, mod_s, ffn_prm, l, nd, nd, 1, ffn_conv_t)
        new_s = (proj_s[:, 0:A_PROJ], s_wkv,
                 s_k.reshape(nd, WINDOW, B_KV_HEADS, HEAD_DIM), s_v.reshape(nd, WINDOW, B_KV_HEADS, HEAD_DIM),
                 jnp.swapaxes(s_conv_t, 0, 1), s_ssm.reshape(nd, C_HEADS, HEAD_DIM, D_STATE),
                 jnp.swapaxes(s_ffn_t, 0, 1))
        for lst, arr in zip(prompt_new, new_p):
            lst.append(arr)
        for lst, arr in zip(sample_new, new_s):
            lst.append(arr)
    outs_p = [jnp.stack(lst) for lst in prompt_new]
    outs_s = [jnp.stack(lst) for lst in sample_new]
    return (xp.reshape(nseq, t, D_MODEL), xs.reshape(nd, 1, D_MODEL), *outs_p, *outs_s)
```

```python
import functools

import jax
import jax.numpy as jnp
from jax import lax
from jax.experimental import pallas as pl
from jax.experimental.pallas import tpu as pltpu

F32 = jnp.float32
BF16 = jnp.bfloat16

D_MODEL = 1024
HEAD_DIM = 64
A_WIDTH = 256
A_HEADS = A_WIDTH // HEAD_DIM
A_DECAY_LORA = 64
A_AAA_LORA = 64
A_GATE_LORA = 128
A_PROJ = 3 * A_WIDTH + A_DECAY_LORA + A_AAA_LORA + A_GATE_LORA
B_WIDTH = 256
B_HEADS = B_WIDTH // HEAD_DIM
B_KV_HEADS = 2
B_GROUP = B_HEADS // B_KV_HEADS
B_KV_WIDTH = B_KV_HEADS * HEAD_DIM
B_PROJ = B_WIDTH + 2 * B_KV_WIDTH
WINDOW = 128
C_WIDTH = 512
C_HEADS = C_WIDTH // HEAD_DIM
C_GROUPS = 2
D_STATE = 128
SSM_CONV = 4
CONV_DIM = C_WIDTH + 2 * C_GROUPS * D_STATE
C_PROJ = C_WIDTH + CONV_DIM + C_HEADS
PROJ = A_PROJ + B_PROJ + C_PROJ
D_FF = 2816
FFN_CONV = 3
NORM_EPS = 1e-6
GN_EPS = 64e-5
N_MOD = 6

LANES = 128
SUBLANES = 8
VMEM_LIMIT_BYTES = 56 * 1024 * 1024

DT_OFF = A_PROJ + B_PROJ + C_WIDTH + CONV_DIM
PROJ_PAD = DT_OFF + LANES
Z_OFF = A_PROJ + B_PROJ
XBC_OFF = Z_OFF + C_WIDTH

RWKV_CHUNK = 64
INV_BLOCK = 8
SSD_CHUNK = 128
SWA_BLOCK = 128
DEC_BLOCK = 8
FF_CHUNK = D_FF // 2


def _mm(a, b):
    return jnp.dot(a.astype(BF16), b.astype(BF16), preferred_element_type=F32)


def _mm_nt(a, b):
    return lax.dot_general(a.astype(BF16), b.astype(BF16), (((1,), (1,)), ((), ())), preferred_element_type=F32)


def _mm_tn(a, b):
    return lax.dot_general(a.astype(BF16), b.astype(BF16), (((0,), (0,)), ((), ())), preferred_element_type=F32)


def _split3(x):
    hi = x.astype(BF16)
    r1 = x - hi.astype(F32)
    mid = r1.astype(BF16)
    lo = (r1 - mid.astype(F32)).astype(BF16)
    return hi, mid, lo


def _mm_sel_left(sel, x):
    s = sel.astype(BF16)
    return jnp.dot(jnp.concatenate([s, s, s], axis=1), jnp.concatenate(_split3(x), axis=0),
                   preferred_element_type=F32)


def _mm_sel_right(x, sel):
    s = sel.astype(BF16)
    return jnp.dot(jnp.concatenate(_split3(x), axis=1), jnp.concatenate([s, s, s], axis=0),
                   preferred_element_type=F32)


def _mm_nt_sel_left(sel, x):
    s = sel.astype(BF16)
    return lax.dot_general(jnp.concatenate([s, s, s], axis=1), jnp.concatenate(_split3(x), axis=1),
                           (((1,), (1,)), ((), ())), preferred_element_type=F32)


def _row_sums(tiles, width):
    ones = jnp.ones((width, width), BF16)
    half = len(tiles) // 2
    dot = lambda xs: jnp.dot(jnp.concatenate(xs, axis=0).astype(BF16), ones, preferred_element_type=F32)
    return jnp.concatenate([dot(tiles[:half]), dot(tiles[half:])], axis=0)


def _column_broadcasts(x, group):
    nb, w = x.shape
    parts = jnp.concatenate(_split3(x), axis=0)
    parts_t = lax.dot_general(_eye(w).astype(BF16), parts, (((1,), (1,)), ((), ())),
                              preferred_element_type=F32).astype(BF16)
    src = _iota((3 * nb, nb * group), 0) % nb
    dst = _iota((3 * nb, nb * group), 1) // group
    return jnp.dot(parts_t, (src == dst).astype(BF16), preferred_element_type=F32)


def _sigmoid(x):
    return 1.0 / (1.0 + jnp.exp(-x))


def _silu(x):
    return x * _sigmoid(x)


def _softplus(x):
    return jnp.maximum(x, 0.0) + jnp.log(1.0 + jnp.exp(-jnp.abs(x)))


def _iota(shape, dim):
    return lax.broadcasted_iota(jnp.int32, shape, dim)


def _eye(n):
    return (_iota((n, n), 0) == _iota((n, n), 1)).astype(F32)


def _tril_ones(n):
    return (_iota((n, n), 0) >= _iota((n, n), 1)).astype(F32)


def _group_sum(x, group):
    gid = _iota(x.shape, x.ndim - 1) // group
    out = jnp.zeros_like(x)
    for h in range(x.shape[-1] // group):
        m = gid == h
        s = jnp.sum(jnp.where(m, x, 0.0), axis=-1, keepdims=True)
        out = jnp.where(m, s, out)
    return out


def _rms_norm(x, g):
    return x * lax.rsqrt(jnp.mean(x * x, axis=-1, keepdims=True) + NORM_EPS) * g


def _head_rms_norm(x, g):
    return x * lax.rsqrt(_group_sum(x * x, HEAD_DIM) * (1.0 / HEAD_DIM) + NORM_EPS) * g


def _shift_rows(x, k, tail):
    rolled = pltpu.roll(x, k, axis=0)
    row = _iota((SUBLANES, x.shape[1]), 0)
    top = jnp.where(row < k, pltpu.roll(tail, k, axis=0), rolled[0:SUBLANES])
    return jnp.concatenate([top, rolled[SUBLANES:]], axis=0)


def _ada_kernel(c_ref, w_ref, b_ref, o_ref):
    o_ref[...] = _mm(_silu(c_ref[...]), w_ref[...]) + b_ref[...]


def _ada(c_all, ada_w, ada_b):
    depth = ada_w.shape[0]
    rows = c_all.shape[0]
    width = N_MOD * D_MODEL
    tn = 1536
    return pl.pallas_call(
        _ada_kernel,
        grid=(depth, width // tn),
        in_specs=[
            pl.BlockSpec((rows, D_MODEL), lambda l, j: (0, 0)),
            pl.BlockSpec((None, D_MODEL, tn), lambda l, j: (l, 0, j)),
            pl.BlockSpec((None, 1, tn), lambda l, j: (l, 0, j)),
        ],
        out_specs=pl.BlockSpec((None, rows, tn), lambda l, j: (l, 0, j)),
        out_shape=jax.ShapeDtypeStruct((depth, rows, width), F32),
        compiler_params=pltpu.CompilerParams(
            dimension_semantics=("arbitrary", "arbitrary"), vmem_limit_bytes=VMEM_LIMIT_BYTES),
        name="ada_mod",
    )(c_all, ada_w, ada_b.reshape(depth, 1, width))


def _in_proj_kernel(x_ref, sh_ref, sc_ref, g_ref, w_ref, o_ref):
    h = _rms_norm(x_ref[...], g_ref[...]) * (1.0 + sc_ref[...]) + sh_ref[...]
    o_ref[...] = _mm(h, w_ref[...])


def _mod_spec(layer, per_row, rows_per_seq_tiles, col):
    if per_row:
        return pl.BlockSpec((None, per_row, D_MODEL), lambda i: (layer, 0, col))
    return pl.BlockSpec((None, None, 1, D_MODEL), lambda i: (layer, i // rows_per_seq_tiles, 0, col))


def _layer_spec(layer, tail, single_buffer=False):
    idx = lambda *_: (layer,) + (0,) * len(tail)
    if single_buffer:
        return pl.BlockSpec((None,) + tuple(tail), idx, pipeline_mode=pl.Buffered(1))
    return pl.BlockSpec((None,) + tuple(tail), idx)


def _in_proj(x, mod, norm_g, w_in, layer, tm, per_row, tiles_per_seq):
    rows = x.shape[0]
    return pl.pallas_call(
        _in_proj_kernel,
        grid=(rows // tm,),
        in_specs=[
            pl.BlockSpec((tm, D_MODEL), lambda i: (i, 0)),
            _mod_spec(layer, per_row, tiles_per_seq, 0),
            _mod_spec(layer, per_row, tiles_per_seq, 1),
            _layer_spec(layer, (1, D_MODEL)),
            _layer_spec(layer, (D_MODEL, PROJ_PAD), single_buffer=True),
        ],
        out_specs=pl.BlockSpec((tm, PROJ_PAD), lambda i: (i, 0)),
        out_shape=jax.ShapeDtypeStruct((rows, PROJ_PAD), F32),
        compiler_params=pltpu.CompilerParams(
            dimension_semantics=("arbitrary",), vmem_limit_bytes=VMEM_LIMIT_BYTES),
        name="in_proj",
    )(x, mod, mod, norm_g, w_in)


_RWKV_PARAMS = ("mu", "w0", "w2", "a0", "a2", "g2", "k_k", "k_a", "r_k", "ln_g", "ln_b")


def _rwkv_prep(xs, p):
    r = xs[:, 0:A_WIDTH]
    k = xs[:, A_WIDTH:2 * A_WIDTH]
    v = xs[:, 2 * A_WIDTH:3 * A_WIDTH]
    lora_in = xs[:, 3 * A_WIDTH:3 * A_WIDTH + LANES]
    lg = xs[:, 3 * A_WIDTH + LANES:A_PROJ]
    w_log = -_softplus(-(p["w0"] + _mm(jnp.tanh(lora_in), p["w2"]))) - 0.5
    log_decay = -jnp.exp(w_log)
    a = _sigmoid(p["a0"] + _mm(lora_in, p["a2"]))
    g = _mm(_sigmoid(lg), p["g2"])
    kk = k * p["k_k"]
    kk = kk / jnp.maximum(jnp.sqrt(_group_sum(kk * kk, HEAD_DIM)), 1e-12)
    k = k * (1.0 + (a - 1.0) * p["k_a"])
    return r, k, v, log_decay, -kk, kk * a, g


def _rwkv_post(y, r, k, v, g, p):
    mean = _group_sum(y, HEAD_DIM) * (1.0 / HEAD_DIM)
    c = y - mean
    var = _group_sum(c * c, HEAD_DIM) * (1.0 / HEAD_DIM)
    yn = c * lax.rsqrt(var + GN_EPS) * p["ln_g"] + p["ln_b"]
    bonus = _group_sum(r * k * p["r_k"], HEAD_DIM) * v
    return (yn + bonus) * g


def _block_diag(x, mask):
    return jnp.where(mask, jnp.concatenate([x] * (x.shape[1] // x.shape[0]), axis=0), jnp.zeros((), x.dtype))


def _rwkv_chunk_setup(r, k, v, logd, cum, av, bv, cst):
    bd_mask, strict, incl, eye_cat, level_masks = cst
    excl = cum - logd
    cum_last = cum[RWKV_CHUNK - 1:RWKV_CHUNK, :]
    e_neg = jnp.exp(-cum)
    e_end = jnp.exp(cum_last - cum)
    at = (av * jnp.exp(excl)).astype(BF16)
    rt = (r * jnp.exp(cum)).astype(BF16)
    bt = (bv * e_neg).astype(BF16)
    kt = (k * e_neg).astype(BF16)
    vb = v.astype(BF16)
    bd = lambda x: _block_diag(x, bd_mask)

    lhs = jnp.concatenate([at, rt], axis=0)
    rhs = jnp.concatenate([bd(bt), bd(kt)], axis=0)
    amat = lax.dot_general(lhs, rhs, (((1,), (1,)), ((), ())), preferred_element_type=F32)
    yield
    n_ab = jnp.where(strict, amat[0:RWKV_CHUNK, 0:A_WIDTH], 0.0)
    n_ak = jnp.where(strict, amat[0:RWKV_CHUNK, A_WIDTH:], 0.0).astype(BF16)
    n_rb = jnp.where(incl, amat[RWKV_CHUNK:, 0:A_WIDTH], 0.0).astype(BF16)
    n_rk = jnp.where(incl, amat[RWKV_CHUNK:, A_WIDTH:], 0.0).astype(BF16)
    wy0 = jnp.dot(jnp.concatenate([n_ak, n_rk], axis=0), bd(vb), preferred_element_type=F32)
    yield
    inv = yield from _unit_lower_inverse(n_ab, eye_cat, level_masks, bd)
    return dict(
        lhs=lhs, inv=inv, w0=wy0[0:RWKV_CHUNK], y0=wy0[RWKV_CHUNK:], n_rb=n_rb, vb=vb,
        bk_end=jnp.concatenate([(bv * e_end).astype(BF16), (k * e_end).astype(BF16)], axis=0),
        decay_end=jnp.exp(cum_last),
    )


def _rwkv_chunk_apply(c, state, bd_mask):
    bd = lambda x: _block_diag(x, bd_mask)
    both = lax.dot_general(c["lhs"], state.astype(BF16), (((1,), (1,)), ((), ())), preferred_element_type=F32)
    yield
    w = both[0:RWKV_CHUNK] + c["w0"]
    ub = jnp.dot(c["inv"], bd(w.astype(BF16)), preferred_element_type=F32).astype(BF16)
    yield
    y = both[RWKV_CHUNK:] + c["y0"] + jnp.dot(c["n_rb"], bd(ub), preferred_element_type=F32)
    upd = lax.dot_general(jnp.concatenate([ub, c["vb"]], axis=0), c["bk_end"],
                          (((0,), (0,)), ((), ())), preferred_element_type=F32)
    return y, state * c["decay_end"] + jnp.where(bd_mask, upd, 0.0)


def _interleave(generators):
    results = [None] * len(generators)
    active = list(enumerate(generators))
    while active:
        still = []
        for i, gen in active:
            try:
                next(gen)
                still.append((i, gen))
            except StopIteration as done:
                results[i] = done.value
        active = still
    return results


def _split2(x):
    hi = x.astype(BF16)
    return hi, (x - hi.astype(F32)).astype(BF16)


def _mm_cat2(lhs_list, b, bd):
    rows = lhs_list[0].shape[0]
    width = b.shape[1]
    b_hi, b_lo = _split2(b)
    lhs = jnp.concatenate([part for a in lhs_list for part in _split2(a)], axis=0)
    prod = jnp.dot(lhs, jnp.concatenate([bd(b_hi), bd(b_lo)], axis=1), preferred_element_type=F32)
    outs = []
    for i in range(len(lhs_list)):
        hi = prod[2 * i * rows:(2 * i + 1) * rows]
        lo = prod[(2 * i + 1) * rows:(2 * i + 2) * rows]
        outs.append((lo[:, 0:width] + hi[:, width:]) + hi[:, 0:width])
    return outs


def _unit_lower_inverse(n, eye_cat, level_masks, bd):
    m = jnp.where(level_masks[0], n, 0.0)
    inv = eye_cat + m
    (m,) = _mm_cat2([m], m, bd)
    yield
    for step in range(INV_BLOCK.bit_length() - 2):
        if step + 1 < INV_BLOCK.bit_length() - 2:
            delta, m = _mm_cat2([inv, m], m, bd)
        else:
            (delta,) = _mm_cat2([inv], m, bd)
        yield
        inv = inv + delta
    inv = inv.astype(BF16)
    for mask in level_masks[1:]:
        off = jnp.where(mask, n, 0.0).astype(BF16)
        g = jnp.dot(inv, bd(off), preferred_element_type=F32)
        yield
        inv = inv + jnp.dot(g.astype(BF16), bd(inv), preferred_element_type=F32).astype(BF16)
        yield
    return inv


def _rwkv_consts():
    w = A_WIDTH
    bd_mask = (_iota((w, w), 0) // HEAD_DIM) == (_iota((w, w), 1) // HEAD_DIM)
    t = _iota((RWKV_CHUNK, w), 0)
    s = _iota((RWKV_CHUNK, w), 1) % RWKV_CHUNK
    level_masks = [t // INV_BLOCK == s // INV_BLOCK]
    size = INV_BLOCK
    while size < RWKV_CHUNK:
        level_masks.append((t // (2 * size) == s // (2 * size)) & (t // size != s // size))
        size *= 2
    return bd_mask, s < t, s <= t, (s == t).astype(F32), level_masks


def _load_params(refs):
    return {name: ref[...] for name, ref in zip(_RWKV_PARAMS, refs)}


def _rwkv_prompt_kernel(pa_ref, *rest):
    prm_refs = rest[:len(_RWKV_PARAMS)]
    ya_ref, shift_ref, wkv_ref, carry_ref, state_ref = rest[len(_RWKV_PARAMS):]
    i = pl.program_id(0)
    nseq, tb, _ = pa_ref.shape

    @pl.when(i == 0)
    def _():
        carry_ref[...] = jnp.zeros_like(carry_ref)
        state_ref[...] = jnp.zeros_like(state_ref)

    p = _load_params(prm_refs)
    cst = _rwkv_consts()
    n_chunks = tb // RWKV_CHUNK
    rows_all = nseq * tb
    pa = jnp.concatenate([pa_ref[n] for n in range(nseq)], axis=0)
    row = _iota(pa.shape, 0)
    prev = pltpu.roll(pa, 1, axis=0)
    for n in range(nseq):
        prev = jnp.where(row == n * tb, carry_ref[n], prev)
        last = pa_ref[n, tb - 1:tb, :]
        carry_ref[n] = last
        shift_ref[n] = last
    r, k, v, logd, av, bv, g = _rwkv_prep(pa + (prev - pa) * p["mu"], p)
    ri = _iota((rows_all, rows_all), 0)
    ci = _iota((rows_all, rows_all), 1)
    chunk_tril = ((ri // RWKV_CHUNK == ci // RWKV_CHUNK) & (ri >= ci)).astype(F32)
    cum = _mm_sel_left(chunk_tril, logd)
    chunk_rows = [slice(q * RWKV_CHUNK, (q + 1) * RWKV_CHUNK) for q in range(rows_all // RWKV_CHUNK)]
    setups = _interleave([
        _rwkv_chunk_setup(r[rows], k[rows], v[rows], logd[rows], cum[rows], av[rows], bv[rows], cst)
        for rows in chunk_rows])

    states = [state_ref[n] for n in range(nseq)]
    ys = [[None] * n_chunks for _ in range(nseq)]
    for c in range(n_chunks):
        stepped = _interleave([_rwkv_chunk_apply(setups[n * n_chunks + c], states[n], cst[0]) for n in range(nseq)])
        for n in range(nseq):
            ys[n][c], states[n] = stepped[n]
    y = jnp.concatenate([y_c for n in range(nseq) for y_c in ys[n]], axis=0)
    out = _rwkv_post(y, r, k, v, g, p)
    for n in range(nseq):
        state_ref[n] = states[n]
        ya_ref[n] = out[n * tb:(n + 1) * tb]

    @pl.when(i == pl.num_programs(0) - 1)
    def _():
        for n in range(nseq):
            for h in range(A_HEADS):
                sl = slice(h * HEAD_DIM, (h + 1) * HEAD_DIM)
                wkv_ref[n, h] = state_ref[n, sl, sl]


def _rwkv_param_specs(layer):
    shapes = {"mu": (1, A_PROJ), "w2": (LANES, A_WIDTH), "a2": (LANES, A_WIDTH), "g2": (A_GATE_LORA, A_WIDTH)}
    return [_layer_spec(layer, shapes.get(name, (1, A_WIDTH))) for name in _RWKV_PARAMS]


def _rwkv_prompt(proj, prm, layer, tb):
    nseq, t, _ = proj.shape
    return pl.pallas_call(
        _rwkv_prompt_kernel,
        grid=(t // tb,),
        in_specs=[pl.BlockSpec((nseq, tb, A_PROJ), lambda i: (0, i, 0))] + _rwkv_param_specs(layer),
        out_specs=[
            pl.BlockSpec((nseq, tb, A_WIDTH), lambda i: (0, i, 0)),
            pl.BlockSpec((nseq, 1, A_PROJ), lambda i: (0, 0, 0)),
            pl.BlockSpec((nseq, A_HEADS, HEAD_DIM, HEAD_DIM), lambda i: (0, 0, 0, 0)),
        ],
        out_shape=[
            jax.ShapeDtypeStruct((nseq, t, A_WIDTH), F32),
            jax.ShapeDtypeStruct((nseq, 1, A_PROJ), F32),
            jax.ShapeDtypeStruct((nseq, A_HEADS, HEAD_DIM, HEAD_DIM), F32),
        ],
        scratch_shapes=[pltpu.VMEM((nseq, 1, A_PROJ), F32), pltpu.VMEM((nseq, A_WIDTH, A_WIDTH), F32)],
        compiler_params=pltpu.CompilerParams(
            dimension_semantics=("arbitrary",), vmem_limit_bytes=VMEM_LIMIT_BYTES),
        name="rwkv_prompt",
    )(proj, *[prm[name] for name in _RWKV_PARAMS])


def _rwkv_decode_kernel(pa_ref, shift_ref, wkv0_ref, *rest):
    prm_refs = rest[:len(_RWKV_PARAMS)]
    _, ya_ref, wkv_ref = rest[len(_RWKV_PARAMS):]
    p = _load_params(prm_refs)
    pa = pa_ref[...]
    nb = pa.shape[0]
    r, k, v, logd, av, bv, g = _rwkv_prep(pa + (shift_ref[...] - pa) * p["mu"], p)
    decay = jnp.exp(logd)
    units = [(j, h) for j in range(nb) for h in range(A_HEADS)]
    head = lambda h: slice(h * HEAD_DIM, (h + 1) * HEAD_DIM)
    row_sums = lambda xs: _row_sums(xs, HEAD_DIM)
    v_cols = _column_broadcasts(v, LANES)
    s0 = [wkv0_ref[j, h] for j, h in units]
    sa = row_sums([s * av[j:j + 1, head(h)] for s, (j, h) in zip(s0, units)])
    sr = []
    for i, (j, h) in enumerate(units):
        rows = slice(i * HEAD_DIM, (i + 1) * HEAD_DIM)
        s = (s0[i] * decay[j:j + 1, head(h)] + sa[rows] * bv[j:j + 1, head(h)]
             + v_cols[head(h), j * LANES:j * LANES + HEAD_DIM] * k[j:j + 1, head(h)])
        wkv_ref[j, h] = s
        sr.append(s * r[j:j + 1, head(h)])
    y_cols = row_sums(sr)
    lane = _iota((A_WIDTH, nb), 1)
    y_t = jnp.zeros((A_WIDTH, nb), F32)
    for j in range(nb):
        y_t = jnp.where(lane == j, y_cols[j * A_WIDTH:(j + 1) * A_WIDTH, 0:nb], y_t)
    y = _mm_nt_sel_left(_eye(nb), y_t)
    ya_ref[...] = _rwkv_post(y, r, k, v, g, p)


def _rwkv_decode(proj, shift_prev, wkv0, prm, layer, wkv_all):
    nd = proj.shape[0]
    nb = DEC_BLOCK
    state_spec = pl.BlockSpec((None, nb, A_HEADS, HEAD_DIM, HEAD_DIM), lambda i: (layer, i, 0, 0, 0))
    n_in = 3 + len(_RWKV_PARAMS)
    return pl.pallas_call(
        _rwkv_decode_kernel,
        grid=(nd // nb,),
        in_specs=[
            pl.BlockSpec((nb, A_PROJ), lambda i: (i, 0)),
            pl.BlockSpec((None, nb, A_PROJ), lambda i: (layer, i, 0)),
            state_spec,
        ] + _rwkv_param_specs(layer) + [pl.BlockSpec(memory_space=pl.ANY)],
        out_specs=[pl.BlockSpec((nb, A_WIDTH), lambda i: (i, 0)), state_spec],
        out_shape=[jax.ShapeDtypeStruct((nd, A_WIDTH), F32), jax.ShapeDtypeStruct(wkv_all.shape, F32)],
        input_output_aliases={n_in: 1},
        compiler_params=pltpu.CompilerParams(
            dimension_semantics=("arbitrary",), vmem_limit_bytes=VMEM_LIMIT_BYTES),
        name="rwkv_decode",
    )(proj, shift_prev, wkv0, *[prm[name] for name in _RWKV_PARAMS], wkv_all)


def _place_q_heads(qn):
    lane_half = _iota((qn.shape[0], B_KV_WIDTH), 1) // HEAD_DIM
    placed = []
    for h in range(B_HEADS):
        kh, g = divmod(h, B_GROUP)
        chunk = qn[:, kh * B_KV_WIDTH:(kh + 1) * B_KV_WIDTH]
        if g != kh:
            chunk = pltpu.roll(chunk, HEAD_DIM, axis=1)
        placed.append(jnp.where(lane_half == kh, chunk, 0.0))
    return placed


def _gather_o_heads(o_heads):
    lane_half = _iota(o_heads[0].shape, 1) // HEAD_DIM
    chunks = []
    for kh in range(B_KV_HEADS):
        parts = []
        for g in range(B_GROUP):
            o = o_heads[kh * B_GROUP + g]
            parts.append(o if g == kh else pltpu.roll(o, HEAD_DIM, axis=1))
        chunks.append(jnp.where(lane_half == 0, parts[0], parts[1]))
    return jnp.concatenate(chunks, axis=1)


def _swa_prompt_kernel(layer, sinks_ref, pb_ref, gq_ref, gk_ref, yb_ref, nk_ref, nv_ref, kprev_ref, vprev_ref):
    j = pl.program_id(1)
    tq = pb_ref.shape[0]
    blk = SWA_BLOCK

    @pl.when(j == 0)
    def _():
        kprev_ref[...] = jnp.zeros_like(kprev_ref)
        vprev_ref[...] = jnp.zeros_like(vprev_ref)

    scale = HEAD_DIM ** -0.5
    rows = _iota((B_GROUP * blk, 2 * blk), 0)
    cols = _iota((B_GROUP * blk, 2 * blk), 1)
    qi = rows % blk
    band = (cols >= qi) & (cols <= qi + WINDOW)
    n_sub = tq // blk
    x = pb_ref[...]
    qn = _head_rms_norm(x[:, 0:B_WIDTH], gq_ref[...])
    kn = _head_rms_norm(x[:, B_WIDTH:B_WIDTH + B_KV_WIDTH], gk_ref[...])
    v = x[:, B_WIDTH + B_KV_WIDTH:B_PROJ]
    kfull = jnp.concatenate([kprev_ref[...], kn], axis=0)
    vfull = jnp.concatenate([vprev_ref[...], v], axis=0)
    placed = _place_q_heads(qn)

    def attend(sb, kh):
        q2 = jnp.concatenate([pl_h[sb * blk:(sb + 1) * blk] for pl_h in placed[kh * B_GROUP:(kh + 1) * B_GROUP]],
                             axis=0)
        s = _mm_nt(q2, kfull[sb * blk:(sb + 2) * blk])
        yield
        first_key = jnp.where(j * n_sub + sb > 0, 0, blk)
        s = jnp.where(band & (cols >= first_key), s * scale, -jnp.inf)
        sink = jnp.where(rows[:, 0:1] < blk, sinks_ref[layer, kh * B_GROUP], sinks_ref[layer, kh * B_GROUP + 1])
        m = jnp.maximum(jnp.max(s, axis=-1, keepdims=True), sink)
        pr = jnp.exp(s - m)
        den = jnp.sum(pr, axis=-1, keepdims=True) + jnp.exp(sink - m)
        o = _mm(pr, vfull[sb * blk:(sb + 2) * blk])
        yield
        return o / den

    outs = _interleave([attend(sb, kh) for sb in range(n_sub) for kh in range(B_KV_HEADS)])
    for sb in range(n_sub):
        heads = [outs[sb * B_KV_HEADS + kh][g * blk:(g + 1) * blk]
                 for kh in range(B_KV_HEADS) for g in range(B_GROUP)]
        yb_ref[sb * blk:(sb + 1) * blk, :] = _gather_o_heads(heads)
    kprev_ref[...] = kn[tq - blk:tq]
    vprev_ref[...] = v[tq - blk:tq]
    nk_ref[...] = kn[tq - blk:tq]
    nv_ref[...] = v[tq - blk:tq]


def _swa_prompt(proj, gq, gk, sinks, layer, tq):
    nseq, t, _ = proj.shape
    col = A_PROJ // B_PROJ
    return pl.pallas_call(
        functools.partial(_swa_prompt_kernel, layer),
        grid=(nseq, t // tq),
        in_specs=[
            pl.BlockSpec(memory_space=pltpu.SMEM),
            pl.BlockSpec((None, tq, B_PROJ), lambda n, j: (n, j, col)),
            _layer_spec(layer, (1, B_WIDTH)),
            _layer_spec(layer, (1, B_KV_WIDTH)),
        ],
        out_specs=[
            pl.BlockSpec((None, tq, B_WIDTH), lambda n, j: (n, j, 0)),
            pl.BlockSpec((None, WINDOW, B_KV_WIDTH), lambda n, j: (n, 0, 0)),
            pl.BlockSpec((None, WINDOW, B_KV_WIDTH), lambda n, j: (n, 0, 0)),
        ],
        out_shape=[
            jax.ShapeDtypeStruct((nseq, t, B_WIDTH), F32),
            jax.ShapeDtypeStruct((nseq, WINDOW, B_KV_WIDTH), F32),
            jax.ShapeDtypeStruct((nseq, WINDOW, B_KV_WIDTH), F32),
        ],
        scratch_shapes=[pltpu.VMEM((SWA_BLOCK, B_KV_WIDTH), F32), pltpu.VMEM((SWA_BLOCK, B_KV_WIDTH), F32)],
        compiler_params=pltpu.CompilerParams(
            dimension_semantics=("arbitrary", "arbitrary"), vmem_limit_bytes=VMEM_LIMIT_BYTES),
        name="swa_prompt",
    )(sinks, proj, gq, gk)


def _swa_decode_kernel(layer, sinks_ref, pb_ref, ck_ref, cv_ref, gq_ref, gk_ref, _k_all, _v_all,
                       yb_ref, nk_ref, nv_ref):
    x = pb_ref[...]
    nb = x.shape[0]
    qn = _head_rms_norm(x[:, 0:B_WIDTH], gq_ref[...])
    kn = _head_rms_norm(x[:, B_WIDTH:B_WIDTH + B_KV_WIDTH], gk_ref[...])
    v = x[:, B_WIDTH + B_KV_WIDTH:B_PROJ]
    placed = _place_q_heads(qn)
    scale = HEAD_DIM ** -0.5
    head = _iota((B_HEADS, 1), 0)
    sink = jnp.zeros((B_HEADS, 1), F32)
    for h in range(B_HEADS):
        sink = jnp.where(head == h, sinks_ref[layer, h], sink)
    last_row = _iota((WINDOW, B_KV_WIDTH), 0) == WINDOW - 1
    def attend(j):
        kc = ck_ref[j]
        vc = cv_ref[j]
        q4 = jnp.concatenate([pl_h[j:j + 1] for pl_h in placed], axis=0)
        s = _mm_nt(q4, kc) * scale
        nk_ref[j] = jnp.where(last_row, kn[j:j + 1], pltpu.roll(kc, WINDOW - 1, axis=0))
        nv_ref[j] = jnp.where(last_row, v[j:j + 1], pltpu.roll(vc, WINDOW - 1, axis=0))
        yield
        s_new = jnp.sum(q4 * kn[j:j + 1], axis=-1, keepdims=True) * scale
        m = jnp.maximum(jnp.maximum(jnp.max(s, axis=-1, keepdims=True), s_new), sink)
        pr = jnp.exp(s - m)
        p_new = jnp.exp(s_new - m)
        den = jnp.sum(pr, axis=-1, keepdims=True) + p_new + jnp.exp(sink - m)
        o = _mm(pr, vc)
        yield
        return (o + p_new * v[j:j + 1]) / den

    o_rows = _interleave([attend(j) for j in range(nb)])
    outs = [jnp.concatenate([o[h:h + 1] for o in o_rows], axis=0) for h in range(B_HEADS)]
    yb_ref[...] = _gather_o_heads(outs)


def _swa_decode(proj, cache_k, cache_v, gq, gk, sinks, layer, k_all, v_all):
    nd = proj.shape[0]
    nb = DEC_BLOCK
    col = A_PROJ // B_PROJ
    cache_spec = pl.BlockSpec((None, nb, WINDOW, B_KV_WIDTH), lambda i: (layer, i, 0, 0))
    any_spec = pl.BlockSpec(memory_space=pl.ANY)
    return pl.pallas_call(
        functools.partial(_swa_decode_kernel, layer),
        grid=(nd // nb,),
        in_specs=[
            pl.BlockSpec(memory_space=pltpu.SMEM),
            pl.BlockSpec((nb, B_PROJ), lambda i: (i, col)),
            cache_spec, cache_spec,
            _layer_spec(layer, (1, B_WIDTH)),
            _layer_spec(layer, (1, B_KV_WIDTH)),
            any_spec, any_spec,
        ],
        out_specs=[pl.BlockSpec((nb, B_WIDTH), lambda i: (i, 0)), cache_spec, cache_spec],
        out_shape=[
            jax.ShapeDtypeStruct((nd, B_WIDTH), F32),
            jax.ShapeDtypeStruct(k_all.shape, F32),
            jax.ShapeDtypeStruct(v_all.shape, F32),
        ],
        input_output_aliases={6: 1, 7: 2},
        compiler_params=pltpu.CompilerParams(
            dimension_semantics=("arbitrary",), vmem_limit_bytes=VMEM_LIMIT_BYTES),
        name="swa_decode",
    )(sinks, proj, cache_k, cache_v, gq, gk, k_all, v_all)


def _head_expand(width_per_head):
    n = C_HEADS * width_per_head
    return (_iota((LANES, n), 0) == _iota((LANES, n), 1) // width_per_head).astype(F32)


def _ssd_prompt_kernel(z_ref, xbc_ref, dt_ref, cw_ref, cb_ref, dtb_ref, aneg_ref, dskip_ref, ng_ref,
                       yc_ref, tail_ref, hout_ref, xpad_ref, h_ref):
    j = pl.program_id(1)
    tb = xbc_ref.shape[0]

    @pl.when(j == 0)
    def _():
        xpad_ref[0:SUBLANES, :] = jnp.zeros((SUBLANES, CONV_DIM), F32)
        h_ref[...] = jnp.zeros_like(h_ref)

    xpad_ref[SUBLANES:SUBLANES + tb, :] = xbc_ref[...]
    cw = cw_ref[...]
    conv = cb_ref[...]
    for k in range(SSM_CONV):
        conv = conv + cw[SSM_CONV - 1 - k:SSM_CONV - k] * xpad_ref[SUBLANES - k:SUBLANES - k + tb, :]
    last_rows = xbc_ref[tb - SUBLANES:tb, :]
    tail_ref[...] = last_rows
    xpad_ref[0:SUBLANES, :] = last_rows
    act = _silu(conv)
    x = act[:, 0:C_WIDTH]
    bm = act[:, C_WIDTH:C_WIDTH + C_GROUPS * D_STATE]
    cm = act[:, C_WIDTH + C_GROUPS * D_STATE:CONV_DIM]
    dt = _softplus(dt_ref[...] + dtb_ref[...])
    xdt = x * _mm_sel_right(dt, _head_expand(HEAD_DIM))

    ch = SSD_CHUNK
    n_chunks = tb // ch
    ri = _iota((tb, tb), 0)
    ci = _iota((tb, tb), 1)
    chunk_tril = ((ri // ch == ci // ch) & (ri >= ci)).astype(F32)
    acum = _mm_sel_left(chunk_tril, dt * aneg_ref[...])
    acum_x = _mm_sel_right(acum, _head_expand(D_STATE))
    causal = _iota((ch, ch), 0) >= _iota((ch, ch), 1)
    lane_lo = _iota((ch, 2 * HEAD_DIM), 1) < HEAD_DIM
    row_lo = _iota((2 * HEAD_DIM, D_STATE), 0) < HEAD_DIM
    heads_per_group = C_HEADS // C_GROUPS
    n_pairs = C_HEADS // 2
    head_lanes = lambda e: slice(e * D_STATE, (e + 1) * D_STATE)
    pair_lanes = lambda pr: slice(pr * 2 * HEAD_DIM, (pr + 1) * 2 * HEAD_DIM)
    group_lanes = lambda pr: slice(((2 * pr) // heads_per_group) * D_STATE, ((2 * pr) // heads_per_group + 1) * D_STATE)

    def chunk_local(c):
        rows = slice(c * ch, (c + 1) * ch)
        acx = acum_x[rows]
        ac_t = acum[rows].T
        last = acx[ch - 1:ch]
        decay_to_end = jnp.exp(last - acx)
        cbs = [_mm_nt(cm[rows, g * D_STATE:(g + 1) * D_STATE], bm[rows, g * D_STATE:(g + 1) * D_STATE])
               for g in range(C_GROUPS)]
        yield
        y_diag, states = [], []
        for pr in range(n_pairs):
            e0, e1 = 2 * pr, 2 * pr + 1
            xp = xdt[rows, pair_lanes(pr)]
            yd = []
            for e in (e0, e1):
                seg = acx[:, head_lanes(e)] - ac_t[e:e + 1, :]
                lmat = jnp.exp(jnp.where(causal, seg, -jnp.inf))
                yd.append(_mm(cbs[(2 * pr) // heads_per_group] * lmat, xp))
            y_diag.append(jnp.where(lane_lo, yd[0], yd[1]))
            scale = jnp.where(lane_lo, decay_to_end[:, head_lanes(e0)], decay_to_end[:, head_lanes(e1)])
            states.append(_mm_tn(xp * scale, bm[rows, group_lanes(pr)]))
        yield
        chunk_decay = jnp.exp(last)
        cds = [jnp.where(row_lo, chunk_decay[:, head_lanes(2 * pr)], chunk_decay[:, head_lanes(2 * pr + 1)])
               for pr in range(n_pairs)]
        return y_diag, states, cds, jnp.exp(acx)

    local = _interleave([chunk_local(c) for c in range(n_chunks)])
    h = [h_ref[pair_lanes(pr), :] for pr in range(n_pairs)]
    y_chunks = []
    for c in range(n_chunks):
        rows = slice(c * ch, (c + 1) * ch)
        y_diag, states, cds, ea = local[c]
        pairs = []
        for pr in range(n_pairs):
            y_off = _mm_nt(cm[rows, group_lanes(pr)], h[pr])
            pairs.append(y_diag[pr] + y_off * jnp.where(lane_lo, ea[:, head_lanes(2 * pr)], ea[:, head_lanes(2 * pr + 1)]))
            h[pr] = h[pr] * cds[pr] + states[pr]
        y_chunks.append(jnp.concatenate(pairs, axis=1))
    for pr in range(n_pairs):
        h_ref[pair_lanes(pr), :] = h[pr]
        hout_ref[pair_lanes(pr), :] = h[pr]

    y = (jnp.concatenate(y_chunks, axis=0) + dskip_ref[...] * x) * _silu(z_ref[...])
    yc_ref[...] = _rms_norm(y, ng_ref[...])


def _ssd_prompt(proj, prm, layer, tb):
    nseq, t, _ = proj.shape
    return pl.pallas_call(
        _ssd_prompt_kernel,
        grid=(nseq, t // tb),
        in_specs=[
            pl.BlockSpec((None, tb, C_WIDTH), lambda n, j: (n, j, Z_OFF // C_WIDTH)),
            pl.BlockSpec((None, tb, CONV_DIM), lambda n, j: (n, j, XBC_OFF // CONV_DIM)),
            pl.BlockSpec((None, tb, LANES), lambda n, j: (n, j, DT_OFF // LANES)),
            _layer_spec(layer, (SSM_CONV, CONV_DIM)),
            _layer_spec(layer, (1, CONV_DIM)),
            _layer_spec(layer, (1, LANES)),
            _layer_spec(layer, (1, LANES)),
            _layer_spec(layer, (1, C_WIDTH)),
            _layer_spec(layer, (1, C_WIDTH)),
        ],
        out_specs=[
            pl.BlockSpec((None, tb, C_WIDTH), lambda n, j: (n, j, 0)),
            pl.BlockSpec((None, SUBLANES, CONV_DIM), lambda n, j: (n, 0, 0)),
            pl.BlockSpec((None, C_WIDTH, D_STATE), lambda n, j: (n, 0, 0)),
        ],
        out_shape=[
            jax.ShapeDtypeStruct((nseq, t, C_WIDTH), F32),
            jax.ShapeDtypeStruct((nseq, SUBLANES, CONV_DIM), F32),
            jax.ShapeDtypeStruct((nseq, C_WIDTH, D_STATE), F32),
        ],
        scratch_shapes=[
            pltpu.VMEM((SUBLANES + tb, CONV_DIM), F32),
            pltpu.VMEM((C_WIDTH, D_STATE), F32),
        ],
        compiler_params=pltpu.CompilerParams(
            dimension_semantics=("arbitrary", "arbitrary"), vmem_limit_bytes=VMEM_LIMIT_BYTES),
        name="ssd_prompt",
    )(proj, proj, proj, prm["conv_w"], prm["conv_b"], prm["dt_bias"], prm["a_neg"], prm["d_skip"], prm["norm_g"])


def _ssd_decode_kernel(z_ref, xbc_ref, dt_ref, buf_ref, h0_ref, cw_ref, cb_ref, dtb_ref, aneg_ref, dskip_ref, ng_ref,
                       _h_all, yc_ref, nbuf_ref, hout_ref):
    xbc = xbc_ref[...]
    nb = xbc.shape[0]
    cw = cw_ref[...]
    conv = cb_ref[...] + cw[SSM_CONV - 1:SSM_CONV] * xbc
    for k in range(SSM_CONV - 1):
        conv = conv + cw[k:k + 1] * buf_ref[k]
        nbuf_ref[k] = buf_ref[k + 1] if k + 1 < SSM_CONV - 1 else xbc
    act = _silu(conv)
    x = act[:, 0:C_WIDTH]
    bm = act[:, C_WIDTH:C_WIDTH + C_GROUPS * D_STATE]
    cm = act[:, C_WIDTH + C_GROUPS * D_STATE:CONV_DIM]
    dt = _softplus(dt_ref[...] + dtb_ref[...])
    decay = _mm_sel_right(jnp.exp(dt * aneg_ref[...]), _head_expand(D_STATE))
    xdt = x * _mm_sel_right(dt, _head_expand(HEAD_DIM))
    heads_per_group = C_HEADS // C_GROUPS
    x_cols = _column_broadcasts(xdt, D_STATE)
    hc = []
    for j in range(nb):
        for e in range(C_HEADS):
            gs = slice((e // heads_per_group) * D_STATE, (e // heads_per_group + 1) * D_STATE)
            hs = slice(e * HEAD_DIM, (e + 1) * HEAD_DIM)
            h = (h0_ref[j, hs, :] * decay[j:j + 1, e * D_STATE:(e + 1) * D_STATE]
                 + x_cols[hs, j * D_STATE:(j + 1) * D_STATE] * bm[j:j + 1, gs])
            hout_ref[j, hs, :] = h
            hc.append(h * cm[j:j + 1, gs])
    y_cols = _row_sums(hc, D_STATE)
    lane = _iota((C_WIDTH, nb), 1)
    y_t = jnp.zeros((C_WIDTH, nb), F32)
    for j in range(nb):
        y_t = jnp.where(lane == j, y_cols[j * C_WIDTH:(j + 1) * C_WIDTH, 0:nb], y_t)
    y = _mm_nt_sel_left(_eye(nb), y_t)
    y = (y + dskip_ref[...] * x) * _silu(z_ref[...])
    yc_ref[...] = _rms_norm(y, ng_ref[...])


def _ssd_decode(proj, conv_buf_t, h0, prm, layer, h_all):
    nd = proj.shape[0]
    nb = DEC_BLOCK
    state_spec = pl.BlockSpec((None, nb, C_WIDTH, D_STATE), lambda i: (layer, i, 0, 0))
    return pl.pallas_call(
        _ssd_decode_kernel,
        grid=(nd // nb,),
        in_specs=[
            pl.BlockSpec((nb, C_WIDTH), lambda i: (i, Z_OFF // C_WIDTH)),
            pl.BlockSpec((nb, CONV_DIM), lambda i: (i, XBC_OFF // CONV_DIM)),
            pl.BlockSpec((nb, LANES), lambda i: (i, DT_OFF // LANES)),
            pl.BlockSpec((None, SSM_CONV - 1, nb, CONV_DIM), lambda i: (layer, 0, i, 0)),
            state_spec,
            _layer_spec(layer, (SSM_CONV, CONV_DIM)),
            _layer_spec(layer, (1, CONV_DIM)),
            _layer_spec(layer, (1, LANES)),
            _layer_spec(layer, (1, LANES)),
            _layer_spec(layer, (1, C_WIDTH)),
            _layer_spec(layer, (1, C_WIDTH)),
            pl.BlockSpec(memory_space=pl.ANY),
        ],
        out_specs=[
            pl.BlockSpec((nb, C_WIDTH), lambda i: (i, 0)),
            pl.BlockSpec((SSM_CONV - 1, nb, CONV_DIM), lambda i: (0, i, 0)),
            state_spec,
        ],
        out_shape=[
            jax.ShapeDtypeStruct((nd, C_WIDTH), F32),
            jax.ShapeDtypeStruct((SSM_CONV - 1, nd, CONV_DIM), F32),
            jax.ShapeDtypeStruct(h_all.shape, F32),
        ],
        input_output_aliases={11: 2},
        compiler_params=pltpu.CompilerParams(
            dimension_semantics=("arbitrary",), vmem_limit_bytes=VMEM_LIMIT_BYTES),
        name="ssd_decode",
    )(proj, proj, proj, conv_buf_t, h0, prm["conv_w"], prm["conv_b"], prm["dt_bias"], prm["a_neg"],
      prm["d_skip"], prm["norm_g"], h_all)


def _out_ffn_core(x, mix, g1, sh2, sc2, g2, gn, wo_ref, wu_ref, cw_ref, cb_ref, wd_ref, conv_fn):
    x1 = x + g1 * _mm(mix, wo_ref[...])
    h = (_rms_norm(x1, gn) * (1.0 + sc2) + sh2).astype(BF16)
    acc = jnp.zeros(x.shape, F32)
    for c in range(D_FF // FF_CHUNK):
        cs = slice(c * FF_CHUNK, (c + 1) * FF_CHUNK)
        gate = jnp.dot(h, wu_ref[:, cs], preferred_element_type=F32)
        val = jnp.dot(h, wu_ref[:, D_FF + c * FF_CHUNK:D_FF + (c + 1) * FF_CHUNK], preferred_element_type=F32)
        gate = conv_fn(gate, cs, cw_ref[:, cs], cb_ref[:, cs])
        acc = acc + _mm(_silu(gate) * val, wd_ref[cs, :])
    return x1 + g2 * acc


def _out_ffn_prompt_kernel(tiles_per_seq, x_ref, ya_ref, yb_ref, yc_ref, g1_ref, sh2_ref, sc2_ref, g2_ref, gn_ref,
                           wo_ref, wu_ref, cw_ref, cb_ref, wd_ref, o_ref, tail_ref, carry_ref):
    i = pl.program_id(0)
    tm = x_ref.shape[0]

    @pl.when(i % tiles_per_seq == 0)
    def _():
        carry_ref[...] = jnp.zeros_like(carry_ref)

    def conv_fn(gate, cs, cw, cb):
        tail = carry_ref[:, cs]
        out = cb + cw[FFN_CONV - 1:FFN_CONV] * gate
        for k in range(1, FFN_CONV):
            out = out + cw[FFN_CONV - 1 - k:FFN_CONV - k] * _shift_rows(gate, k, tail)
        last_rows = gate[tm - SUBLANES:tm, :]
        carry_ref[:, cs] = last_rows
        tail_ref[:, cs] = last_rows
        return out

    mix = jnp.concatenate([ya_ref[...], yb_ref[...], yc_ref[...]], axis=1)
    o_ref[...] = _out_ffn_core(x_ref[...], mix, g1_ref[...], sh2_ref[...], sc2_ref[...], g2_ref[...], gn_ref[...],
                               wo_ref, wu_ref, cw_ref, cb_ref, wd_ref, conv_fn)


def _out_ffn_decode_kernel(x_ref, ya_ref, yb_ref, yc_ref, g1_ref, sh2_ref, sc2_ref, g2_ref, gn_ref,
                           wo_ref, wu_ref, cw_ref, cb_ref, wd_ref, buf_ref, o_ref, nbuf_ref):
    def conv_fn(gate, cs, cw, cb):
        out = cb + cw[FFN_CONV - 1:FFN_CONV] * gate
        for k in range(FFN_CONV - 1):
            out = out + cw[k:k + 1] * buf_ref[k, :, cs]
            nbuf_ref[k, :, cs] = buf_ref[k + 1, :, cs] if k + 1 < FFN_CONV - 1 else gate
        return out

    mix = jnp.concatenate([ya_ref[...], yb_ref[...], yc_ref[...]], axis=1)
    o_ref[...] = _out_ffn_core(x_ref[...], mix, g1_ref[...], sh2_ref[...], sc2_ref[...], g2_ref[...], gn_ref[...],
                               wo_ref, wu_ref, cw_ref, cb_ref, wd_ref, conv_fn)


def _out_ffn(x, ya, yb, yc, mod, prm, layer, tm, per_row, tiles_per_seq, conv_buf_t=None):
    rows = x.shape[0]
    nseq = rows // (tm * tiles_per_seq) if not per_row else None
    row_spec = lambda w: pl.BlockSpec((tm, w), lambda i: (i, 0))
    in_specs = [row_spec(D_MODEL), row_spec(A_WIDTH), row_spec(B_WIDTH), row_spec(C_WIDTH)]
    in_specs += [_mod_spec(layer, per_row, tiles_per_seq, c) for c in (2, 3, 4, 5)]
    in_specs += [
        _layer_spec(layer, (1, D_MODEL)),
        _layer_spec(layer, (D_MODEL, D_MODEL), single_buffer=True),
        _layer_spec(layer, (D_MODEL, 2 * D_FF), single_buffer=True),
        _layer_spec(layer, (FFN_CONV, D_FF)),
        _layer_spec(layer, (1, D_FF)),
        _layer_spec(layer, (D_FF, D_MODEL), single_buffer=True),
    ]
    args = [x, ya, yb, yc, mod, mod, mod, mod, prm["norm_ffn_g"], prm["w_out"], prm["w_up"], prm["ffn_conv_w"],
            prm["ffn_conv_b"], prm["w_down"]]
    params = pltpu.CompilerParams(dimension_semantics=("arbitrary",), vmem_limit_bytes=VMEM_LIMIT_BYTES)
    if per_row:
        buf_shape = (FFN_CONV - 1, rows, D_FF)
        return pl.pallas_call(
            _out_ffn_decode_kernel,
            grid=(rows // tm,),
            in_specs=in_specs + [pl.BlockSpec((None,) + buf_shape, lambda i: (layer, 0, i, 0))],
            out_specs=[row_spec(D_MODEL), pl.BlockSpec(buf_shape, lambda i: (0, i, 0))],
            out_shape=[jax.ShapeDtypeStruct((rows, D_MODEL), F32), jax.ShapeDtypeStruct(buf_shape, F32)],
            compiler_params=params,
            name="out_ffn_decode",
        )(*args, conv_buf_t)
    return pl.pallas_call(
        functools.partial(_out_ffn_prompt_kernel, tiles_per_seq),
        grid=(rows // tm,),
        in_specs=in_specs,
        out_specs=[row_spec(D_MODEL), pl.BlockSpec((None, SUBLANES, D_FF), lambda i: (i // tiles_per_seq, 0, 0))],
        out_shape=[jax.ShapeDtypeStruct((rows, D_MODEL), F32), jax.ShapeDtypeStruct((nseq, SUBLANES, D_FF), F32)],
        scratch_shapes=[pltpu.VMEM((SUBLANES, D_FF), F32)],
        compiler_params=params,
        name="out_ffn_prompt",
    )(*args)


def _row_param(p, width=None):
    if width is not None and p.shape[-1] != width:
        p = jnp.pad(p, ((0, 0), (0, width - p.shape[-1])))
    return p.reshape(p.shape[0], 1, p.shape[-1])


def _largest_tile(total, cap, quantum):
    tile = min(total, cap)
    while total % tile or tile % quantum:
        tile -= quantum
    return tile


def kernel(x_prompt, x_sample, c_prompt, c_sample, state_rwkv_shift, state_rwkv_wkv, cache_swa_k, cache_swa_v, state_ssm_conv, state_ssm, state_ffn_conv, ada_w, ada_b, norm_mix_g, norm_ffn_g, w_in, w_out, rwkv_mu, rwkv_w0, rwkv_w2, rwkv_a0, rwkv_a2, rwkv_g2, rwkv_k_k, rwkv_k_a, rwkv_r_k, rwkv_ln_g, rwkv_ln_b, attn_q_norm_g, attn_k_norm_g, attn_sinks, ssm_conv_w, ssm_conv_b, ssm_dt_bias, ssm_a_log, ssm_d, ssm_norm_g, ffn_w_up, ffn_conv_w, ffn_conv_b, ffn_w_down):
    depth = w_in.shape[0]
    nseq, t, _ = x_prompt.shape
    nd = x_sample.shape[0]
    assert x_sample.shape[1] == 1 and nd % DEC_BLOCK == 0 and t % SSD_CHUNK == 0

    tm = _largest_tile(t, 512, SUBLANES)
    tiles_per_seq = t // tm
    tb_rwkv = _largest_tile(t, 256, RWKV_CHUNK)
    tq_swa = _largest_tile(t, 512, SWA_BLOCK)
    tb_ssd = _largest_tile(t, 512, SSD_CHUNK)

    w_in_p = jnp.pad(w_in, ((0, 0), (0, 0), (0, PROJ_PAD - PROJ))).astype(BF16)
    rwkv_prm = {
        "mu": _row_param(rwkv_mu), "w0": _row_param(rwkv_w0), "a0": _row_param(rwkv_a0),
        "w2": jnp.pad(rwkv_w2, ((0, 0), (0, LANES - A_DECAY_LORA), (0, 0))),
        "a2": jnp.pad(rwkv_a2, ((0, 0), (A_DECAY_LORA, 0), (0, 0))),
        "g2": rwkv_g2,
        "k_k": _row_param(rwkv_k_k), "k_a": _row_param(rwkv_k_a),
        "r_k": _row_param(rwkv_r_k.reshape(depth, A_WIDTH)),
        "ln_g": _row_param(rwkv_ln_g), "ln_b": _row_param(rwkv_ln_b),
    }
    gq = _row_param(jnp.tile(attn_q_norm_g, (1, B_HEADS)))
    gk = _row_param(jnp.tile(attn_k_norm_g, (1, B_KV_HEADS)))
    ssd_prm = {
        "conv_w": ssm_conv_w, "conv_b": _row_param(ssm_conv_b),
        "dt_bias": _row_param(ssm_dt_bias, LANES),
        "a_neg": _row_param(-jnp.exp(ssm_a_log), LANES),
        "d_skip": _row_param(jnp.repeat(ssm_d, HEAD_DIM, axis=1)),
        "norm_g": _row_param(ssm_norm_g),
    }
    ffn_prm = {
        "norm_ffn_g": _row_param(norm_ffn_g), "w_out": w_out.astype(BF16), "w_up": ffn_w_up.astype(BF16),
        "ffn_conv_w": ffn_conv_w, "ffn_conv_b": _row_param(ffn_conv_b), "w_down": ffn_w_down.astype(BF16),
    }
    norm_mix = _row_param(norm_mix_g)

    c_all = jnp.concatenate([jnp.pad(c_prompt, ((0, SUBLANES - nseq), (0, 0))), c_sample], axis=0)
    mod = _ada(c_all, ada_w, ada_b)
    mod_p = mod[:, 0:nseq].reshape(depth, nseq, 1, N_MOD * D_MODEL)
    mod_s = mod[:, SUBLANES:]

    cache_k = cache_swa_k.reshape(depth, nd, WINDOW, B_KV_WIDTH)
    cache_v = cache_swa_v.reshape(depth, nd, WINDOW, B_KV_WIDTH)
    ssm_conv_t = jnp.swapaxes(state_ssm_conv, 1, 2)
    ssm_h0 = state_ssm.reshape(depth, nd, C_WIDTH, D_STATE)
    ffn_conv_t = jnp.swapaxes(state_ffn_conv, 1, 2)

    xp = x_prompt.reshape(nseq * t, D_MODEL)
    xs = x_sample.reshape(nd, D_MODEL)
    prompt_new = [[] for _ in range(7)]
    sample_new = [[] for _ in range(3)]
    s_wkv = jnp.zeros(state_rwkv_wkv.shape, F32)
    s_k = jnp.zeros(cache_k.shape, F32)
    s_v = jnp.zeros(cache_v.shape, F32)
    s_ssm = jnp.zeros(ssm_h0.shape, F32)
    for l in range(depth):
        proj = _in_proj(xp, mod_p, norm_mix, w_in_p, l, tm, 0, tiles_per_seq).reshape(nseq, t, PROJ_PAD)
        ya, p_shift, p_wkv = _rwkv_prompt(proj, rwkv_prm, l, tb_rwkv)
        yb, p_k, p_v = _swa_prompt(proj, gq, gk, attn_sinks, l, tq_swa)
        yc, p_conv, p_ssm = _ssd_prompt(proj, ssd_prm, l, tb_ssd)
        xp, p_ffn = _out_ffn(xp, ya.reshape(nseq * t, A_WIDTH), yb.reshape(nseq * t, B_WIDTH),
                             yc.reshape(nseq * t, C_WIDTH), mod_p, ffn_prm, l, tm, 0, tiles_per_seq)
        new_p = (p_shift.reshape(nseq, A_PROJ), p_wkv,
                 p_k.reshape(nseq, WINDOW, B_KV_HEADS, HEAD_DIM), p_v.reshape(nseq, WINDOW, B_KV_HEADS, HEAD_DIM),
                 p_conv[:, SUBLANES - (SSM_CONV - 1):], p_ssm.reshape(nseq, C_HEADS, HEAD_DIM, D_STATE),
                 p_ffn[:, SUBLANES - (FFN_CONV - 1):])
        proj_s = _in_proj(xs, mod_s, norm_mix, w_in_p, l, nd, nd, 1)
        ya, s_wkv = _rwkv_decode(proj_s, state_rwkv_shift, state_rwkv_wkv, rwkv_prm, l, s_wkv)
        yb, s_k, s_v = _swa_decode(proj_s, cache_k, cache_v, gq, gk, attn_sinks, l, s_k, s_v)
        yc, s_conv_t, s_ssm = _ssd_decode(proj_s, ssm_conv_t, ssm_h0, ssd_prm, l, s_ssm)
        xs, s_ffn_t = _out_ffn(xs, ya, yb, yc, mod_s, ffn_prm, l, nd, nd, 1, ffn_conv_t)
        new_s = (proj_s[:, 0:A_PROJ], jnp.swapaxes(s_conv_t, 0, 1), jnp.swapaxes(s_ffn_t, 0, 1))
        for lst, arr in zip(prompt_new, new_p):
            lst.append(arr)
        for lst, arr in zip(sample_new, new_s):
            lst.append(arr)
    outs_p = [jnp.stack(lst) for lst in prompt_new]
    s_shift, s_conv, s_ffn = [jnp.stack(lst) for lst in sample_new]
    outs_s = (s_shift, s_wkv, s_k.reshape(depth, nd, WINDOW, B_KV_HEADS, HEAD_DIM),
              s_v.reshape(depth, nd, WINDOW, B_KV_HEADS, HEAD_DIM), s_conv,
              s_ssm.reshape(depth, nd, C_HEADS, HEAD_DIM, D_STATE), s_ffn)
    return (xp.reshape(nseq, t, D_MODEL), xs.reshape(nd, 1, D_MODEL), *outs_p, *outs_s)
```

```python
import functools

import jax
import jax.numpy as jnp
from jax import lax
from jax.experimental import pallas as pl
from jax.experimental.pallas import tpu as pltpu

F32 = jnp.float32
BF16 = jnp.bfloat16

D_MODEL = 1024
HEAD_DIM = 64
A_WIDTH = 256
A_HEADS = A_WIDTH // HEAD_DIM
A_DECAY_LORA = 64
A_AAA_LORA = 64
A_GATE_LORA = 128
A_PROJ = 3 * A_WIDTH + A_DECAY_LORA + A_AAA_LORA + A_GATE_LORA
B_WIDTH = 256
B_HEADS = B_WIDTH // HEAD_DIM
B_KV_HEADS = 2
B_GROUP = B_HEADS // B_KV_HEADS
B_KV_WIDTH = B_KV_HEADS * HEAD_DIM
B_PROJ = B_WIDTH + 2 * B_KV_WIDTH
WINDOW = 128
C_WIDTH = 512
C_HEADS = C_WIDTH // HEAD_DIM
C_GROUPS = 2
D_STATE = 128
SSM_CONV = 4
CONV_DIM = C_WIDTH + 2 * C_GROUPS * D_STATE
C_PROJ = C_WIDTH + CONV_DIM + C_HEADS
PROJ = A_PROJ + B_PROJ + C_PROJ
D_FF = 2816
FFN_CONV = 3
NORM_EPS = 1e-6
GN_EPS = 64e-5
N_MOD = 6

LANES = 128
SUBLANES = 8
VMEM_LIMIT_BYTES = 56 * 1024 * 1024

DT_OFF = A_PROJ + B_PROJ + C_WIDTH + CONV_DIM
PROJ_PAD = DT_OFF + LANES
Z_OFF = A_PROJ + B_PROJ
XBC_OFF = Z_OFF + C_WIDTH

RWKV_CHUNK = 64
INV_BLOCK = 8
RWKV_GROUP = 4
SSD_CHUNK = 128
SWA_BLOCK = 128
DEC_BLOCK = 8
FF_CHUNK = D_FF // 2


def _mm(a, b):
    return jnp.dot(a.astype(BF16), b.astype(BF16), preferred_element_type=F32)


def _mm_nt(a, b):
    return lax.dot_general(a.astype(BF16), b.astype(BF16), (((1,), (1,)), ((), ())), preferred_element_type=F32)


def _mm_tn(a, b):
    return lax.dot_general(a.astype(BF16), b.astype(BF16), (((0,), (0,)), ((), ())), preferred_element_type=F32)


def _split3(x):
    hi = x.astype(BF16)
    r1 = x - hi.astype(F32)
    mid = r1.astype(BF16)
    lo = (r1 - mid.astype(F32)).astype(BF16)
    return hi, mid, lo


def _mm_sel_left(sel, x):
    s = sel.astype(BF16)
    return jnp.dot(jnp.concatenate([s, s, s], axis=1), jnp.concatenate(_split3(x), axis=0),
                   preferred_element_type=F32)


def _mm_sel_right(x, sel):
    s = sel.astype(BF16)
    return jnp.dot(jnp.concatenate(_split3(x), axis=1), jnp.concatenate([s, s, s], axis=0),
                   preferred_element_type=F32)


def _mm_nt_sel_left(sel, x):
    s = sel.astype(BF16)
    return lax.dot_general(jnp.concatenate([s, s, s], axis=1), jnp.concatenate(_split3(x), axis=1),
                           (((1,), (1,)), ((), ())), preferred_element_type=F32)


def _row_sums(tiles, width):
    ones = jnp.ones((width, width), BF16)
    half = len(tiles) // 2
    dot = lambda xs: jnp.dot(jnp.concatenate(xs, axis=0).astype(BF16), ones, preferred_element_type=F32)
    return jnp.concatenate([dot(tiles[:half]), dot(tiles[half:])], axis=0)


def _column_broadcasts(x, group):
    nb, w = x.shape
    parts = jnp.concatenate(_split3(x), axis=0)
    parts_t = lax.dot_general(_eye(w).astype(BF16), parts, (((1,), (1,)), ((), ())),
                              preferred_element_type=F32).astype(BF16)
    src = _iota((3 * nb, nb * group), 0) % nb
    dst = _iota((3 * nb, nb * group), 1) // group
    return jnp.dot(parts_t, (src == dst).astype(BF16), preferred_element_type=F32)


def _sigmoid(x):
    return 1.0 / (1.0 + jnp.exp(-x))


def _silu(x):
    return x * _sigmoid(x)


def _softplus(x):
    return jnp.maximum(x, 0.0) + jnp.log(1.0 + jnp.exp(-jnp.abs(x)))


def _iota(shape, dim):
    return lax.broadcasted_iota(jnp.int32, shape, dim)


def _eye(n):
    return (_iota((n, n), 0) == _iota((n, n), 1)).astype(F32)


def _tril_ones(n):
    return (_iota((n, n), 0) >= _iota((n, n), 1)).astype(F32)


def _group_sum(x, group):
    gid = _iota(x.shape, x.ndim - 1) // group
    out = jnp.zeros_like(x)
    for h in range(x.shape[-1] // group):
        m = gid == h
        s = jnp.sum(jnp.where(m, x, 0.0), axis=-1, keepdims=True)
        out = jnp.where(m, s, out)
    return out


def _rms_norm(x, g):
    return x * lax.rsqrt(jnp.mean(x * x, axis=-1, keepdims=True) + NORM_EPS) * g


def _head_rms_norm(x, g):
    return x * lax.rsqrt(_group_sum(x * x, HEAD_DIM) * (1.0 / HEAD_DIM) + NORM_EPS) * g


def _shift_rows(x, k, tail):
    rolled = pltpu.roll(x, k, axis=0)
    row = _iota((SUBLANES, x.shape[1]), 0)
    top = jnp.where(row < k, pltpu.roll(tail, k, axis=0), rolled[0:SUBLANES])
    return jnp.concatenate([top, rolled[SUBLANES:]], axis=0)


def _ada_kernel(c_ref, w_ref, b_ref, o_ref):
    o_ref[...] = _mm(_silu(c_ref[...]), w_ref[...]) + b_ref[...]


def _ada(c_all, ada_w, ada_b):
    depth = ada_w.shape[0]
    rows = c_all.shape[0]
    width = N_MOD * D_MODEL
    tn = 1536
    return pl.pallas_call(
        _ada_kernel,
        grid=(depth, width // tn),
        in_specs=[
            pl.BlockSpec((rows, D_MODEL), lambda l, j: (0, 0)),
            pl.BlockSpec((None, D_MODEL, tn), lambda l, j: (l, 0, j)),
            pl.BlockSpec((None, 1, tn), lambda l, j: (l, 0, j)),
        ],
        out_specs=pl.BlockSpec((None, rows, tn), lambda l, j: (l, 0, j)),
        out_shape=jax.ShapeDtypeStruct((depth, rows, width), F32),
        compiler_params=pltpu.CompilerParams(
            dimension_semantics=("arbitrary", "arbitrary"), vmem_limit_bytes=VMEM_LIMIT_BYTES),
        name="ada_mod",
    )(c_all, ada_w, ada_b.reshape(depth, 1, width))


def _in_proj_kernel(x_ref, sh_ref, sc_ref, g_ref, w_ref, o_ref):
    h = _rms_norm(x_ref[...], g_ref[...]) * (1.0 + sc_ref[...]) + sh_ref[...]
    o_ref[...] = _mm(h, w_ref[...])


def _mod_spec(layer, per_row, rows_per_seq_tiles, col):
    if per_row:
        return pl.BlockSpec((None, per_row, D_MODEL), lambda i: (layer, 0, col))
    return pl.BlockSpec((None, None, 1, D_MODEL), lambda i: (layer, i // rows_per_seq_tiles, 0, col))


def _layer_spec(layer, tail, single_buffer=False):
    idx = lambda *_: (layer,) + (0,) * len(tail)
    if single_buffer:
        return pl.BlockSpec((None,) + tuple(tail), idx, pipeline_mode=pl.Buffered(1))
    return pl.BlockSpec((None,) + tuple(tail), idx)


def _in_proj(x, mod, norm_g, w_in, layer, tm, per_row, tiles_per_seq):
    rows = x.shape[0]
    return pl.pallas_call(
        _in_proj_kernel,
        grid=(rows // tm,),
        in_specs=[
            pl.BlockSpec((tm, D_MODEL), lambda i: (i, 0)),
            _mod_spec(layer, per_row, tiles_per_seq, 0),
            _mod_spec(layer, per_row, tiles_per_seq, 1),
            _layer_spec(layer, (1, D_MODEL)),
            _layer_spec(layer, (D_MODEL, PROJ_PAD), single_buffer=True),
        ],
        out_specs=pl.BlockSpec((tm, PROJ_PAD), lambda i: (i, 0)),
        out_shape=jax.ShapeDtypeStruct((rows, PROJ_PAD), F32),
        compiler_params=pltpu.CompilerParams(
            dimension_semantics=("arbitrary",), vmem_limit_bytes=VMEM_LIMIT_BYTES),
        name="in_proj",
    )(x, mod, mod, norm_g, w_in)


_RWKV_PARAMS = ("mu", "w0", "w2", "a0", "a2", "g2", "k_k", "k_a", "r_k", "ln_g", "ln_b")


def _rwkv_prep(xs, p):
    r = xs[:, 0:A_WIDTH]
    k = xs[:, A_WIDTH:2 * A_WIDTH]
    v = xs[:, 2 * A_WIDTH:3 * A_WIDTH]
    lora_in = xs[:, 3 * A_WIDTH:3 * A_WIDTH + LANES]
    lg = xs[:, 3 * A_WIDTH + LANES:A_PROJ]
    w_log = -_softplus(-(p["w0"] + _mm(jnp.tanh(lora_in), p["w2"]))) - 0.5
    log_decay = -jnp.exp(w_log)
    a = _sigmoid(p["a0"] + _mm(lora_in, p["a2"]))
    g = _mm(_sigmoid(lg), p["g2"])
    kk = k * p["k_k"]
    kk = kk / jnp.maximum(jnp.sqrt(_group_sum(kk * kk, HEAD_DIM)), 1e-12)
    k = k * (1.0 + (a - 1.0) * p["k_a"])
    return r, k, v, log_decay, -kk, kk * a, g


def _rwkv_post(y, r, k, v, g, p):
    mean = _group_sum(y, HEAD_DIM) * (1.0 / HEAD_DIM)
    c = y - mean
    var = _group_sum(c * c, HEAD_DIM) * (1.0 / HEAD_DIM)
    yn = c * lax.rsqrt(var + GN_EPS) * p["ln_g"] + p["ln_b"]
    bonus = _group_sum(r * k * p["r_k"], HEAD_DIM) * v
    return (yn + bonus) * g


def _block_diag(x, mask):
    return jnp.where(mask, jnp.concatenate([x] * (x.shape[1] // x.shape[0]), axis=0), jnp.zeros((), x.dtype))


def _rwkv_chunk_setup(r, k, v, logd, cum, av, bv, cst):
    bd_mask, strict, incl, eye_cat, level_masks = cst
    excl = cum - logd
    cum_last = cum[RWKV_CHUNK - 1:RWKV_CHUNK, :]
    e_neg = jnp.exp(-cum)
    e_end = jnp.exp(cum_last - cum)
    at = (av * jnp.exp(excl)).astype(BF16)
    rt = (r * jnp.exp(cum)).astype(BF16)
    bt = (bv * e_neg).astype(BF16)
    kt = (k * e_neg).astype(BF16)
    vb = v.astype(BF16)
    bd = lambda x: _block_diag(x, bd_mask)

    lhs = jnp.concatenate([at, rt], axis=0)
    rhs = jnp.concatenate([bd(bt), bd(kt)], axis=0)
    amat = lax.dot_general(lhs, rhs, (((1,), (1,)), ((), ())), preferred_element_type=F32)
    yield
    n_ab = jnp.where(strict, amat[0:RWKV_CHUNK, 0:A_WIDTH], 0.0)
    n_ak = jnp.where(strict, amat[0:RWKV_CHUNK, A_WIDTH:], 0.0).astype(BF16)
    n_rb = jnp.where(incl, amat[RWKV_CHUNK:, 0:A_WIDTH], 0.0).astype(BF16)
    n_rk = jnp.where(incl, amat[RWKV_CHUNK:, A_WIDTH:], 0.0).astype(BF16)
    wy0 = jnp.dot(jnp.concatenate([n_ak, n_rk], axis=0), bd(vb), preferred_element_type=F32)
    yield
    inv = yield from _unit_lower_inverse(n_ab, eye_cat, level_masks, bd)
    return dict(
        lhs=lhs, inv=inv, w0=wy0[0:RWKV_CHUNK], y0=wy0[RWKV_CHUNK:], n_rb=n_rb, vb=vb,
        bk_end=jnp.concatenate([(bv * e_end).astype(BF16), (k * e_end).astype(BF16)], axis=0),
        decay_end=jnp.exp(cum_last),
    )


def _rwkv_chunk_apply(c, state, bd_mask):
    bd = lambda x: _block_diag(x, bd_mask)
    both = lax.dot_general(c["lhs"], state.astype(BF16), (((1,), (1,)), ((), ())), preferred_element_type=F32)
    yield
    w = both[0:RWKV_CHUNK] + c["w0"]
    ub = jnp.dot(c["inv"], bd(w.astype(BF16)), preferred_element_type=F32).astype(BF16)
    yield
    y = both[RWKV_CHUNK:] + c["y0"] + jnp.dot(c["n_rb"], bd(ub), preferred_element_type=F32)
    upd = lax.dot_general(jnp.concatenate([ub, c["vb"]], axis=0), c["bk_end"],
                          (((0,), (0,)), ((), ())), preferred_element_type=F32)
    return y, state * c["decay_end"] + jnp.where(bd_mask, upd, 0.0)


def _interleave(generators):
    results = [None] * len(generators)
    active = list(enumerate(generators))
    while active:
        still = []
        for i, gen in active:
            try:
                next(gen)
                still.append((i, gen))
            except StopIteration as done:
                results[i] = done.value
        active = still
    return results


def _split2(x):
    hi = x.astype(BF16)
    return hi, (x - hi.astype(F32)).astype(BF16)


def _mm_cat2(lhs_list, b, bd):
    rows = lhs_list[0].shape[0]
    width = b.shape[1]
    b_hi, b_lo = _split2(b)
    lhs = jnp.concatenate([part for a in lhs_list for part in _split2(a)], axis=0)
    prod = jnp.dot(lhs, jnp.concatenate([bd(b_hi), bd(b_lo)], axis=1), preferred_element_type=F32)
    outs = []
    for i in range(len(lhs_list)):
        hi = prod[2 * i * rows:(2 * i + 1) * rows]
        lo = prod[(2 * i + 1) * rows:(2 * i + 2) * rows]
        outs.append((lo[:, 0:width] + hi[:, width:]) + hi[:, 0:width])
    return outs


def _unit_lower_inverse(n, eye_cat, level_masks, bd):
    m = jnp.where(level_masks[0], n, 0.0)
    inv = eye_cat + m
    (m,) = _mm_cat2([m], m, bd)
    yield
    for step in range(INV_BLOCK.bit_length() - 2):
        if step + 1 < INV_BLOCK.bit_length() - 2:
            delta, m = _mm_cat2([inv, m], m, bd)
        else:
            (delta,) = _mm_cat2([inv], m, bd)
        yield
        inv = inv + delta
    inv = inv.astype(BF16)
    for mask in level_masks[1:]:
        off = jnp.where(mask, n, 0.0).astype(BF16)
        g = jnp.dot(inv, bd(off), preferred_element_type=F32)
        yield
        inv = inv + jnp.dot(g.astype(BF16), bd(inv), preferred_element_type=F32).astype(BF16)
        yield
    return inv


def _rwkv_consts():
    w = A_WIDTH
    bd_mask = (_iota((w, w), 0) // HEAD_DIM) == (_iota((w, w), 1) // HEAD_DIM)
    t = _iota((RWKV_CHUNK, w), 0)
    s = _iota((RWKV_CHUNK, w), 1) % RWKV_CHUNK
    level_masks = [t // INV_BLOCK == s // INV_BLOCK]
    size = INV_BLOCK
    while size < RWKV_CHUNK:
        level_masks.append((t // (2 * size) == s // (2 * size)) & (t // size != s // size))
        size *= 2
    return bd_mask, s < t, s <= t, (s == t).astype(F32), level_masks


def _load_params(refs):
    return {name: ref[...] for name, ref in zip(_RWKV_PARAMS, refs)}


def _rwkv_prompt_kernel(pa_ref, *rest):
    prm_refs = rest[:len(_RWKV_PARAMS)]
    ya_ref, shift_ref, wkv_ref, carry_ref, state_ref = rest[len(_RWKV_PARAMS):]
    i = pl.program_id(0)
    nseq, tb, _ = pa_ref.shape

    @pl.when(i == 0)
    def _():
        carry_ref[...] = jnp.zeros_like(carry_ref)
        state_ref[...] = jnp.zeros_like(state_ref)

    p = _load_params(prm_refs)
    cst = _rwkv_consts()
    n_chunks = tb // RWKV_CHUNK
    rows_all = nseq * tb
    pa = jnp.concatenate([pa_ref[n] for n in range(nseq)], axis=0)
    row = _iota(pa.shape, 0)
    prev = pltpu.roll(pa, 1, axis=0)
    for n in range(nseq):
        prev = jnp.where(row == n * tb, carry_ref[n], prev)
        last = pa_ref[n, tb - 1:tb, :]
        carry_ref[n] = last
        shift_ref[n] = last
    r, k, v, logd, av, bv, g = _rwkv_prep(pa + (prev - pa) * p["mu"], p)
    span = RWKV_GROUP * RWKV_CHUNK if rows_all % (RWKV_GROUP * RWKV_CHUNK) == 0 else RWKV_CHUNK
    ri = _iota((span, span), 0)
    ci = _iota((span, span), 1)
    chunk_tril = ((ri // RWKV_CHUNK == ci // RWKV_CHUNK) & (ri >= ci)).astype(F32)
    cum = jnp.concatenate([_mm_sel_left(chunk_tril, logd[s0:s0 + span]) for s0 in range(0, rows_all, span)],
                          axis=0)
    def setup(n, c):
        rows = slice(n * tb + c * RWKV_CHUNK, n * tb + (c + 1) * RWKV_CHUNK)
        return _rwkv_chunk_setup(r[rows], k[rows], v[rows], logd[rows], cum[rows], av[rows], bv[rows], cst)

    def apply_group(n, state, group_setups):
        ys = []
        for chunk in group_setups:
            y, state = yield from _rwkv_chunk_apply(chunk, state, cst[0])
            ys.append(y)
        return ys, state

    group = min(n_chunks, RWKV_GROUP)
    n_groups = n_chunks // group
    states = [state_ref[n] for n in range(nseq)]
    ys = [[] for _ in range(nseq)]
    pending = None
    for gi in range(n_groups + 1):
        jobs = []
        if pending is not None:
            jobs += [apply_group(n, states[n], pending[n]) for n in range(nseq)]
        if gi < n_groups:
            jobs += [setup(n, c) for n in range(nseq) for c in range(gi * group, (gi + 1) * group)]
        done = _interleave(jobs)
        if pending is not None:
            for n in range(nseq):
                ys_n, states[n] = done[n]
                ys[n] += ys_n
            done = done[nseq:]
        pending = [done[n * group:(n + 1) * group] for n in range(nseq)] if gi < n_groups else None
    y = jnp.concatenate([y_c for n in range(nseq) for y_c in ys[n]], axis=0)
    out = _rwkv_post(y, r, k, v, g, p)
    for n in range(nseq):
        state_ref[n] = states[n]
        ya_ref[n] = out[n * tb:(n + 1) * tb]

    @pl.when(i == pl.num_programs(0) - 1)
    def _():
        for n in range(nseq):
            for h in range(A_HEADS):
                sl = slice(h * HEAD_DIM, (h + 1) * HEAD_DIM)
                wkv_ref[n, h] = state_ref[n, sl, sl]


def _rwkv_param_specs(layer):
    shapes = {"mu": (1, A_PROJ), "w2": (LANES, A_WIDTH), "a2": (LANES, A_WIDTH), "g2": (A_GATE_LORA, A_WIDTH)}
    return [_layer_spec(layer, shapes.get(name, (1, A_WIDTH))) for name in _RWKV_PARAMS]


def _rwkv_prompt(proj, prm, layer, tb):
    nseq, t, _ = proj.shape
    return pl.pallas_call(
        _rwkv_prompt_kernel,
        grid=(t // tb,),
        in_specs=[pl.BlockSpec((nseq, tb, A_PROJ), lambda i: (0, i, 0))] + _rwkv_param_specs(layer),
        out_specs=[
            pl.BlockSpec((nseq, tb, A_WIDTH), lambda i: (0, i, 0)),
            pl.BlockSpec((nseq, 1, A_PROJ), lambda i: (0, 0, 0)),
            pl.BlockSpec((nseq, A_HEADS, HEAD_DIM, HEAD_DIM), lambda i: (0, 0, 0, 0)),
        ],
        out_shape=[
            jax.ShapeDtypeStruct((nseq, t, A_WIDTH), F32),
            jax.ShapeDtypeStruct((nseq, 1, A_PROJ), F32),
            jax.ShapeDtypeStruct((nseq, A_HEADS, HEAD_DIM, HEAD_DIM), F32),
        ],
        scratch_shapes=[pltpu.VMEM((nseq, 1, A_PROJ), F32), pltpu.VMEM((nseq, A_WIDTH, A_WIDTH), F32)],
        compiler_params=pltpu.CompilerParams(
            dimension_semantics=("arbitrary",), vmem_limit_bytes=VMEM_LIMIT_BYTES),
        name="rwkv_prompt",
    )(proj, *[prm[name] for name in _RWKV_PARAMS])


def _rwkv_decode_kernel(pa_ref, shift_ref, wkv0_ref, *rest):
    prm_refs = rest[:len(_RWKV_PARAMS)]
    _, ya_ref, wkv_ref, cols_ref, rows_ref, y_ref = rest[len(_RWKV_PARAMS):]
    h = pl.program_id(0)
    p = _load_params(prm_refs)

    @pl.when(h == 0)
    def _():
        pa = pa_ref[...]
        r, k, v, logd, av, bv, g = _rwkv_prep(pa + (shift_ref[...] - pa) * p["mu"], p)
        for idx, x in enumerate((r, k, v, jnp.exp(logd), av, bv)):
            cols_ref[idx] = x.T
        for idx, x in enumerate((r, k, v, g)):
            rows_ref[idx] = x

    hs = pl.ds(pl.multiple_of(h * HEAD_DIM, HEAD_DIM), HEAD_DIM)
    r_h, k_h, v_h, d_h, a_h, b_h = [cols_ref[idx, hs, :] for idx in range(6)]
    ys = []
    for vi in range(HEAD_DIM):
        s = wkv0_ref[vi]
        u = jnp.sum(s * a_h, axis=0, keepdims=True)
        s = s * d_h + u * b_h + v_h[vi:vi + 1, :] * k_h
        wkv_ref[vi] = s
        ys.append(jnp.sum(s * r_h, axis=0, keepdims=True))
    y_ref[hs, :] = jnp.concatenate(ys, axis=0)

    @pl.when(h == A_HEADS - 1)
    def _():
        ya_ref[...] = _rwkv_post(y_ref[...].T, rows_ref[0], rows_ref[1], rows_ref[2], rows_ref[3], p)


def _rwkv_decode(proj, shift_prev, wkv0_t, prm, layer, wkv_all_t):
    nd = proj.shape[0]
    state_spec = pl.BlockSpec((None, None, HEAD_DIM, HEAD_DIM, nd), lambda h: (layer, h, 0, 0, 0))
    n_in = 3 + len(_RWKV_PARAMS)
    return pl.pallas_call(
        _rwkv_decode_kernel,
        grid=(A_HEADS,),
        in_specs=[
            pl.BlockSpec((nd, A_PROJ), lambda h: (0, 0)),
            pl.BlockSpec((None, nd, A_PROJ), lambda h: (layer, 0, 0)),
            state_spec,
        ] + _rwkv_param_specs(layer) + [pl.BlockSpec(memory_space=pl.ANY)],
        out_specs=[pl.BlockSpec((nd, A_WIDTH), lambda h: (0, 0)), state_spec],
        out_shape=[jax.ShapeDtypeStruct((nd, A_WIDTH), F32), jax.ShapeDtypeStruct(wkv_all_t.shape, F32)],
        input_output_aliases={n_in: 1},
        scratch_shapes=[pltpu.VMEM((6, A_WIDTH, nd), F32), pltpu.VMEM((4, nd, A_WIDTH), F32),
                        pltpu.VMEM((A_WIDTH, nd), F32)],
        compiler_params=pltpu.CompilerParams(
            dimension_semantics=("arbitrary",), vmem_limit_bytes=VMEM_LIMIT_BYTES),
        name="rwkv_decode",
    )(proj, shift_prev, wkv0_t, *[prm[name] for name in _RWKV_PARAMS], wkv_all_t)


def _place_q_heads(qn):
    lane_half = _iota((qn.shape[0], B_KV_WIDTH), 1) // HEAD_DIM
    placed = []
    for h in range(B_HEADS):
        kh, g = divmod(h, B_GROUP)
        chunk = qn[:, kh * B_KV_WIDTH:(kh + 1) * B_KV_WIDTH]
        if g != kh:
            chunk = pltpu.roll(chunk, HEAD_DIM, axis=1)
        placed.append(jnp.where(lane_half == kh, chunk, 0.0))
    return placed


def _gather_o_heads(o_heads):
    lane_half = _iota(o_heads[0].shape, 1) // HEAD_DIM
    chunks = []
    for kh in range(B_KV_HEADS):
        parts = []
        for g in range(B_GROUP):
            o = o_heads[kh * B_GROUP + g]
            parts.append(o if g == kh else pltpu.roll(o, HEAD_DIM, axis=1))
        chunks.append(jnp.where(lane_half == 0, parts[0], parts[1]))
    return jnp.concatenate(chunks, axis=1)


def _swa_prompt_kernel(layer, sinks_ref, pb_ref, gq_ref, gk_ref, yb_ref, nk_ref, nv_ref, kprev_ref, vprev_ref):
    j = pl.program_id(1)
    tq = pb_ref.shape[0]
    blk = SWA_BLOCK

    @pl.when(j == 0)
    def _():
        kprev_ref[...] = jnp.zeros_like(kprev_ref)
        vprev_ref[...] = jnp.zeros_like(vprev_ref)

    scale = HEAD_DIM ** -0.5
    rows = _iota((B_GROUP * blk, 2 * blk), 0)
    cols = _iota((B_GROUP * blk, 2 * blk), 1)
    qi = rows % blk
    band = (cols >= qi) & (cols <= qi + WINDOW)
    n_sub = tq // blk
    x = pb_ref[...]
    qn = _head_rms_norm(x[:, 0:B_WIDTH], gq_ref[...])
    kn = _head_rms_norm(x[:, B_WIDTH:B_WIDTH + B_KV_WIDTH], gk_ref[...])
    v = x[:, B_WIDTH + B_KV_WIDTH:B_PROJ]
    kfull = jnp.concatenate([kprev_ref[...], kn], axis=0)
    vfull = jnp.concatenate([vprev_ref[...], v], axis=0)
    placed = _place_q_heads(qn)

    def attend(sb, kh):
        q2 = jnp.concatenate([pl_h[sb * blk:(sb + 1) * blk] for pl_h in placed[kh * B_GROUP:(kh + 1) * B_GROUP]],
                             axis=0)
        s = _mm_nt(q2, kfull[sb * blk:(sb + 2) * blk])
        yield
        first_key = jnp.where(j * n_sub + sb > 0, 0, blk)
        s = jnp.where(band & (cols >= first_key), s * scale, -jnp.inf)
        sink = jnp.where(rows[:, 0:1] < blk, sinks_ref[layer, kh * B_GROUP], sinks_ref[layer, kh * B_GROUP + 1])
        m = jnp.maximum(jnp.max(s, axis=-1, keepdims=True), sink)
        pr = jnp.exp(s - m)
        den = jnp.sum(pr, axis=-1, keepdims=True) + jnp.exp(sink - m)
        o = _mm(pr, vfull[sb * blk:(sb + 2) * blk])
        yield
        return o / den

    outs = _interleave([attend(sb, kh) for sb in range(n_sub) for kh in range(B_KV_HEADS)])
    for sb in range(n_sub):
        heads = [outs[sb * B_KV_HEADS + kh][g * blk:(g + 1) * blk]
                 for kh in range(B_KV_HEADS) for g in range(B_GROUP)]
        yb_ref[sb * blk:(sb + 1) * blk, :] = _gather_o_heads(heads)
    kprev_ref[...] = kn[tq - blk:tq]
    vprev_ref[...] = v[tq - blk:tq]
    nk_ref[...] = kn[tq - blk:tq]
    nv_ref[...] = v[tq - blk:tq]


def _swa_prompt(proj, gq, gk, sinks, layer, tq):
    nseq, t, _ = proj.shape
    col = A_PROJ // B_PROJ
    return pl.pallas_call(
        functools.partial(_swa_prompt_kernel, layer),
        grid=(nseq, t // tq),
        in_specs=[
            pl.BlockSpec(memory_space=pltpu.SMEM),
            pl.BlockSpec((None, tq, B_PROJ), lambda n, j: (n, j, col)),
            _layer_spec(layer, (1, B_WIDTH)),
            _layer_spec(layer, (1, B_KV_WIDTH)),
        ],
        out_specs=[
            pl.BlockSpec((None, tq, B_WIDTH), lambda n, j: (n, j, 0)),
            pl.BlockSpec((None, WINDOW, B_KV_WIDTH), lambda n, j: (n, 0, 0)),
            pl.BlockSpec((None, WINDOW, B_KV_WIDTH), lambda n, j: (n, 0, 0)),
        ],
        out_shape=[
            jax.ShapeDtypeStruct((nseq, t, B_WIDTH), F32),
            jax.ShapeDtypeStruct((nseq, WINDOW, B_KV_WIDTH), F32),
            jax.ShapeDtypeStruct((nseq, WINDOW, B_KV_WIDTH), F32),
        ],
        scratch_shapes=[pltpu.VMEM((SWA_BLOCK, B_KV_WIDTH), F32), pltpu.VMEM((SWA_BLOCK, B_KV_WIDTH), F32)],
        compiler_params=pltpu.CompilerParams(
            dimension_semantics=("arbitrary", "arbitrary"), vmem_limit_bytes=VMEM_LIMIT_BYTES),
        name="swa_prompt",
    )(sinks, proj, gq, gk)


def _swa_decode_kernel(layer, sinks_ref, pb_ref, ck_ref, cv_ref, gq_ref, gk_ref, _k_all, _v_all,
                       yb_ref, nk_ref, nv_ref):
    x = pb_ref[...]
    nb = x.shape[0]
    qn = _head_rms_norm(x[:, 0:B_WIDTH], gq_ref[...])
    kn = _head_rms_norm(x[:, B_WIDTH:B_WIDTH + B_KV_WIDTH], gk_ref[...])
    v = x[:, B_WIDTH + B_KV_WIDTH:B_PROJ]
    placed = _place_q_heads(qn)
    scale = HEAD_DIM ** -0.5
    head = _iota((B_HEADS, 1), 0)
    sink = jnp.zeros((B_HEADS, 1), F32)
    for h in range(B_HEADS):
        sink = jnp.where(head == h, sinks_ref[layer, h], sink)
    last_key = _iota((B_KV_WIDTH, WINDOW), 1) == WINDOW - 1
    k_cols = _column_broadcasts(kn, WINDOW)
    v_cols = _column_broadcasts(v, WINDOW)

    def attend(j):
        kc = ck_ref[j]
        vc = cv_ref[j]
        group = slice(j * WINDOW, (j + 1) * WINDOW)
        q4 = jnp.concatenate([pl_h[j:j + 1] for pl_h in placed], axis=0)
        s = _mm(q4, kc) * scale
        nk_ref[j] = jnp.where(last_key, k_cols[:, group], pltpu.roll(kc, WINDOW - 1, axis=1))
        nv_ref[j] = jnp.where(last_key, v_cols[:, group], pltpu.roll(vc, WINDOW - 1, axis=1))
        yield
        s_new = jnp.sum(q4 * kn[j:j + 1], axis=-1, keepdims=True) * scale
        m = jnp.maximum(jnp.maximum(jnp.max(s, axis=-1, keepdims=True), s_new), sink)
        pr = jnp.exp(s - m)
        p_new = jnp.exp(s_new - m)
        den = jnp.sum(pr, axis=-1, keepdims=True) + p_new + jnp.exp(sink - m)
        o = _mm_nt(pr, vc)
        yield
        return (o + p_new * v[j:j + 1]) / den

    o_rows = _interleave([attend(j) for j in range(nb)])
    outs = [jnp.concatenate([o[h:h + 1] for o in o_rows], axis=0) for h in range(B_HEADS)]
    yb_ref[...] = _gather_o_heads(outs)


def _swa_decode(proj, cache_k, cache_v, gq, gk, sinks, layer, k_all, v_all):
    nd = proj.shape[0]
    nb = DEC_BLOCK
    col = A_PROJ // B_PROJ
    cache_spec = pl.BlockSpec((None, nb, WINDOW, B_KV_WIDTH), lambda i: (layer, i, 0, 0))
    any_spec = pl.BlockSpec(memory_space=pl.ANY)
    return pl.pallas_call(
        functools.partial(_swa_decode_kernel, layer),
        grid=(nd // nb,),
        in_specs=[
            pl.BlockSpec(memory_space=pltpu.SMEM),
            pl.BlockSpec((nb, B_PROJ), lambda i: (i, col)),
            cache_spec, cache_spec,
            _layer_spec(layer, (1, B_WIDTH)),
            _layer_spec(layer, (1, B_KV_WIDTH)),
            any_spec, any_spec,
        ],
        out_specs=[pl.BlockSpec((nb, B_WIDTH), lambda i: (i, 0)), cache_spec, cache_spec],
        out_shape=[
            jax.ShapeDtypeStruct((nd, B_WIDTH), F32),
            jax.ShapeDtypeStruct(k_all.shape, F32),
            jax.ShapeDtypeStruct(v_all.shape, F32),
        ],
        input_output_aliases={6: 1, 7: 2},
        compiler_params=pltpu.CompilerParams(
            dimension_semantics=("arbitrary",), vmem_limit_bytes=VMEM_LIMIT_BYTES),
        name="swa_decode",
    )(sinks, proj, cache_k, cache_v, gq, gk, k_all, v_all)


def _head_expand(width_per_head):
    n = C_HEADS * width_per_head
    return (_iota((LANES, n), 0) == _iota((LANES, n), 1) // width_per_head).astype(F32)


def _ssd_prompt_kernel(z_ref, xbc_ref, dt_ref, cw_ref, cb_ref, dtb_ref, aneg_ref, dskip_ref, ng_ref,
                       yc_ref, tail_ref, hout_ref, xpad_ref, h_ref):
    j = pl.program_id(1)
    tb = xbc_ref.shape[0]

    @pl.when(j == 0)
    def _():
        xpad_ref[0:SUBLANES, :] = jnp.zeros((SUBLANES, CONV_DIM), F32)
        h_ref[...] = jnp.zeros_like(h_ref)

    xpad_ref[SUBLANES:SUBLANES + tb, :] = xbc_ref[...]
    cw = cw_ref[...]
    conv = cb_ref[...]
    for k in range(SSM_CONV):
        conv = conv + cw[SSM_CONV - 1 - k:SSM_CONV - k] * xpad_ref[SUBLANES - k:SUBLANES - k + tb, :]
    last_rows = xbc_ref[tb - SUBLANES:tb, :]
    tail_ref[...] = last_rows
    xpad_ref[0:SUBLANES, :] = last_rows
    act = _silu(conv)
    x = act[:, 0:C_WIDTH]
    bm = act[:, C_WIDTH:C_WIDTH + C_GROUPS * D_STATE]
    cm = act[:, C_WIDTH + C_GROUPS * D_STATE:CONV_DIM]
    dt = _softplus(dt_ref[...] + dtb_ref[...])
    xdt = x * _mm_sel_right(dt, _head_expand(HEAD_DIM))

    ch = SSD_CHUNK
    n_chunks = tb // ch
    ri = _iota((tb, tb), 0)
    ci = _iota((tb, tb), 1)
    chunk_tril = ((ri // ch == ci // ch) & (ri >= ci)).astype(F32)
    acum = _mm_sel_left(chunk_tril, dt * aneg_ref[...])
    acum_x = _mm_sel_right(acum, _head_expand(D_STATE))
    causal = _iota((ch, ch), 0) >= _iota((ch, ch), 1)
    lane_lo = _iota((ch, 2 * HEAD_DIM), 1) < HEAD_DIM
    row_lo = _iota((2 * HEAD_DIM, D_STATE), 0) < HEAD_DIM
    heads_per_group = C_HEADS // C_GROUPS
    n_pairs = C_HEADS // 2
    head_lanes = lambda e: slice(e * D_STATE, (e + 1) * D_STATE)
    pair_lanes = lambda pr: slice(pr * 2 * HEAD_DIM, (pr + 1) * 2 * HEAD_DIM)
    group_lanes = lambda pr: slice(((2 * pr) // heads_per_group) * D_STATE, ((2 * pr) // heads_per_group + 1) * D_STATE)

    def chunk_local(c):
        rows = slice(c * ch, (c + 1) * ch)
        acx = acum_x[rows]
        ac_t = acum[rows].T
        last = acx[ch - 1:ch]
        decay_to_end = jnp.exp(last - acx)
        cbs = [_mm_nt(cm[rows, g * D_STATE:(g + 1) * D_STATE], bm[rows, g * D_STATE:(g + 1) * D_STATE])
               for g in range(C_GROUPS)]
        yield
        y_diag, states = [], []
        for pr in range(n_pairs):
            e0, e1 = 2 * pr, 2 * pr + 1
            xp = xdt[rows, pair_lanes(pr)]
            yd = []
            for e in (e0, e1):
                seg = acx[:, head_lanes(e)] - ac_t[e:e + 1, :]
                lmat = jnp.exp(jnp.where(causal, seg, -jnp.inf))
                yd.append(_mm(cbs[(2 * pr) // heads_per_group] * lmat, xp))
            y_diag.append(jnp.where(lane_lo, yd[0], yd[1]))
            scale = jnp.where(lane_lo, decay_to_end[:, head_lanes(e0)], decay_to_end[:, head_lanes(e1)])
            states.append(_mm_tn(xp * scale, bm[rows, group_lanes(pr)]))
        yield
        chunk_decay = jnp.exp(last)
        cds = [jnp.where(row_lo, chunk_decay[:, head_lanes(2 * pr)], chunk_decay[:, head_lanes(2 * pr + 1)])
               for pr in range(n_pairs)]
        return y_diag, states, cds, jnp.exp(acx)

    local = _interleave([chunk_local(c) for c in range(n_chunks)])
    h = [h_ref[pair_lanes(pr), :] for pr in range(n_pairs)]
    y_chunks = []
    for c in range(n_chunks):
        rows = slice(c * ch, (c + 1) * ch)
        y_diag, states, cds, ea = local[c]
        pairs = []
        for pr in range(n_pairs):
            y_off = _mm_nt(cm[rows, group_lanes(pr)], h[pr])
            pairs.append(y_diag[pr] + y_off * jnp.where(lane_lo, ea[:, head_lanes(2 * pr)], ea[:, head_lanes(2 * pr + 1)]))
            h[pr] = h[pr] * cds[pr] + states[pr]
        y_chunks.append(jnp.concatenate(pairs, axis=1))
    for pr in range(n_pairs):
        h_ref[pair_lanes(pr), :] = h[pr]
        hout_ref[pair_lanes(pr), :] = h[pr]

    y = (jnp.concatenate(y_chunks, axis=0) + dskip_ref[...] * x) * _silu(z_ref[...])
    yc_ref[...] = _rms_norm(y, ng_ref[...])


def _ssd_prompt(proj, prm, layer, tb):
    nseq, t, _ = proj.shape
    return pl.pallas_call(
        _ssd_prompt_kernel,
        grid=(nseq, t // tb),
        in_specs=[
            pl.BlockSpec((None, tb, C_WIDTH), lambda n, j: (n, j, Z_OFF // C_WIDTH)),
            pl.BlockSpec((None, tb, CONV_DIM), lambda n, j: (n, j, XBC_OFF // CONV_DIM)),
            pl.BlockSpec((None, tb, LANES), lambda n, j: (n, j, DT_OFF // LANES)),
            _layer_spec(layer, (SSM_CONV, CONV_DIM)),
            _layer_spec(layer, (1, CONV_DIM)),
            _layer_spec(layer, (1, LANES)),
            _layer_spec(layer, (1, LANES)),
            _layer_spec(layer, (1, C_WIDTH)),
            _layer_spec(layer, (1, C_WIDTH)),
        ],
        out_specs=[
            pl.BlockSpec((None, tb, C_WIDTH), lambda n, j: (n, j, 0)),
            pl.BlockSpec((None, SUBLANES, CONV_DIM), lambda n, j: (n, 0, 0)),
            pl.BlockSpec((None, C_WIDTH, D_STATE), lambda n, j: (n, 0, 0)),
        ],
        out_shape=[
            jax.ShapeDtypeStruct((nseq, t, C_WIDTH), F32),
            jax.ShapeDtypeStruct((nseq, SUBLANES, CONV_DIM), F32),
            jax.ShapeDtypeStruct((nseq, C_WIDTH, D_STATE), F32),
        ],
        scratch_shapes=[
            pltpu.VMEM((SUBLANES + tb, CONV_DIM), F32),
            pltpu.VMEM((C_WIDTH, D_STATE), F32),
        ],
        compiler_params=pltpu.CompilerParams(
            dimension_semantics=("arbitrary", "arbitrary"), vmem_limit_bytes=VMEM_LIMIT_BYTES),
        name="ssd_prompt",
    )(proj, proj, proj, prm["conv_w"], prm["conv_b"], prm["dt_bias"], prm["a_neg"], prm["d_skip"], prm["norm_g"])


def _ssd_decode_kernel(z_ref, xbc_ref, dt_ref, buf_ref, h0_ref, cw_ref, cb_ref, dtb_ref, aneg_ref, dskip_ref, ng_ref,
                       _h_all, yc_ref, nbuf_ref, hout_ref):
    xbc = xbc_ref[...]
    nb = xbc.shape[0]
    cw = cw_ref[...]
    conv = cb_ref[...] + cw[SSM_CONV - 1:SSM_CONV] * xbc
    for k in range(SSM_CONV - 1):
        conv = conv + cw[k:k + 1] * buf_ref[k]
        nbuf_ref[k] = buf_ref[k + 1] if k + 1 < SSM_CONV - 1 else xbc
    act = _silu(conv)
    x = act[:, 0:C_WIDTH]
    bm = act[:, C_WIDTH:C_WIDTH + C_GROUPS * D_STATE]
    cm = act[:, C_WIDTH + C_GROUPS * D_STATE:CONV_DIM]
    dt = _softplus(dt_ref[...] + dtb_ref[...])
    decay = _mm_sel_right(jnp.exp(dt * aneg_ref[...]), _head_expand(D_STATE))
    xdt = x * _mm_sel_right(dt, _head_expand(HEAD_DIM))
    heads_per_group = C_HEADS // C_GROUPS
    x_cols = _column_broadcasts(xdt, D_STATE)
    hc = []
    for j in range(nb):
        for e in range(C_HEADS):
            gs = slice((e // heads_per_group) * D_STATE, (e // heads_per_group + 1) * D_STATE)
            hs = slice(e * HEAD_DIM, (e + 1) * HEAD_DIM)
            h = (h0_ref[j, hs, :] * decay[j:j + 1, e * D_STATE:(e + 1) * D_STATE]
                 + x_cols[hs, j * D_STATE:(j + 1) * D_STATE] * bm[j:j + 1, gs])
            hout_ref[j, hs, :] = h
            hc.append(h * cm[j:j + 1, gs])
    y_cols = _row_sums(hc, D_STATE)
    lane = _iota((C_WIDTH, nb), 1)
    y_t = jnp.zeros((C_WIDTH, nb), F32)
    for j in range(nb):
        y_t = jnp.where(lane == j, y_cols[j * C_WIDTH:(j + 1) * C_WIDTH, 0:nb], y_t)
    y = _mm_nt_sel_left(_eye(nb), y_t)
    y = (y + dskip_ref[...] * x) * _silu(z_ref[...])
    yc_ref[...] = _rms_norm(y, ng_ref[...])


def _ssd_decode(proj, conv_buf_t, h0, prm, layer, h_all):
    nd = proj.shape[0]
    nb = DEC_BLOCK
    state_spec = pl.BlockSpec((None, nb, C_WIDTH, D_STATE), lambda i: (layer, i, 0, 0))
    return pl.pallas_call(
        _ssd_decode_kernel,
        grid=(nd // nb,),
        in_specs=[
            pl.BlockSpec((nb, C_WIDTH), lambda i: (i, Z_OFF // C_WIDTH)),
            pl.BlockSpec((nb, CONV_DIM), lambda i: (i, XBC_OFF // CONV_DIM)),
            pl.BlockSpec((nb, LANES), lambda i: (i, DT_OFF // LANES)),
            pl.BlockSpec((None, SSM_CONV - 1, nb, CONV_DIM), lambda i: (layer, 0, i, 0)),
            state_spec,
            _layer_spec(layer, (SSM_CONV, CONV_DIM)),
            _layer_spec(layer, (1, CONV_DIM)),
            _layer_spec(layer, (1, LANES)),
            _layer_spec(layer, (1, LANES)),
            _layer_spec(layer, (1, C_WIDTH)),
            _layer_spec(layer, (1, C_WIDTH)),
            pl.BlockSpec(memory_space=pl.ANY),
        ],
        out_specs=[
            pl.BlockSpec((nb, C_WIDTH), lambda i: (i, 0)),
            pl.BlockSpec((SSM_CONV - 1, nb, CONV_DIM), lambda i: (0, i, 0)),
            state_spec,
        ],
        out_shape=[
            jax.ShapeDtypeStruct((nd, C_WIDTH), F32),
            jax.ShapeDtypeStruct((SSM_CONV - 1, nd, CONV_DIM), F32),
            jax.ShapeDtypeStruct(h_all.shape, F32),
        ],
        input_output_aliases={11: 2},
        compiler_params=pltpu.CompilerParams(
            dimension_semantics=("arbitrary",), vmem_limit_bytes=VMEM_LIMIT_BYTES),
        name="ssd_decode",
    )(proj, proj, proj, conv_buf_t, h0, prm["conv_w"], prm["conv_b"], prm["dt_bias"], prm["a_neg"],
      prm["d_skip"], prm["norm_g"], h_all)


def _out_ffn_core(x, mix, g1, sh2, sc2, g2, gn, wo_ref, wu_ref, cw_ref, cb_ref, wd_ref, conv_fn):
    x1 = x + g1 * _mm(mix, wo_ref[...])
    h = (_rms_norm(x1, gn) * (1.0 + sc2) + sh2).astype(BF16)
    acc = jnp.zeros(x.shape, F32)
    for c in range(D_FF // FF_CHUNK):
        cs = slice(c * FF_CHUNK, (c + 1) * FF_CHUNK)
        gate = jnp.dot(h, wu_ref[:, cs], preferred_element_type=F32)
        val = jnp.dot(h, wu_ref[:, D_FF + c * FF_CHUNK:D_FF + (c + 1) * FF_CHUNK], preferred_element_type=F32)
        gate = conv_fn(gate, cs, cw_ref[:, cs], cb_ref[:, cs])
        acc = acc + _mm(_silu(gate) * val, wd_ref[cs, :])
    return x1 + g2 * acc


def _out_ffn_prompt_kernel(tiles_per_seq, x_ref, ya_ref, yb_ref, yc_ref, g1_ref, sh2_ref, sc2_ref, g2_ref, gn_ref,
                           wo_ref, wu_ref, cw_ref, cb_ref, wd_ref, o_ref, tail_ref, carry_ref):
    i = pl.program_id(0)
    tm = x_ref.shape[0]

    @pl.when(i % tiles_per_seq == 0)
    def _():
        carry_ref[...] = jnp.zeros_like(carry_ref)

    def conv_fn(gate, cs, cw, cb):
        tail = carry_ref[:, cs]
        out = cb + cw[FFN_CONV - 1:FFN_CONV] * gate
        for k in range(1, FFN_CONV):
            out = out + cw[FFN_CONV - 1 - k:FFN_CONV - k] * _shift_rows(gate, k, tail)
        last_rows = gate[tm - SUBLANES:tm, :]
        carry_ref[:, cs] = last_rows
        tail_ref[:, cs] = last_rows
        return out

    mix = jnp.concatenate([ya_ref[...], yb_ref[...], yc_ref[...]], axis=1)
    o_ref[...] = _out_ffn_core(x_ref[...], mix, g1_ref[...], sh2_ref[...], sc2_ref[...], g2_ref[...], gn_ref[...],
                               wo_ref, wu_ref, cw_ref, cb_ref, wd_ref, conv_fn)


def _out_ffn_decode_kernel(x_ref, ya_ref, yb_ref, yc_ref, g1_ref, sh2_ref, sc2_ref, g2_ref, gn_ref,
                           wo_ref, wu_ref, cw_ref, cb_ref, wd_ref, buf_ref, o_ref, nbuf_ref):
    def conv_fn(gate, cs, cw, cb):
        out = cb + cw[FFN_CONV - 1:FFN_CONV] * gate
        for k in range(FFN_CONV - 1):
            out = out + cw[k:k + 1] * buf_ref[k, :, cs]
            nbuf_ref[k, :, cs] = buf_ref[k + 1, :, cs] if k + 1 < FFN_CONV - 1 else gate
        return out

    mix = jnp.concatenate([ya_ref[...], yb_ref[...], yc_ref[...]], axis=1)
    o_ref[...] = _out_ffn_core(x_ref[...], mix, g1_ref[...], sh2_ref[...], sc2_ref[...], g2_ref[...], gn_ref[...],
                               wo_ref, wu_ref, cw_ref, cb_ref, wd_ref, conv_fn)


def _out_ffn(x, ya, yb, yc, mod, prm, layer, tm, per_row, tiles_per_seq, conv_buf_t=None):
    rows = x.shape[0]
    nseq = rows // (tm * tiles_per_seq) if not per_row else None
    row_spec = lambda w: pl.BlockSpec((tm, w), lambda i: (i, 0))
    in_specs = [row_spec(D_MODEL), row_spec(A_WIDTH), row_spec(B_WIDTH), row_spec(C_WIDTH)]
    in_specs += [_mod_spec(layer, per_row, tiles_per_seq, c) for c in (2, 3, 4, 5)]
    in_specs += [
        _layer_spec(layer, (1, D_MODEL)),
        _layer_spec(layer, (D_MODEL, D_MODEL), single_buffer=True),
        _layer_spec(layer, (D_MODEL, 2 * D_FF), single_buffer=True),
        _layer_spec(layer, (FFN_CONV, D_FF)),
        _layer_spec(layer, (1, D_FF)),
        _layer_spec(layer, (D_FF, D_MODEL), single_buffer=True),
    ]
    args = [x, ya, yb, yc, mod, mod, mod, mod, prm["norm_ffn_g"], prm["w_out"], prm["w_up"], prm["ffn_conv_w"],
            prm["ffn_conv_b"], prm["w_down"]]
    params = pltpu.CompilerParams(dimension_semantics=("arbitrary",), vmem_limit_bytes=VMEM_LIMIT_BYTES)
    if per_row:
        buf_shape = (FFN_CONV - 1, rows, D_FF)
        return pl.pallas_call(
            _out_ffn_decode_kernel,
            grid=(rows // tm,),
            in_specs=in_specs + [pl.BlockSpec((None,) + buf_shape, lambda i: (layer, 0, i, 0))],
            out_specs=[row_spec(D_MODEL), pl.BlockSpec(buf_shape, lambda i: (0, i, 0))],
            out_shape=[jax.ShapeDtypeStruct((rows, D_MODEL), F32), jax.ShapeDtypeStruct(buf_shape, F32)],
            compiler_params=params,
            name="out_ffn_decode",
        )(*args, conv_buf_t)
    return pl.pallas_call(
        functools.partial(_out_ffn_prompt_kernel, tiles_per_seq),
        grid=(rows // tm,),
        in_specs=in_specs,
        out_specs=[row_spec(D_MODEL), pl.BlockSpec((None, SUBLANES, D_FF), lambda i: (i // tiles_per_seq, 0, 0))],
        out_shape=[jax.ShapeDtypeStruct((rows, D_MODEL), F32), jax.ShapeDtypeStruct((nseq, SUBLANES, D_FF), F32)],
        scratch_shapes=[pltpu.VMEM((SUBLANES, D_FF), F32)],
        compiler_params=params,
        name="out_ffn_prompt",
    )(*args)


def _row_param(p, width=None):
    if width is not None and p.shape[-1] != width:
        p = jnp.pad(p, ((0, 0), (0, width - p.shape[-1])))
    return p.reshape(p.shape[0], 1, p.shape[-1])


def _largest_tile(total, cap, quantum):
    tile = min(total, cap)
    while total % tile or tile % quantum:
        tile -= quantum
    return tile


def kernel(x_prompt, x_sample, c_prompt, c_sample, state_rwkv_shift, state_rwkv_wkv, cache_swa_k, cache_swa_v, state_ssm_conv, state_ssm, state_ffn_conv, ada_w, ada_b, norm_mix_g, norm_ffn_g, w_in, w_out, rwkv_mu, rwkv_w0, rwkv_w2, rwkv_a0, rwkv_a2, rwkv_g2, rwkv_k_k, rwkv_k_a, rwkv_r_k, rwkv_ln_g, rwkv_ln_b, attn_q_norm_g, attn_k_norm_g, attn_sinks, ssm_conv_w, ssm_conv_b, ssm_dt_bias, ssm_a_log, ssm_d, ssm_norm_g, ffn_w_up, ffn_conv_w, ffn_conv_b, ffn_w_down):
    depth = w_in.shape[0]
    nseq, t, _ = x_prompt.shape
    nd = x_sample.shape[0]
    assert x_sample.shape[1] == 1 and nd % DEC_BLOCK == 0 and t % SSD_CHUNK == 0

    tm = _largest_tile(t, 512, SUBLANES)
    tiles_per_seq = t // tm
    tb_rwkv = _largest_tile(t, 2 * RWKV_GROUP * RWKV_CHUNK, RWKV_CHUNK)
    tq_swa = _largest_tile(t, 512, SWA_BLOCK)
    tb_ssd = _largest_tile(t, 512, SSD_CHUNK)

    w_in_p = jnp.pad(w_in, ((0, 0), (0, 0), (0, PROJ_PAD - PROJ))).astype(BF16)
    rwkv_prm = {
        "mu": _row_param(rwkv_mu), "w0": _row_param(rwkv_w0), "a0": _row_param(rwkv_a0),
        "w2": jnp.pad(rwkv_w2, ((0, 0), (0, LANES - A_DECAY_LORA), (0, 0))),
        "a2": jnp.pad(rwkv_a2, ((0, 0), (A_DECAY_LORA, 0), (0, 0))),
        "g2": rwkv_g2,
        "k_k": _row_param(rwkv_k_k), "k_a": _row_param(rwkv_k_a),
        "r_k": _row_param(rwkv_r_k.reshape(depth, A_WIDTH)),
        "ln_g": _row_param(rwkv_ln_g), "ln_b": _row_param(rwkv_ln_b),
    }
    gq = _row_param(jnp.tile(attn_q_norm_g, (1, B_HEADS)))
    gk = _row_param(jnp.tile(attn_k_norm_g, (1, B_KV_HEADS)))
    ssd_prm = {
        "conv_w": ssm_conv_w, "conv_b": _row_param(ssm_conv_b),
        "dt_bias": _row_param(ssm_dt_bias, LANES),
        "a_neg": _row_param(-jnp.exp(ssm_a_log), LANES),
        "d_skip": _row_param(jnp.repeat(ssm_d, HEAD_DIM, axis=1)),
        "norm_g": _row_param(ssm_norm_g),
    }
    ffn_prm = {
        "norm_ffn_g": _row_param(norm_ffn_g), "w_out": w_out.astype(BF16), "w_up": ffn_w_up.astype(BF16),
        "ffn_conv_w": ffn_conv_w, "ffn_conv_b": _row_param(ffn_conv_b), "w_down": ffn_w_down.astype(BF16),
    }
    norm_mix = _row_param(norm_mix_g)

    c_all = jnp.concatenate([jnp.pad(c_prompt, ((0, SUBLANES - nseq), (0, 0))), c_sample], axis=0)
    mod = _ada(c_all, ada_w, ada_b)
    mod_p = mod[:, 0:nseq].reshape(depth, nseq, 1, N_MOD * D_MODEL)
    mod_s = mod[:, SUBLANES:]

    to_keys_minor = lambda c: jnp.transpose(c, (0, 1, 3, 4, 2)).reshape(depth, nd, B_KV_WIDTH, WINDOW)
    from_keys_minor = lambda c: jnp.transpose(c.reshape(depth, nd, B_KV_HEADS, HEAD_DIM, WINDOW), (0, 1, 4, 2, 3))
    cache_k = to_keys_minor(cache_swa_k)
    cache_v = to_keys_minor(cache_swa_v)
    wkv0_t = jnp.transpose(state_rwkv_wkv, (0, 2, 3, 4, 1))
    ssm_conv_t = jnp.swapaxes(state_ssm_conv, 1, 2)
    ssm_h0 = state_ssm.reshape(depth, nd, C_WIDTH, D_STATE)
    ffn_conv_t = jnp.swapaxes(state_ffn_conv, 1, 2)

    xp = x_prompt.reshape(nseq * t, D_MODEL)
    xs = x_sample.reshape(nd, D_MODEL)
    prompt_new = [[] for _ in range(7)]
    sample_new = [[] for _ in range(3)]
    s_wkv = jnp.zeros(wkv0_t.shape, F32)
    s_k = jnp.zeros(cache_k.shape, F32)
    s_v = jnp.zeros(cache_v.shape, F32)
    s_ssm = jnp.zeros(ssm_h0.shape, F32)
    for l in range(depth):
        proj = _in_proj(xp, mod_p, norm_mix, w_in_p, l, tm, 0, tiles_per_seq).reshape(nseq, t, PROJ_PAD)
        ya, p_shift, p_wkv = _rwkv_prompt(proj, rwkv_prm, l, tb_rwkv)
        yb, p_k, p_v = _swa_prompt(proj, gq, gk, attn_sinks, l, tq_swa)
        yc, p_conv, p_ssm = _ssd_prompt(proj, ssd_prm, l, tb_ssd)
        xp, p_ffn = _out_ffn(xp, ya.reshape(nseq * t, A_WIDTH), yb.reshape(nseq * t, B_WIDTH),
                             yc.reshape(nseq * t, C_WIDTH), mod_p, ffn_prm, l, tm, 0, tiles_per_seq)
        new_p = (p_shift.reshape(nseq, A_PROJ), p_wkv,
                 p_k.reshape(nseq, WINDOW, B_KV_HEADS, HEAD_DIM), p_v.reshape(nseq, WINDOW, B_KV_HEADS, HEAD_DIM),
                 p_conv[:, SUBLANES - (SSM_CONV - 1):], p_ssm.reshape(nseq, C_HEADS, HEAD_DIM, D_STATE),
                 p_ffn[:, SUBLANES - (FFN_CONV - 1):])
        proj_s = _in_proj(xs, mod_s, norm_mix, w_in_p, l, nd, nd, 1)
        ya, s_wkv = _rwkv_decode(proj_s, state_rwkv_shift, wkv0_t, rwkv_prm, l, s_wkv)
        yb, s_k, s_v = _swa_decode(proj_s, cache_k, cache_v, gq, gk, attn_sinks, l, s_k, s_v)
        yc, s_conv_t, s_ssm = _ssd_decode(proj_s, ssm_conv_t, ssm_h0, ssd_prm, l, s_ssm)
        xs, s_ffn_t = _out_ffn(xs, ya, yb, yc, mod_s, ffn_prm, l, nd, nd, 1, ffn_conv_t)
        new_s = (proj_s[:, 0:A_PROJ], jnp.swapaxes(s_conv_t, 0, 1), jnp.swapaxes(s_ffn_t, 0, 1))
        for lst, arr in zip(prompt_new, new_p):
            lst.append(arr)
        for lst, arr in zip(sample_new, new_s):
            lst.append(arr)
    outs_p = [jnp.stack(lst) for lst in prompt_new]
    s_shift, s_conv, s_ffn = [jnp.stack(lst) for lst in sample_new]
    outs_s = (s_shift, jnp.transpose(s_wkv, (0, 4, 1, 2, 3)), from_keys_minor(s_k), from_keys_minor(s_v), s_conv,
              s_ssm.reshape(depth, nd, C_HEADS, HEAD_DIM, D_STATE), s_ffn)
    return (xp.reshape(nseq, t, D_MODEL), xs.reshape(nd, 1, D_MODEL), *outs_p, *outs_s)
```

```python
import functools

import jax
import jax.numpy as jnp
from jax import lax
from jax.experimental import pallas as pl
from jax.experimental.pallas import tpu as pltpu

F32 = jnp.float32
BF16 = jnp.bfloat16

D_MODEL = 1024
HEAD_DIM = 64
A_WIDTH = 256
A_HEADS = A_WIDTH // HEAD_DIM
A_DECAY_LORA = 64
A_AAA_LORA = 64
A_GATE_LORA = 128
A_PROJ = 3 * A_WIDTH + A_DECAY_LORA + A_AAA_LORA + A_GATE_LORA
B_WIDTH = 256
B_HEADS = B_WIDTH // HEAD_DIM
B_KV_HEADS = 2
B_GROUP = B_HEADS // B_KV_HEADS
B_KV_WIDTH = B_KV_HEADS * HEAD_DIM
B_PROJ = B_WIDTH + 2 * B_KV_WIDTH
WINDOW = 128
C_WIDTH = 512
C_HEADS = C_WIDTH // HEAD_DIM
C_GROUPS = 2
D_STATE = 128
SSM_CONV = 4
CONV_DIM = C_WIDTH + 2 * C_GROUPS * D_STATE
C_PROJ = C_WIDTH + CONV_DIM + C_HEADS
PROJ = A_PROJ + B_PROJ + C_PROJ
D_FF = 2816
FFN_CONV = 3
NORM_EPS = 1e-6
GN_EPS = 64e-5
N_MOD = 6

LANES = 128
SUBLANES = 8
VMEM_LIMIT_BYTES = 56 * 1024 * 1024

DT_OFF = A_PROJ + B_PROJ + C_WIDTH + CONV_DIM
PROJ_PAD = DT_OFF + LANES
Z_OFF = A_PROJ + B_PROJ
XBC_OFF = Z_OFF + C_WIDTH

RWKV_CHUNK = 64
INV_BLOCK = 8
RWKV_GROUP = 4
SSD_CHUNK = 128
SWA_BLOCK = 128
DEC_BLOCK = 8
FF_CHUNK = D_FF // 2


def _mm(a, b):
    return jnp.dot(a.astype(BF16), b.astype(BF16), preferred_element_type=F32)


def _mm_nt(a, b):
    return lax.dot_general(a.astype(BF16), b.astype(BF16), (((1,), (1,)), ((), ())), preferred_element_type=F32)


def _mm_tn(a, b):
    return lax.dot_general(a.astype(BF16), b.astype(BF16), (((0,), (0,)), ((), ())), preferred_element_type=F32)


def _split3(x):
    hi = x.astype(BF16)
    r1 = x - hi.astype(F32)
    mid = r1.astype(BF16)
    lo = (r1 - mid.astype(F32)).astype(BF16)
    return hi, mid, lo


def _mm_sel_left(sel, x):
    s = sel.astype(BF16)
    return jnp.dot(jnp.concatenate([s, s, s], axis=1), jnp.concatenate(_split3(x), axis=0),
                   preferred_element_type=F32)


def _mm_sel_right(x, sel):
    s = sel.astype(BF16)
    return jnp.dot(jnp.concatenate(_split3(x), axis=1), jnp.concatenate([s, s, s], axis=0),
                   preferred_element_type=F32)


def _mm_nt_sel_left(sel, x):
    s = sel.astype(BF16)
    return lax.dot_general(jnp.concatenate([s, s, s], axis=1), jnp.concatenate(_split3(x), axis=1),
                           (((1,), (1,)), ((), ())), preferred_element_type=F32)


def _row_sums(tiles, width):
    ones = jnp.ones((width, width), BF16)
    half = len(tiles) // 2
    dot = lambda xs: jnp.dot(jnp.concatenate(xs, axis=0).astype(BF16), ones, preferred_element_type=F32)
    return jnp.concatenate([dot(tiles[:half]), dot(tiles[half:])], axis=0)


def _column_broadcasts(x, group):
    nb, w = x.shape
    parts = jnp.concatenate(_split3(x), axis=0)
    parts_t = lax.dot_general(_eye(w).astype(BF16), parts, (((1,), (1,)), ((), ())),
                              preferred_element_type=F32).astype(BF16)
    src = _iota((3 * nb, nb * group), 0) % nb
    dst = _iota((3 * nb, nb * group), 1) // group
    return jnp.dot(parts_t, (src == dst).astype(BF16), preferred_element_type=F32)


def _sigmoid(x):
    return 1.0 / (1.0 + jnp.exp(-x))


def _silu(x):
    return x * _sigmoid(x)


def _softplus(x):
    return jnp.maximum(x, 0.0) + jnp.log(1.0 + jnp.exp(-jnp.abs(x)))


def _iota(shape, dim):
    return lax.broadcasted_iota(jnp.int32, shape, dim)


def _eye(n):
    return (_iota((n, n), 0) == _iota((n, n), 1)).astype(F32)


def _tril_ones(n):
    return (_iota((n, n), 0) >= _iota((n, n), 1)).astype(F32)


def _group_sum(x, group):
    tiles = []
    for t0 in range(0, x.shape[-1], LANES):
        tile = x[:, t0:t0 + LANES]
        gid = _iota(tile.shape, 1) // group
        sums = [jnp.sum(jnp.where(gid == h, tile, 0.0), axis=-1, keepdims=True) for h in range(LANES // group)]
        out = sums[-1]
        for h in range(LANES // group - 1):
            out = jnp.where(gid == h, sums[h], out)
        tiles.append(jnp.broadcast_to(out, tile.shape))
    return jnp.concatenate(tiles, axis=1) if len(tiles) > 1 else tiles[0]


def _rms_norm(x, g):
    return x * lax.rsqrt(jnp.mean(x * x, axis=-1, keepdims=True) + NORM_EPS) * g


def _head_rms_norm(x, g):
    return x * lax.rsqrt(_group_sum(x * x, HEAD_DIM) * (1.0 / HEAD_DIM) + NORM_EPS) * g


def _shift_rows(x, k, tail):
    rolled = pltpu.roll(x, k, axis=0)
    row = _iota((SUBLANES, x.shape[1]), 0)
    top = jnp.where(row < k, pltpu.roll(tail, k, axis=0), rolled[0:SUBLANES])
    return jnp.concatenate([top, rolled[SUBLANES:]], axis=0)


def _ada_kernel(c_ref, w_ref, b_ref, o_ref):
    o_ref[...] = _mm(_silu(c_ref[...]), w_ref[...]) + b_ref[...]


def _ada(c_all, ada_w, ada_b):
    depth = ada_w.shape[0]
    rows = c_all.shape[0]
    width = N_MOD * D_MODEL
    tn = 1536
    return pl.pallas_call(
        _ada_kernel,
        grid=(depth, width // tn),
        in_specs=[
            pl.BlockSpec((rows, D_MODEL), lambda l, j: (0, 0)),
            pl.BlockSpec((None, D_MODEL, tn), lambda l, j: (l, 0, j)),
            pl.BlockSpec((None, 1, tn), lambda l, j: (l, 0, j)),
        ],
        out_specs=pl.BlockSpec((None, rows, tn), lambda l, j: (l, 0, j)),
        out_shape=jax.ShapeDtypeStruct((depth, rows, width), F32),
        compiler_params=pltpu.CompilerParams(
            dimension_semantics=("arbitrary", "arbitrary"), vmem_limit_bytes=VMEM_LIMIT_BYTES),
        name="ada_mod",
    )(c_all, ada_w, ada_b.reshape(depth, 1, width))


def _in_proj_kernel(x_ref, sh_ref, sc_ref, g_ref, w_ref, o_ref, w_bf_ref):
    @pl.when(pl.program_id(0) == 0)
    def _():
        w_bf_ref[...] = jnp.zeros_like(w_bf_ref)
        w_bf_ref[:, 0:PROJ] = w_ref[...].astype(BF16)

    h = _rms_norm(x_ref[...], g_ref[...]) * (1.0 + sc_ref[...]) + sh_ref[...]
    o_ref[...] = jnp.dot(h.astype(BF16), w_bf_ref[...], preferred_element_type=F32)


def _mod_spec(layer, per_row, rows_per_seq_tiles, col):
    if per_row:
        return pl.BlockSpec((None, per_row, D_MODEL), lambda i: (layer, 0, col))
    return pl.BlockSpec((None, None, 1, D_MODEL), lambda i: (layer, i // rows_per_seq_tiles, 0, col))


def _layer_spec(layer, tail, single_buffer=False):
    idx = lambda *_: (layer,) + (0,) * len(tail)
    if single_buffer:
        return pl.BlockSpec((None,) + tuple(tail), idx, pipeline_mode=pl.Buffered(1))
    return pl.BlockSpec((None,) + tuple(tail), idx)


def _in_proj(x, mod, norm_g, w_in, layer, tm, per_row, tiles_per_seq):
    rows = x.shape[0]
    return pl.pallas_call(
        _in_proj_kernel,
        grid=(rows // tm,),
        in_specs=[
            pl.BlockSpec((tm, D_MODEL), lambda i: (i, 0)),
            _mod_spec(layer, per_row, tiles_per_seq, 0),
            _mod_spec(layer, per_row, tiles_per_seq, 1),
            _layer_spec(layer, (1, D_MODEL)),
            _layer_spec(layer, (D_MODEL, PROJ), single_buffer=True),
        ],
        out_specs=pl.BlockSpec((tm, PROJ_PAD), lambda i: (i, 0)),
        out_shape=jax.ShapeDtypeStruct((rows, PROJ_PAD), F32),
        scratch_shapes=[pltpu.VMEM((D_MODEL, PROJ_PAD), BF16)],
        compiler_params=pltpu.CompilerParams(
            dimension_semantics=("arbitrary",), vmem_limit_bytes=VMEM_LIMIT_BYTES),
        name="in_proj",
    )(x, mod, mod, norm_g, w_in)


_RWKV_PARAMS = ("mu", "w0", "w2", "a0", "a2", "g2", "k_k", "k_a", "r_k", "ln_g", "ln_b")


def _rwkv_prep(xs, p):
    r = xs[:, 0:A_WIDTH]
    k = xs[:, A_WIDTH:2 * A_WIDTH]
    v = xs[:, 2 * A_WIDTH:3 * A_WIDTH]
    lora_in = xs[:, 3 * A_WIDTH:3 * A_WIDTH + LANES]
    lg = xs[:, 3 * A_WIDTH + LANES:A_PROJ]
    w_log = -_softplus(-(p["w0"] + _mm(jnp.tanh(lora_in), p["w2"]))) - 0.5
    log_decay = -jnp.exp(w_log)
    a = _sigmoid(p["a0"] + _mm(lora_in, p["a2"]))
    g = _mm(_sigmoid(lg), p["g2"])
    kk = k * p["k_k"]
    kk = kk / jnp.maximum(jnp.sqrt(_group_sum(kk * kk, HEAD_DIM)), 1e-12)
    k = k * (1.0 + (a - 1.0) * p["k_a"])
    return r, k, v, log_decay, -kk, kk * a, g


def _rwkv_post(y, r, k, v, g, p):
    mean = _group_sum(y, HEAD_DIM) * (1.0 / HEAD_DIM)
    c = y - mean
    var = _group_sum(c * c, HEAD_DIM) * (1.0 / HEAD_DIM)
    yn = c * lax.rsqrt(var + GN_EPS) * p["ln_g"] + p["ln_b"]
    bonus = _group_sum(r * k * p["r_k"], HEAD_DIM) * v
    return (yn + bonus) * g


def _block_diag(x, mask01):
    return jnp.concatenate([x] * (x.shape[1] // x.shape[0]), axis=0) * mask01


def _rwkv_chunk_setup(r, k, v, logd, cum, av, bv, cst):
    bd_mask, strict, incl, eye_cat, level_masks = cst
    excl = cum - logd
    cum_last = cum[RWKV_CHUNK - 1:RWKV_CHUNK, :]
    e_neg = jnp.exp(-cum)
    e_end = jnp.exp(cum_last - cum)
    at = (av * jnp.exp(excl)).astype(BF16)
    rt = (r * jnp.exp(cum)).astype(BF16)
    bt = (bv * e_neg).astype(BF16)
    kt = (k * e_neg).astype(BF16)
    vb = v.astype(BF16)
    bd = lambda x: _block_diag(x, bd_mask)

    lhs = jnp.concatenate([at, rt], axis=0)
    rhs = jnp.concatenate([bd(bt), bd(kt)], axis=0)
    amat = lax.dot_general(lhs, rhs, (((1,), (1,)), ((), ())), preferred_element_type=F32)
    yield
    n_ab = jnp.where(strict, amat[0:RWKV_CHUNK, 0:A_WIDTH], 0.0)
    n_ak = jnp.where(strict, amat[0:RWKV_CHUNK, A_WIDTH:], 0.0).astype(BF16)
    n_rb = jnp.where(incl, amat[RWKV_CHUNK:, 0:A_WIDTH], 0.0).astype(BF16)
    n_rk = jnp.where(incl, amat[RWKV_CHUNK:, A_WIDTH:], 0.0).astype(BF16)
    wy0 = jnp.dot(jnp.concatenate([n_ak, n_rk], axis=0), bd(vb), preferred_element_type=F32)
    yield
    inv = yield from _unit_lower_inverse(n_ab, eye_cat, level_masks, bd)
    return dict(
        lhs=lhs, inv=inv, w0=wy0[0:RWKV_CHUNK], y0=wy0[RWKV_CHUNK:], n_rb=n_rb, vb=vb,
        bk_end=jnp.concatenate([(bv * e_end).astype(BF16), (k * e_end).astype(BF16)], axis=0),
        decay_end=jnp.exp(cum_last),
    )


def _rwkv_chunk_apply(c, state, bd_mask):
    bd = lambda x: _block_diag(x, bd_mask)
    both = lax.dot_general(c["lhs"], state.astype(BF16), (((1,), (1,)), ((), ())), preferred_element_type=F32)
    yield
    w = both[0:RWKV_CHUNK] + c["w0"]
    ub = jnp.dot(c["inv"], bd(w.astype(BF16)), preferred_element_type=F32).astype(BF16)
    yield
    y = both[RWKV_CHUNK:] + c["y0"] + jnp.dot(c["n_rb"], bd(ub), preferred_element_type=F32)
    upd = lax.dot_general(jnp.concatenate([ub, c["vb"]], axis=0), c["bk_end"],
                          (((0,), (0,)), ((), ())), preferred_element_type=F32)
    return y, state * c["decay_end"] + upd * bd_mask.astype(F32)


def _interleave(generators):
    results = [None] * len(generators)
    active = list(enumerate(generators))
    while active:
        still = []
        for i, gen in active:
            try:
                next(gen)
                still.append((i, gen))
            except StopIteration as done:
                results[i] = done.value
        active = still
    return results


def _split2(x):
    hi = x.astype(BF16)
    return hi, (x - hi.astype(F32)).astype(BF16)


def _mm_cat2(lhs_list, b, bd):
    rows = lhs_list[0].shape[0]
    width = b.shape[1]
    b_hi, b_lo = _split2(b)
    lhs = jnp.concatenate([part for a in lhs_list for part in _split2(a)], axis=0)
    prod = jnp.dot(lhs, jnp.concatenate([bd(b_hi), bd(b_lo)], axis=1), preferred_element_type=F32)
    outs = []
    for i in range(len(lhs_list)):
        hi = prod[2 * i * rows:(2 * i + 1) * rows]
        lo = prod[(2 * i + 1) * rows:(2 * i + 2) * rows]
        outs.append((lo[:, 0:width] + hi[:, width:]) + hi[:, 0:width])
    return outs


def _unit_lower_inverse(n, eye_cat, level_masks, bd):
    m = jnp.where(level_masks[0], n, 0.0)
    inv = eye_cat + m
    (m,) = _mm_cat2([m], m, bd)
    yield
    for step in range(INV_BLOCK.bit_length() - 2):
        if step + 1 < INV_BLOCK.bit_length() - 2:
            delta, m = _mm_cat2([inv, m], m, bd)
        else:
            (delta,) = _mm_cat2([inv], m, bd)
        yield
        inv = inv + delta
    inv = inv.astype(BF16)
    for mask in level_masks[1:]:
        off = jnp.where(mask, n, 0.0).astype(BF16)
        g = jnp.dot(inv, bd(off), preferred_element_type=F32)
        yield
        inv = inv + jnp.dot(g.astype(BF16), bd(inv), preferred_element_type=F32).astype(BF16)
        yield
    return inv


def _rwkv_consts():
    w = A_WIDTH
    bd_mask = ((_iota((w, w), 0) // HEAD_DIM) == (_iota((w, w), 1) // HEAD_DIM)).astype(BF16)
    t = _iota((RWKV_CHUNK, w), 0)
    s = _iota((RWKV_CHUNK, w), 1) % RWKV_CHUNK
    level_masks = [t // INV_BLOCK == s // INV_BLOCK]
    size = INV_BLOCK
    while size < RWKV_CHUNK:
        level_masks.append((t // (2 * size) == s // (2 * size)) & (t // size != s // size))
        size *= 2
    return bd_mask, s < t, s <= t, (s == t).astype(F32), level_masks


def _load_params(refs):
    return {name: ref[...] for name, ref in zip(_RWKV_PARAMS, refs)}


def _rwkv_prompt_kernel(pa_ref, *rest):
    prm_refs = rest[:len(_RWKV_PARAMS)]
    ya_ref, shift_ref, wkv_ref, carry_ref, state_ref = rest[len(_RWKV_PARAMS):]
    i = pl.program_id(0)
    nseq, tb, _ = pa_ref.shape

    @pl.when(i == 0)
    def _():
        carry_ref[...] = jnp.zeros_like(carry_ref)
        state_ref[...] = jnp.zeros_like(state_ref)

    p = _load_params(prm_refs)
    cst = _rwkv_consts()
    n_chunks = tb // RWKV_CHUNK
    group = min(n_chunks, RWKV_GROUP)
    n_groups = n_chunks // group
    slab = group * RWKV_CHUNK
    ri = _iota((slab, slab), 0)
    ci = _iota((slab, slab), 1)
    chunk_tril = ((ri // RWKV_CHUNK == ci // RWKV_CHUNK) & (ri >= ci)).astype(F32)
    first_row = _iota((slab, A_PROJ), 0) == 0

    def prepare(n, gi):
        r0 = gi * slab
        pa = pa_ref[n, r0:r0 + slab, :]
        before = carry_ref[n] if gi == 0 else pa_ref[n, r0 - 1:r0, :]
        prev = jnp.where(first_row, before, pltpu.roll(pa, 1, axis=0))
        r, k, v, logd, av, bv, g = _rwkv_prep(pa + (prev - pa) * p["mu"], p)
        yield
        cum = _mm_sel_left(chunk_tril, logd)
        yield
        return dict(r=r, k=k, v=v, logd=logd, cum=cum, av=av, bv=bv, g=g)

    def setup(s, c):
        rows = slice(c * RWKV_CHUNK, (c + 1) * RWKV_CHUNK)
        return _rwkv_chunk_setup(s["r"][rows], s["k"][rows], s["v"][rows], s["logd"][rows], s["cum"][rows],
                                 s["av"][rows], s["bv"][rows], cst)

    def apply_group(state, group_setups):
        ys = []
        for chunk in group_setups:
            y, state = yield from _rwkv_chunk_apply(chunk, state, cst[0])
            ys.append(y)
        return jnp.concatenate(ys, axis=0), state

    states = [state_ref[n] for n in range(nseq)]
    slabs = {0: _interleave([prepare(n, 0) for n in range(nseq)])}
    setups = {}
    for gi in range(n_groups + 1):
        jobs = []
        if gi >= 1:
            jobs += [apply_group(states[n], setups[gi - 1][n]) for n in range(nseq)]
        if gi < n_groups:
            jobs += [setup(slabs[gi][n], c) for n in range(nseq) for c in range(group)]
        if gi + 1 < n_groups:
            jobs += [prepare(n, gi + 1) for n in range(nseq)]
        done = _interleave(jobs)
        if gi >= 1:
            for n in range(nseq):
                y, states[n] = done[n]
                s = slabs[gi - 1][n]
                ya_ref[n, (gi - 1) * slab:gi * slab, :] = _rwkv_post(y, s["r"], s["k"], s["v"], s["g"], p)
            done = done[nseq:]
        if gi < n_groups:
            setups[gi] = [done[n * group:(n + 1) * group] for n in range(nseq)]
            done = done[nseq * group:]
        if gi + 1 < n_groups:
            slabs[gi + 1] = done
    for n in range(nseq):
        state_ref[n] = states[n]
        last = pa_ref[n, tb - 1:tb, :]
        carry_ref[n] = last
        shift_ref[n] = last

    @pl.when(i == pl.num_programs(0) - 1)
    def _():
        for n in range(nseq):
            for h in range(A_HEADS):
                sl = slice(h * HEAD_DIM, (h + 1) * HEAD_DIM)
                wkv_ref[n, h] = state_ref[n, sl, sl]


def _rwkv_param_specs(layer):
    shapes = {"mu": (1, A_PROJ), "w2": (LANES, A_WIDTH), "a2": (LANES, A_WIDTH), "g2": (A_GATE_LORA, A_WIDTH)}
    return [_layer_spec(layer, shapes.get(name, (1, A_WIDTH))) for name in _RWKV_PARAMS]


def _rwkv_prompt(proj, prm, layer, tb):
    nseq, t, _ = proj.shape
    return pl.pallas_call(
        _rwkv_prompt_kernel,
        grid=(t // tb,),
        in_specs=[pl.BlockSpec((nseq, tb, A_PROJ), lambda i: (0, i, 0))] + _rwkv_param_specs(layer),
        out_specs=[
            pl.BlockSpec((nseq, tb, A_WIDTH), lambda i: (0, i, 0)),
            pl.BlockSpec((nseq, 1, A_PROJ), lambda i: (0, 0, 0)),
            pl.BlockSpec((nseq, A_HEADS, HEAD_DIM, HEAD_DIM), lambda i: (0, 0, 0, 0)),
        ],
        out_shape=[
            jax.ShapeDtypeStruct((nseq, t, A_WIDTH), F32),
            jax.ShapeDtypeStruct((nseq, 1, A_PROJ), F32),
            jax.ShapeDtypeStruct((nseq, A_HEADS, HEAD_DIM, HEAD_DIM), F32),
        ],
        scratch_shapes=[pltpu.VMEM((nseq, 1, A_PROJ), F32), pltpu.VMEM((nseq, A_WIDTH, A_WIDTH), F32)],
        compiler_params=pltpu.CompilerParams(
            dimension_semantics=("arbitrary",), vmem_limit_bytes=VMEM_LIMIT_BYTES),
        name="rwkv_prompt",
    )(proj, *[prm[name] for name in _RWKV_PARAMS])


def _rwkv_decode_kernel(pa_ref, shift_ref, wkv0_ref, *rest):
    prm_refs = rest[:len(_RWKV_PARAMS)]
    _, ya_ref, wkv_ref, cols_ref, rows_ref, y_ref = rest[len(_RWKV_PARAMS):]
    h = pl.program_id(0)
    p = _load_params(prm_refs)

    @pl.when(h == 0)
    def _():
        pa = pa_ref[...]
        r, k, v, logd, av, bv, g = _rwkv_prep(pa + (shift_ref[...] - pa) * p["mu"], p)
        for idx, x in enumerate((r, k, v, jnp.exp(logd), av, bv)):
            cols_ref[idx] = x.T
        for idx, x in enumerate((r, k, v, g)):
            rows_ref[idx] = x

    hs = pl.ds(pl.multiple_of(h * HEAD_DIM, HEAD_DIM), HEAD_DIM)
    r_h, k_h, v_h, d_h, a_h, b_h = [cols_ref[idx, hs, :] for idx in range(6)]
    ys = []
    for vi in range(HEAD_DIM):
        s = wkv0_ref[vi]
        u = jnp.sum(s * a_h, axis=0, keepdims=True)
        s = s * d_h + u * b_h + v_h[vi:vi + 1, :] * k_h
        wkv_ref[vi] = s
        ys.append(jnp.sum(s * r_h, axis=0, keepdims=True))
    y_ref[hs, :] = jnp.concatenate(ys, axis=0)

    @pl.when(h == A_HEADS - 1)
    def _():
        ya_ref[...] = _rwkv_post(y_ref[...].T, rows_ref[0], rows_ref[1], rows_ref[2], rows_ref[3], p)


def _rwkv_decode(proj, shift_prev, wkv0_t, prm, layer, wkv_all_t):
    nd = proj.shape[0]
    state_spec = pl.BlockSpec((None, None, HEAD_DIM, HEAD_DIM, nd), lambda h: (layer, h, 0, 0, 0))
    n_in = 3 + len(_RWKV_PARAMS)
    return pl.pallas_call(
        _rwkv_decode_kernel,
        grid=(A_HEADS,),
        in_specs=[
            pl.BlockSpec((nd, A_PROJ), lambda h: (0, 0)),
            pl.BlockSpec((None, nd, A_PROJ), lambda h: (layer, 0, 0)),
            state_spec,
        ] + _rwkv_param_specs(layer) + [pl.BlockSpec(memory_space=pl.ANY)],
        out_specs=[pl.BlockSpec((nd, A_WIDTH), lambda h: (0, 0)), state_spec],
        out_shape=[jax.ShapeDtypeStruct((nd, A_WIDTH), F32), jax.ShapeDtypeStruct(wkv_all_t.shape, F32)],
        input_output_aliases={n_in: 1},
        scratch_shapes=[pltpu.VMEM((6, A_WIDTH, nd), F32), pltpu.VMEM((4, nd, A_WIDTH), F32),
                        pltpu.VMEM((A_WIDTH, nd), F32)],
        compiler_params=pltpu.CompilerParams(
            dimension_semantics=("arbitrary",), vmem_limit_bytes=VMEM_LIMIT_BYTES),
        name="rwkv_decode",
    )(proj, shift_prev, wkv0_t, *[prm[name] for name in _RWKV_PARAMS], wkv_all_t)


def _place_q_heads(qn):
    lane_half = _iota((qn.shape[0], B_KV_WIDTH), 1) // HEAD_DIM
    placed = []
    for h in range(B_HEADS):
        kh, g = divmod(h, B_GROUP)
        chunk = qn[:, kh * B_KV_WIDTH:(kh + 1) * B_KV_WIDTH]
        if g != kh:
            chunk = pltpu.roll(chunk, HEAD_DIM, axis=1)
        placed.append(jnp.where(lane_half == kh, chunk, 0.0))
    return placed


def _gather_o_heads(o_heads):
    lane_half = _iota(o_heads[0].shape, 1) // HEAD_DIM
    chunks = []
    for kh in range(B_KV_HEADS):
        parts = []
        for g in range(B_GROUP):
            o = o_heads[kh * B_GROUP + g]
            parts.append(o if g == kh else pltpu.roll(o, HEAD_DIM, axis=1))
        chunks.append(jnp.where(lane_half == 0, parts[0], parts[1]))
    return jnp.concatenate(chunks, axis=1)


def _swa_prompt_kernel(layer, sinks_ref, pb_ref, gq_ref, gk_ref, yb_ref, nk_ref, nv_ref, kprev_ref, vprev_ref):
    j = pl.program_id(1)
    tq = pb_ref.shape[0]
    blk = SWA_BLOCK

    @pl.when(j == 0)
    def _():
        kprev_ref[...] = jnp.zeros_like(kprev_ref)
        vprev_ref[...] = jnp.zeros_like(vprev_ref)

    scale = HEAD_DIM ** -0.5
    rows = _iota((B_GROUP * blk, 2 * blk), 0)
    cols = _iota((B_GROUP * blk, 2 * blk), 1)
    qi = rows % blk
    band = (cols >= qi) & (cols <= qi + WINDOW)
    n_sub = tq // blk
    x = pb_ref[...]
    qn = _head_rms_norm(x[:, 0:B_WIDTH], gq_ref[...])
    kn = _head_rms_norm(x[:, B_WIDTH:B_WIDTH + B_KV_WIDTH], gk_ref[...])
    v = x[:, B_WIDTH + B_KV_WIDTH:B_PROJ]
    kfull = jnp.concatenate([kprev_ref[...], kn], axis=0)
    vfull = jnp.concatenate([vprev_ref[...], v], axis=0)
    placed = _place_q_heads(qn)

    def attend(sb, kh):
        q2 = jnp.concatenate([pl_h[sb * blk:(sb + 1) * blk] for pl_h in placed[kh * B_GROUP:(kh + 1) * B_GROUP]],
                             axis=0)
        s = _mm_nt(q2, kfull[sb * blk:(sb + 2) * blk])
        yield
        first_key = jnp.where(j * n_sub + sb > 0, 0, blk)
        s = jnp.where(band & (cols >= first_key), s * scale, -jnp.inf)
        sink = jnp.where(rows[:, 0:1] < blk, sinks_ref[layer, kh * B_GROUP], sinks_ref[layer, kh * B_GROUP + 1])
        m = jnp.maximum(jnp.max(s, axis=-1, keepdims=True), sink)
        pr = jnp.exp(s - m)
        den = jnp.sum(pr, axis=-1, keepdims=True) + jnp.exp(sink - m)
        o = _mm(pr, vfull[sb * blk:(sb + 2) * blk])
        yield
        return o / den

    outs = _interleave([attend(sb, kh) for sb in range(n_sub) for kh in range(B_KV_HEADS)])
    for sb in range(n_sub):
        heads = [outs[sb * B_KV_HEADS + kh][g * blk:(g + 1) * blk]
                 for kh in range(B_KV_HEADS) for g in range(B_GROUP)]
        yb_ref[sb * blk:(sb + 1) * blk, :] = _gather_o_heads(heads)
    kprev_ref[...] = kn[tq - blk:tq]
    vprev_ref[...] = v[tq - blk:tq]
    nk_ref[...] = kn[tq - blk:tq]
    nv_ref[...] = v[tq - blk:tq]


def _swa_prompt(proj, gq, gk, sinks, layer, tq):
    nseq, t, _ = proj.shape
    col = A_PROJ // B_PROJ
    return pl.pallas_call(
        functools.partial(_swa_prompt_kernel, layer),
        grid=(nseq, t // tq),
        in_specs=[
            pl.BlockSpec(memory_space=pltpu.SMEM),
            pl.BlockSpec((None, tq, B_PROJ), lambda n, j: (n, j, col)),
            _layer_spec(layer, (1, B_WIDTH)),
            _layer_spec(layer, (1, B_KV_WIDTH)),
        ],
        out_specs=[
            pl.BlockSpec((None, tq, B_WIDTH), lambda n, j: (n, j, 0)),
            pl.BlockSpec((None, WINDOW, B_KV_WIDTH), lambda n, j: (n, 0, 0)),
            pl.BlockSpec((None, WINDOW, B_KV_WIDTH), lambda n, j: (n, 0, 0)),
        ],
        out_shape=[
            jax.ShapeDtypeStruct((nseq, t, B_WIDTH), F32),
            jax.ShapeDtypeStruct((nseq, WINDOW, B_KV_WIDTH), F32),
            jax.ShapeDtypeStruct((nseq, WINDOW, B_KV_WIDTH), F32),
        ],
        scratch_shapes=[pltpu.VMEM((SWA_BLOCK, B_KV_WIDTH), F32), pltpu.VMEM((SWA_BLOCK, B_KV_WIDTH), F32)],
        compiler_params=pltpu.CompilerParams(
            dimension_semantics=("arbitrary", "arbitrary"), vmem_limit_bytes=VMEM_LIMIT_BYTES),
        name="swa_prompt",
    )(sinks, proj, gq, gk)


def _swa_decode_kernel(layer, sinks_ref, pb_ref, ck_ref, cv_ref, gq_ref, gk_ref, _k_all, _v_all,
                       yb_ref, nk_ref, nv_ref):
    x = pb_ref[...]
    nb = x.shape[0]
    qn = _head_rms_norm(x[:, 0:B_WIDTH], gq_ref[...])
    kn = _head_rms_norm(x[:, B_WIDTH:B_WIDTH + B_KV_WIDTH], gk_ref[...])
    v = x[:, B_WIDTH + B_KV_WIDTH:B_PROJ]
    placed = _place_q_heads(qn)
    scale = HEAD_DIM ** -0.5
    head = _iota((B_HEADS, 1), 0)
    sink = jnp.zeros((B_HEADS, 1), F32)
    for h in range(B_HEADS):
        sink = jnp.where(head == h, sinks_ref[layer, h], sink)
    last_key = _iota((B_KV_WIDTH, WINDOW), 1) == WINDOW - 1
    k_cols = _column_broadcasts(kn, WINDOW)
    v_cols = _column_broadcasts(v, WINDOW)

    def attend(j):
        kc = ck_ref[j]
        vc = cv_ref[j]
        group = slice(j * WINDOW, (j + 1) * WINDOW)
        q4 = jnp.concatenate([pl_h[j:j + 1] for pl_h in placed], axis=0)
        s = _mm(q4, kc) * scale
        nk_ref[j] = jnp.where(last_key, k_cols[:, group], pltpu.roll(kc, WINDOW - 1, axis=1))
        nv_ref[j] = jnp.where(last_key, v_cols[:, group], pltpu.roll(vc, WINDOW - 1, axis=1))
        yield
        s_new = jnp.sum(q4 * kn[j:j + 1], axis=-1, keepdims=True) * scale
        m = jnp.maximum(jnp.maximum(jnp.max(s, axis=-1, keepdims=True), s_new), sink)
        pr = jnp.exp(s - m)
        p_new = jnp.exp(s_new - m)
        den = jnp.sum(pr, axis=-1, keepdims=True) + p_new + jnp.exp(sink - m)
        o = _mm_nt(pr, vc)
        yield
        return (o + p_new * v[j:j + 1]) / den

    o_rows = _interleave([attend(j) for j in range(nb)])
    outs = [jnp.concatenate([o[h:h + 1] for o in o_rows], axis=0) for h in range(B_HEADS)]
    yb_ref[...] = _gather_o_heads(outs)


def _swa_decode(proj, cache_k, cache_v, gq, gk, sinks, layer, k_all, v_all):
    nd = proj.shape[0]
    nb = DEC_BLOCK
    col = A_PROJ // B_PROJ
    cache_spec = pl.BlockSpec((None, nb, WINDOW, B_KV_WIDTH), lambda i: (layer, i, 0, 0))
    any_spec = pl.BlockSpec(memory_space=pl.ANY)
    return pl.pallas_call(
        functools.partial(_swa_decode_kernel, layer),
        grid=(nd // nb,),
        in_specs=[
            pl.BlockSpec(memory_space=pltpu.SMEM),
            pl.BlockSpec((nb, B_PROJ), lambda i: (i, col)),
            cache_spec, cache_spec,
            _layer_spec(layer, (1, B_WIDTH)),
            _layer_spec(layer, (1, B_KV_WIDTH)),
            any_spec, any_spec,
        ],
        out_specs=[pl.BlockSpec((nb, B_WIDTH), lambda i: (i, 0)), cache_spec, cache_spec],
        out_shape=[
            jax.ShapeDtypeStruct((nd, B_WIDTH), F32),
            jax.ShapeDtypeStruct(k_all.shape, F32),
            jax.ShapeDtypeStruct(v_all.shape, F32),
        ],
        input_output_aliases={6: 1, 7: 2},
        compiler_params=pltpu.CompilerParams(
            dimension_semantics=("arbitrary",), vmem_limit_bytes=VMEM_LIMIT_BYTES),
        name="swa_decode",
    )(sinks, proj, cache_k, cache_v, gq, gk, k_all, v_all)


def _head_expand(width_per_head):
    n = C_HEADS * width_per_head
    return (_iota((LANES, n), 0) == _iota((LANES, n), 1) // width_per_head).astype(F32)


def _ssd_prompt_kernel(z_ref, xbc_ref, dt_ref, cw_ref, cb_ref, dtb_ref, aneg_ref, dskip_ref, ng_ref,
                       yc_ref, tail_ref, hout_ref, xpad_ref, h_ref):
    j = pl.program_id(1)
    tb = xbc_ref.shape[0]

    @pl.when(j == 0)
    def _():
        xpad_ref[0:SUBLANES, :] = jnp.zeros((SUBLANES, CONV_DIM), F32)
        h_ref[...] = jnp.zeros_like(h_ref)

    xpad_ref[SUBLANES:SUBLANES + tb, :] = xbc_ref[...]
    cw = cw_ref[...]
    conv = cb_ref[...]
    for k in range(SSM_CONV):
        conv = conv + cw[SSM_CONV - 1 - k:SSM_CONV - k] * xpad_ref[SUBLANES - k:SUBLANES - k + tb, :]
    last_rows = xbc_ref[tb - SUBLANES:tb, :]
    tail_ref[...] = last_rows
    xpad_ref[0:SUBLANES, :] = last_rows
    act = _silu(conv)
    x = act[:, 0:C_WIDTH]
    bm = act[:, C_WIDTH:C_WIDTH + C_GROUPS * D_STATE]
    cm = act[:, C_WIDTH + C_GROUPS * D_STATE:CONV_DIM]
    dt = _softplus(dt_ref[...] + dtb_ref[...])
    xdt = x * _mm_sel_right(dt, _head_expand(HEAD_DIM))

    ch = SSD_CHUNK
    n_chunks = tb // ch
    ri = _iota((tb, tb), 0)
    ci = _iota((tb, tb), 1)
    chunk_tril = ((ri // ch == ci // ch) & (ri >= ci)).astype(F32)
    acum = _mm_sel_left(chunk_tril, dt * aneg_ref[...])
    acum_x = _mm_sel_right(acum, _head_expand(D_STATE))
    causal = _iota((ch, ch), 0) >= _iota((ch, ch), 1)
    lane_lo = _iota((ch, 2 * HEAD_DIM), 1) < HEAD_DIM
    row_lo = _iota((2 * HEAD_DIM, D_STATE), 0) < HEAD_DIM
    heads_per_group = C_HEADS // C_GROUPS
    n_pairs = C_HEADS // 2
    head_lanes = lambda e: slice(e * D_STATE, (e + 1) * D_STATE)
    pair_lanes = lambda pr: slice(pr * 2 * HEAD_DIM, (pr + 1) * 2 * HEAD_DIM)
    group_lanes = lambda pr: slice(((2 * pr) // heads_per_group) * D_STATE, ((2 * pr) // heads_per_group + 1) * D_STATE)

    def chunk_local(c):
        rows = slice(c * ch, (c + 1) * ch)
        acx = acum_x[rows]
        ac_t = acum[rows].T
        last = acx[ch - 1:ch]
        decay_to_end = jnp.exp(last - acx)
        cbs = [_mm_nt(cm[rows, g * D_STATE:(g + 1) * D_STATE], bm[rows, g * D_STATE:(g + 1) * D_STATE])
               for g in range(C_GROUPS)]
        yield
        y_diag, states = [], []
        for pr in range(n_pairs):
            e0, e1 = 2 * pr, 2 * pr + 1
            xp = xdt[rows, pair_lanes(pr)]
            yd = []
            for e in (e0, e1):
                seg = acx[:, head_lanes(e)] - ac_t[e:e + 1, :]
                lmat = jnp.exp(jnp.where(causal, seg, -jnp.inf))
                yd.append(_mm(cbs[(2 * pr) // heads_per_group] * lmat, xp))
            y_diag.append(jnp.where(lane_lo, yd[0], yd[1]))
            scale = jnp.where(lane_lo, decay_to_end[:, head_lanes(e0)], decay_to_end[:, head_lanes(e1)])
            states.append(_mm_tn(xp * scale, bm[rows, group_lanes(pr)]))
        yield
        chunk_decay = jnp.exp(last)
        cds = [jnp.where(row_lo, chunk_decay[:, head_lanes(2 * pr)], chunk_decay[:, head_lanes(2 * pr + 1)])
               for pr in range(n_pairs)]
        return y_diag, states, cds, jnp.exp(acx)

    local = _interleave([chunk_local(c) for c in range(n_chunks)])
    h = [h_ref[pair_lanes(pr), :] for pr in range(n_pairs)]
    y_chunks = []
    for c in range(n_chunks):
        rows = slice(c * ch, (c + 1) * ch)
        y_diag, states, cds, ea = local[c]
        pairs = []
        for pr in range(n_pairs):
            y_off = _mm_nt(cm[rows, group_lanes(pr)], h[pr])
            pairs.append(y_diag[pr] + y_off * jnp.where(lane_lo, ea[:, head_lanes(2 * pr)], ea[:, head_lanes(2 * pr + 1)]))
            h[pr] = h[pr] * cds[pr] + states[pr]
        y_chunks.append(jnp.concatenate(pairs, axis=1))
    for pr in range(n_pairs):
        h_ref[pair_lanes(pr), :] = h[pr]
        hout_ref[pair_lanes(pr), :] = h[pr]

    y = (jnp.concatenate(y_chunks, axis=0) + dskip_ref[...] * x) * _silu(z_ref[...])
    yc_ref[...] = _rms_norm(y, ng_ref[...])


def _ssd_prompt(proj, prm, layer, tb):
    nseq, t, _ = proj.shape
    return pl.pallas_call(
        _ssd_prompt_kernel,
        grid=(nseq, t // tb),
        in_specs=[
            pl.BlockSpec((None, tb, C_WIDTH), lambda n, j: (n, j, Z_OFF // C_WIDTH)),
            pl.BlockSpec((None, tb, CONV_DIM), lambda n, j: (n, j, XBC_OFF // CONV_DIM)),
            pl.BlockSpec((None, tb, LANES), lambda n, j: (n, j, DT_OFF // LANES)),
            _layer_spec(layer, (SSM_CONV, CONV_DIM)),
            _layer_spec(layer, (1, CONV_DIM)),
            _layer_spec(layer, (1, LANES)),
            _layer_spec(layer, (1, LANES)),
            _layer_spec(layer, (1, C_WIDTH)),
            _layer_spec(layer, (1, C_WIDTH)),
        ],
        out_specs=[
            pl.BlockSpec((None, tb, C_WIDTH), lambda n, j: (n, j, 0)),
            pl.BlockSpec((None, SUBLANES, CONV_DIM), lambda n, j: (n, 0, 0)),
            pl.BlockSpec((None, C_WIDTH, D_STATE), lambda n, j: (n, 0, 0)),
        ],
        out_shape=[
            jax.ShapeDtypeStruct((nseq, t, C_WIDTH), F32),
            jax.ShapeDtypeStruct((nseq, SUBLANES, CONV_DIM), F32),
            jax.ShapeDtypeStruct((nseq, C_WIDTH, D_STATE), F32),
        ],
        scratch_shapes=[
            pltpu.VMEM((SUBLANES + tb, CONV_DIM), F32),
            pltpu.VMEM((C_WIDTH, D_STATE), F32),
        ],
        compiler_params=pltpu.CompilerParams(
            dimension_semantics=("arbitrary", "arbitrary"), vmem_limit_bytes=VMEM_LIMIT_BYTES),
        name="ssd_prompt",
    )(proj, proj, proj, prm["conv_w"], prm["conv_b"], prm["dt_bias"], prm["a_neg"], prm["d_skip"], prm["norm_g"])


def _ssd_decode_kernel(z_ref, xbc_ref, dt_ref, buf_ref, h0_ref, cw_ref, cb_ref, dtb_ref, aneg_ref, dskip_ref, ng_ref,
                       _h_all, yc_ref, nbuf_ref, hout_ref):
    xbc = xbc_ref[...]
    nb = xbc.shape[0]
    cw = cw_ref[...]
    conv = cb_ref[...] + cw[SSM_CONV - 1:SSM_CONV] * xbc
    for k in range(SSM_CONV - 1):
        conv = conv + cw[k:k + 1] * buf_ref[k]
        nbuf_ref[k] = buf_ref[k + 1] if k + 1 < SSM_CONV - 1 else xbc
    act = _silu(conv)
    x = act[:, 0:C_WIDTH]
    bm = act[:, C_WIDTH:C_WIDTH + C_GROUPS * D_STATE]
    cm = act[:, C_WIDTH + C_GROUPS * D_STATE:CONV_DIM]
    dt = _softplus(dt_ref[...] + dtb_ref[...])
    decay = _mm_sel_right(jnp.exp(dt * aneg_ref[...]), _head_expand(D_STATE))
    xdt = x * _mm_sel_right(dt, _head_expand(HEAD_DIM))
    heads_per_group = C_HEADS // C_GROUPS
    x_cols = _column_broadcasts(xdt, D_STATE)
    hc = []
    for j in range(nb):
        for e in range(C_HEADS):
            gs = slice((e // heads_per_group) * D_STATE, (e // heads_per_group + 1) * D_STATE)
            hs = slice(e * HEAD_DIM, (e + 1) * HEAD_DIM)
            h = (h0_ref[j, hs, :] * decay[j:j + 1, e * D_STATE:(e + 1) * D_STATE]
                 + x_cols[hs, j * D_STATE:(j + 1) * D_STATE] * bm[j:j + 1, gs])
            hout_ref[j, hs, :] = h
            hc.append(h * cm[j:j + 1, gs])
    y_cols = _row_sums(hc, D_STATE)
    lane = _iota((C_WIDTH, nb), 1)
    y_t = jnp.zeros((C_WIDTH, nb), F32)
    for j in range(nb):
        y_t = jnp.where(lane == j, y_cols[j * C_WIDTH:(j + 1) * C_WIDTH, 0:nb], y_t)
    y = _mm_nt_sel_left(_eye(nb), y_t)
    y = (y + dskip_ref[...] * x) * _silu(z_ref[...])
    yc_ref[...] = _rms_norm(y, ng_ref[...])


def _ssd_decode(proj, conv_buf_t, h0, prm, layer, h_all):
    nd = proj.shape[0]
    nb = DEC_BLOCK
    state_spec = pl.BlockSpec((None, nb, C_WIDTH, D_STATE), lambda i: (layer, i, 0, 0))
    return pl.pallas_call(
        _ssd_decode_kernel,
        grid=(nd // nb,),
        in_specs=[
            pl.BlockSpec((nb, C_WIDTH), lambda i: (i, Z_OFF // C_WIDTH)),
            pl.BlockSpec((nb, CONV_DIM), lambda i: (i, XBC_OFF // CONV_DIM)),
            pl.BlockSpec((nb, LANES), lambda i: (i, DT_OFF // LANES)),
            pl.BlockSpec((None, SSM_CONV - 1, nb, CONV_DIM), lambda i: (layer, 0, i, 0)),
            state_spec,
            _layer_spec(layer, (SSM_CONV, CONV_DIM)),
            _layer_spec(layer, (1, CONV_DIM)),
            _layer_spec(layer, (1, LANES)),
            _layer_spec(layer, (1, LANES)),
            _layer_spec(layer, (1, C_WIDTH)),
            _layer_spec(layer, (1, C_WIDTH)),
            pl.BlockSpec(memory_space=pl.ANY),
        ],
        out_specs=[
            pl.BlockSpec((nb, C_WIDTH), lambda i: (i, 0)),
            pl.BlockSpec((SSM_CONV - 1, nb, CONV_DIM), lambda i: (0, i, 0)),
            state_spec,
        ],
        out_shape=[
            jax.ShapeDtypeStruct((nd, C_WIDTH), F32),
            jax.ShapeDtypeStruct((SSM_CONV - 1, nd, CONV_DIM), F32),
            jax.ShapeDtypeStruct(h_all.shape, F32),
        ],
        input_output_aliases={11: 2},
        compiler_params=pltpu.CompilerParams(
            dimension_semantics=("arbitrary",), vmem_limit_bytes=VMEM_LIMIT_BYTES),
        name="ssd_decode",
    )(proj, proj, proj, conv_buf_t, h0, prm["conv_w"], prm["conv_b"], prm["dt_bias"], prm["a_neg"],
      prm["d_skip"], prm["norm_g"], h_all)


def _out_ffn_core(x, mix, g1, sh2, sc2, g2, gn, wo_ref, wu_ref, cw_ref, cb_ref, wd_ref, conv_fn):
    x1 = x + g1 * _mm(mix, wo_ref[...])
    h = (_rms_norm(x1, gn) * (1.0 + sc2) + sh2).astype(BF16)
    acc = jnp.zeros(x.shape, F32)
    for c in range(D_FF // FF_CHUNK):
        cs = slice(c * FF_CHUNK, (c + 1) * FF_CHUNK)
        gate = jnp.dot(h, wu_ref[:, cs], preferred_element_type=F32)
        val = jnp.dot(h, wu_ref[:, D_FF + c * FF_CHUNK:D_FF + (c + 1) * FF_CHUNK], preferred_element_type=F32)
        gate = conv_fn(gate, cs, cw_ref[:, cs], cb_ref[:, cs])
        acc = acc + _mm(_silu(gate) * val, wd_ref[cs, :])
    return x1 + g2 * acc


def _out_ffn_prompt_kernel(tiles_per_seq, x_ref, ya_ref, yb_ref, yc_ref, g1_ref, sh2_ref, sc2_ref, g2_ref, gn_ref,
                           wo_ref, wu_ref, cw_ref, cb_ref, wd_ref, o_ref, tail_ref, carry_ref):
    i = pl.program_id(0)
    tm = x_ref.shape[0]

    @pl.when(i % tiles_per_seq == 0)
    def _():
        carry_ref[...] = jnp.zeros_like(carry_ref)

    def conv_fn(gate, cs, cw, cb):
        tail = carry_ref[:, cs]
        out = cb + cw[FFN_CONV - 1:FFN_CONV] * gate
        for k in range(1, FFN_CONV):
            out = out + cw[FFN_CONV - 1 - k:FFN_CONV - k] * _shift_rows(gate, k, tail)
        last_rows = gate[tm - SUBLANES:tm, :]
        carry_ref[:, cs] = last_rows
        tail_ref[:, cs] = last_rows
        return out

    mix = jnp.concatenate([ya_ref[...], yb_ref[...], yc_ref[...]], axis=1)
    o_ref[...] = _out_ffn_core(x_ref[...], mix, g1_ref[...], sh2_ref[...], sc2_ref[...], g2_ref[...], gn_ref[...],
                               wo_ref, wu_ref, cw_ref, cb_ref, wd_ref, conv_fn)


def _out_ffn_decode_kernel(x_ref, ya_ref, yb_ref, yc_ref, g1_ref, sh2_ref, sc2_ref, g2_ref, gn_ref,
                           wo_ref, wu_ref, cw_ref, cb_ref, wd_ref, buf_ref, o_ref, nbuf_ref):
    def conv_fn(gate, cs, cw, cb):
        out = cb + cw[FFN_CONV - 1:FFN_CONV] * gate
        for k in range(FFN_CONV - 1):
            out = out + cw[k:k + 1] * buf_ref[k, :, cs]
            nbuf_ref[k, :, cs] = buf_ref[k + 1, :, cs] if k + 1 < FFN_CONV - 1 else gate
        return out

    mix = jnp.concatenate([ya_ref[...], yb_ref[...], yc_ref[...]], axis=1)
    o_ref[...] = _out_ffn_core(x_ref[...], mix, g1_ref[...], sh2_ref[...], sc2_ref[...], g2_ref[...], gn_ref[...],
                               wo_ref, wu_ref, cw_ref, cb_ref, wd_ref, conv_fn)


def _out_ffn(x, ya, yb, yc, mod, prm, layer, tm, per_row, tiles_per_seq, conv_buf_t=None):
    rows = x.shape[0]
    nseq = rows // (tm * tiles_per_seq) if not per_row else None
    row_spec = lambda w: pl.BlockSpec((tm, w), lambda i: (i, 0))
    in_specs = [row_spec(D_MODEL), row_spec(A_WIDTH), row_spec(B_WIDTH), row_spec(C_WIDTH)]
    in_specs += [_mod_spec(layer, per_row, tiles_per_seq, c) for c in (2, 3, 4, 5)]
    in_specs += [
        _layer_spec(layer, (1, D_MODEL)),
        _layer_spec(layer, (D_MODEL, D_MODEL), single_buffer=True),
        _layer_spec(layer, (D_MODEL, 2 * D_FF), single_buffer=True),
        _layer_spec(layer, (FFN_CONV, D_FF)),
        _layer_spec(layer, (1, D_FF)),
        _layer_spec(layer, (D_FF, D_MODEL), single_buffer=True),
    ]
    args = [x, ya, yb, yc, mod, mod, mod, mod, prm["norm_ffn_g"], prm["w_out"], prm["w_up"], prm["ffn_conv_w"],
            prm["ffn_conv_b"], prm["w_down"]]
    params = pltpu.CompilerParams(dimension_semantics=("arbitrary",), vmem_limit_bytes=VMEM_LIMIT_BYTES)
    if per_row:
        buf_shape = (FFN_CONV - 1, rows, D_FF)
        return pl.pallas_call(
            _out_ffn_decode_kernel,
            grid=(rows // tm,),
            in_specs=in_specs + [pl.BlockSpec((None,) + buf_shape, lambda i: (layer, 0, i, 0))],
            out_specs=[row_spec(D_MODEL), pl.BlockSpec(buf_shape, lambda i: (0, i, 0))],
            out_shape=[jax.ShapeDtypeStruct((rows, D_MODEL), F32), jax.ShapeDtypeStruct(buf_shape, F32)],
            compiler_params=params,
            name="out_ffn_decode",
        )(*args, conv_buf_t)
    return pl.pallas_call(
        functools.partial(_out_ffn_prompt_kernel, tiles_per_seq),
        grid=(rows // tm,),
        in_specs=in_specs,
        out_specs=[row_spec(D_MODEL), pl.BlockSpec((None, SUBLANES, D_FF), lambda i: (i // tiles_per_seq, 0, 0))],
        out_shape=[jax.ShapeDtypeStruct((rows, D_MODEL), F32), jax.ShapeDtypeStruct((nseq, SUBLANES, D_FF), F32)],
        scratch_shapes=[pltpu.VMEM((SUBLANES, D_FF), F32)],
        compiler_params=params,
        name="out_ffn_prompt",
    )(*args)


def _row_param(p, width=None):
    if width is not None and p.shape[-1] != width:
        p = jnp.pad(p, ((0, 0), (0, width - p.shape[-1])))
    return p.reshape(p.shape[0], 1, p.shape[-1])


def _largest_tile(total, cap, quantum):
    tile = min(total, cap)
    while total % tile or tile % quantum:
        tile -= quantum
    return tile


def kernel(x_prompt, x_sample, c_prompt, c_sample, state_rwkv_shift, state_rwkv_wkv, cache_swa_k, cache_swa_v, state_ssm_conv, state_ssm, state_ffn_conv, ada_w, ada_b, norm_mix_g, norm_ffn_g, w_in, w_out, rwkv_mu, rwkv_w0, rwkv_w2, rwkv_a0, rwkv_a2, rwkv_g2, rwkv_k_k, rwkv_k_a, rwkv_r_k, rwkv_ln_g, rwkv_ln_b, attn_q_norm_g, attn_k_norm_g, attn_sinks, ssm_conv_w, ssm_conv_b, ssm_dt_bias, ssm_a_log, ssm_d, ssm_norm_g, ffn_w_up, ffn_conv_w, ffn_conv_b, ffn_w_down):
    depth = w_in.shape[0]
    nseq, t, _ = x_prompt.shape
    nd = x_sample.shape[0]
    assert x_sample.shape[1] == 1 and nd % DEC_BLOCK == 0 and t % SSD_CHUNK == 0

    tm = _largest_tile(t, 512, SUBLANES)
    tiles_per_seq = t // tm
    tb_rwkv = _largest_tile(t, 4 * RWKV_GROUP * RWKV_CHUNK, RWKV_CHUNK)
    tq_swa = _largest_tile(t, 512, SWA_BLOCK)
    tb_ssd = _largest_tile(t, 512, SSD_CHUNK)

    rwkv_prm = {
        "mu": _row_param(rwkv_mu), "w0": _row_param(rwkv_w0), "a0": _row_param(rwkv_a0),
        "w2": jnp.pad(rwkv_w2, ((0, 0), (0, LANES - A_DECAY_LORA), (0, 0))),
        "a2": jnp.pad(rwkv_a2, ((0, 0), (A_DECAY_LORA, 0), (0, 0))),
        "g2": rwkv_g2,
        "k_k": _row_param(rwkv_k_k), "k_a": _row_param(rwkv_k_a),
        "r_k": _row_param(rwkv_r_k.reshape(depth, A_WIDTH)),
        "ln_g": _row_param(rwkv_ln_g), "ln_b": _row_param(rwkv_ln_b),
    }
    gq = _row_param(jnp.tile(attn_q_norm_g, (1, B_HEADS)))
    gk = _row_param(jnp.tile(attn_k_norm_g, (1, B_KV_HEADS)))
    ssd_prm = {
        "conv_w": ssm_conv_w, "conv_b": _row_param(ssm_conv_b),
        "dt_bias": _row_param(ssm_dt_bias, LANES),
        "a_neg": _row_param(-jnp.exp(ssm_a_log), LANES),
        "d_skip": _row_param(jnp.repeat(ssm_d, HEAD_DIM, axis=1)),
        "norm_g": _row_param(ssm_norm_g),
    }
    ffn_prm = {
        "norm_ffn_g": _row_param(norm_ffn_g), "w_out": w_out.astype(BF16), "w_up": ffn_w_up.astype(BF16),
        "ffn_conv_w": ffn_conv_w, "ffn_conv_b": _row_param(ffn_conv_b), "w_down": ffn_w_down.astype(BF16),
    }
    norm_mix = _row_param(norm_mix_g)

    c_all = jnp.concatenate([jnp.pad(c_prompt, ((0, SUBLANES - nseq), (0, 0))), c_sample], axis=0)
    mod = _ada(c_all, ada_w, ada_b)
    mod_p = mod[:, 0:nseq].reshape(depth, nseq, 1, N_MOD * D_MODEL)
    mod_s = mod[:, SUBLANES:]

    to_keys_minor = lambda c: jnp.transpose(c, (0, 1, 3, 4, 2)).reshape(depth, nd, B_KV_WIDTH, WINDOW)
    from_keys_minor = lambda c: jnp.transpose(c.reshape(depth, nd, B_KV_HEADS, HEAD_DIM, WINDOW), (0, 1, 4, 2, 3))
    cache_k = to_keys_minor(cache_swa_k)
    cache_v = to_keys_minor(cache_swa_v)
    wkv0_t = jnp.transpose(state_rwkv_wkv, (0, 2, 3, 4, 1))
    ssm_conv_t = jnp.swapaxes(state_ssm_conv, 1, 2)
    ssm_h0 = state_ssm.reshape(depth, nd, C_WIDTH, D_STATE)
    ffn_conv_t = jnp.swapaxes(state_ffn_conv, 1, 2)

    xp = x_prompt.reshape(nseq * t, D_MODEL)
    xs = x_sample.reshape(nd, D_MODEL)
    prompt_new = [[] for _ in range(7)]
    sample_new = [[] for _ in range(3)]
    s_wkv = jnp.zeros(wkv0_t.shape, F32)
    s_k = jnp.zeros(cache_k.shape, F32)
    s_v = jnp.zeros(cache_v.shape, F32)
    s_ssm = jnp.zeros(ssm_h0.shape, F32)
    for l in range(depth):
        proj = _in_proj(xp, mod_p, norm_mix, w_in, l, tm, 0, tiles_per_seq).reshape(nseq, t, PROJ_PAD)
        ya, p_shift, p_wkv = _rwkv_prompt(proj, rwkv_prm, l, tb_rwkv)
        yb, p_k, p_v = _swa_prompt(proj, gq, gk, attn_sinks, l, tq_swa)
        yc, p_conv, p_ssm = _ssd_prompt(proj, ssd_prm, l, tb_ssd)
        xp, p_ffn = _out_ffn(xp, ya.reshape(nseq * t, A_WIDTH), yb.reshape(nseq * t, B_WIDTH),
                             yc.reshape(nseq * t, C_WIDTH), mod_p, ffn_prm, l, tm, 0, tiles_per_seq)
        new_p = (p_shift.reshape(nseq, A_PROJ), p_wkv,
                 p_k.reshape(nseq, WINDOW, B_KV_HEADS, HEAD_DIM), p_v.reshape(nseq, WINDOW, B_KV_HEADS, HEAD_DIM),
                 p_conv[:, SUBLANES - (SSM_CONV - 1):], p_ssm.reshape(nseq, C_HEADS, HEAD_DIM, D_STATE),
                 p_ffn[:, SUBLANES - (FFN_CONV - 1):])
        proj_s = _in_proj(xs, mod_s, norm_mix, w_in, l, nd, nd, 1)
        ya, s_wkv = _rwkv_decode(proj_s, state_rwkv_shift, wkv0_t, rwkv_prm, l, s_wkv)
        yb, s_k, s_v = _swa_decode(proj_s, cache_k, cache_v, gq, gk, attn_sinks, l, s_k, s_v)
        yc, s_conv_t, s_ssm = _ssd_decode(proj_s, ssm_conv_t, ssm_h0, ssd_prm, l, s_ssm)
        xs, s_ffn_t = _out_ffn(xs, ya, yb, yc, mod_s, ffn_prm, l, nd, nd, 1, ffn_conv_t)
        new_s = (proj_s[:, 0:A_PROJ], jnp.swapaxes(s_conv_t, 0, 1), jnp.swapaxes(s_ffn_t, 0, 1))
        for lst, arr in zip(prompt_new, new_p):
            lst.append(arr)
        for lst, arr in zip(sample_new, new_s):
            lst.append(arr)
    outs_p = [jnp.stack(lst) for lst in prompt_new]
    s_shift, s_conv, s_ffn = [jnp.stack(lst) for lst in sample_new]
    outs_s = (s_shift, jnp.transpose(s_wkv, (0, 4, 1, 2, 3)), from_keys_minor(s_k), from_keys_minor(s_v), s_conv,
              s_ssm.reshape(depth, nd, C_HEADS, HEAD_DIM, D_STATE), s_ffn)
    return (xp.reshape(nseq, t, D_MODEL), xs.reshape(nd, 1, D_MODEL), *outs_p, *outs_s)
```

```python
import functools

import jax
import jax.numpy as jnp
from jax import lax
from jax.experimental import pallas as pl
from jax.experimental.pallas import tpu as pltpu

F32 = jnp.float32
BF16 = jnp.bfloat16

D_MODEL = 1024
HEAD_DIM = 64
A_WIDTH = 256
A_HEADS = A_WIDTH // HEAD_DIM
A_DECAY_LORA = 64
A_AAA_LORA = 64
A_GATE_LORA = 128
A_PROJ = 3 * A_WIDTH + A_DECAY_LORA + A_AAA_LORA + A_GATE_LORA
B_WIDTH = 256
B_HEADS = B_WIDTH // HEAD_DIM
B_KV_HEADS = 2
B_GROUP = B_HEADS // B_KV_HEADS
B_KV_WIDTH = B_KV_HEADS * HEAD_DIM
B_PROJ = B_WIDTH + 2 * B_KV_WIDTH
WINDOW = 128
C_WIDTH = 512
C_HEADS = C_WIDTH // HEAD_DIM
C_GROUPS = 2
D_STATE = 128
SSM_CONV = 4
CONV_DIM = C_WIDTH + 2 * C_GROUPS * D_STATE
C_PROJ = C_WIDTH + CONV_DIM + C_HEADS
PROJ = A_PROJ + B_PROJ + C_PROJ
D_FF = 2816
FFN_CONV = 3
NORM_EPS = 1e-6
GN_EPS = 64e-5
N_MOD = 6

LANES = 128
SUBLANES = 8
VMEM_LIMIT_BYTES = 56 * 1024 * 1024

DT_OFF = A_PROJ + B_PROJ + C_WIDTH + CONV_DIM
PROJ_PAD = DT_OFF + LANES
Z_OFF = A_PROJ + B_PROJ
XBC_OFF = Z_OFF + C_WIDTH

RWKV_CHUNK = 64
INV_BLOCK = 4
RWKV_GROUP = 4
SSD_CHUNK = 128
SWA_BLOCK = 128
DEC_BLOCK = 8
FF_CHUNK = D_FF // 2


def _mm(a, b):
    return jnp.dot(a.astype(BF16), b.astype(BF16), preferred_element_type=F32)


def _mm_nt(a, b):
    return lax.dot_general(a.astype(BF16), b.astype(BF16), (((1,), (1,)), ((), ())), preferred_element_type=F32)


def _mm_tn(a, b):
    return lax.dot_general(a.astype(BF16), b.astype(BF16), (((0,), (0,)), ((), ())), preferred_element_type=F32)


def _split3(x):
    hi = x.astype(BF16)
    r1 = x - hi.astype(F32)
    mid = r1.astype(BF16)
    lo = (r1 - mid.astype(F32)).astype(BF16)
    return hi, mid, lo


def _mm_sel_left(sel, x):
    s = sel.astype(BF16)
    return jnp.dot(jnp.concatenate([s, s, s], axis=1), jnp.concatenate(_split3(x), axis=0),
                   preferred_element_type=F32)


def _mm_sel_right(x, sel):
    s = sel.astype(BF16)
    return jnp.dot(jnp.concatenate(_split3(x), axis=1), jnp.concatenate([s, s, s], axis=0),
                   preferred_element_type=F32)


def _mm_nt_sel_left(sel, x):
    s = sel.astype(BF16)
    return lax.dot_general(jnp.concatenate([s, s, s], axis=1), jnp.concatenate(_split3(x), axis=1),
                           (((1,), (1,)), ((), ())), preferred_element_type=F32)


def _row_sums(tiles, width):
    ones = jnp.ones((width, width), BF16)
    half = len(tiles) // 2
    dot = lambda xs: jnp.dot(jnp.concatenate(xs, axis=0).astype(BF16), ones, preferred_element_type=F32)
    return jnp.concatenate([dot(tiles[:half]), dot(tiles[half:])], axis=0)


def _column_broadcasts(x, group):
    nb, w = x.shape
    parts = jnp.concatenate(_split3(x), axis=0)
    parts_t = lax.dot_general(_eye(w).astype(BF16), parts, (((1,), (1,)), ((), ())),
                              preferred_element_type=F32).astype(BF16)
    src = _iota((3 * nb, nb * group), 0) % nb
    dst = _iota((3 * nb, nb * group), 1) // group
    return jnp.dot(parts_t, (src == dst).astype(BF16), preferred_element_type=F32)


def _sigmoid(x):
    return 1.0 / (1.0 + jnp.exp(-x))


def _silu(x):
    return x * _sigmoid(x)


def _softplus(x):
    return jnp.maximum(x, 0.0) + jnp.log(1.0 + jnp.exp(-jnp.abs(x)))


def _iota(shape, dim):
    return lax.broadcasted_iota(jnp.int32, shape, dim)


def _eye(n):
    return (_iota((n, n), 0) == _iota((n, n), 1)).astype(F32)


def _tril_ones(n):
    return (_iota((n, n), 0) >= _iota((n, n), 1)).astype(F32)


def _group_sum(x, group):
    tiles = []
    for t0 in range(0, x.shape[-1], LANES):
        tile = x[:, t0:t0 + LANES]
        gid = _iota(tile.shape, 1) // group
        sums = [jnp.sum(jnp.where(gid == h, tile, 0.0), axis=-1, keepdims=True) for h in range(LANES // group)]
        out = sums[-1]
        for h in range(LANES // group - 1):
            out = jnp.where(gid == h, sums[h], out)
        tiles.append(jnp.broadcast_to(out, tile.shape))
    return jnp.concatenate(tiles, axis=1) if len(tiles) > 1 else tiles[0]


def _rms_norm(x, g):
    return x * lax.rsqrt(jnp.mean(x * x, axis=-1, keepdims=True) + NORM_EPS) * g


def _head_rms_norm(x, g):
    return x * lax.rsqrt(_group_sum(x * x, HEAD_DIM) * (1.0 / HEAD_DIM) + NORM_EPS) * g


def _shift_rows(x, k, tail):
    rolled = pltpu.roll(x, k, axis=0)
    row = _iota((SUBLANES, x.shape[1]), 0)
    top = jnp.where(row < k, pltpu.roll(tail, k, axis=0), rolled[0:SUBLANES])
    return jnp.concatenate([top, rolled[SUBLANES:]], axis=0)


def _ada_kernel(c_ref, w_ref, b_ref, o_ref):
    o_ref[...] = _mm(_silu(c_ref[...]), w_ref[...]) + b_ref[...]


def _ada(c_all, ada_w, ada_b):
    depth = ada_w.shape[0]
    rows = c_all.shape[0]
    width = N_MOD * D_MODEL
    tn = 1536
    return pl.pallas_call(
        _ada_kernel,
        grid=(depth, width // tn),
        in_specs=[
            pl.BlockSpec((rows, D_MODEL), lambda l, j: (0, 0)),
            pl.BlockSpec((None, D_MODEL, tn), lambda l, j: (l, 0, j)),
            pl.BlockSpec((None, 1, tn), lambda l, j: (l, 0, j)),
        ],
        out_specs=pl.BlockSpec((None, rows, tn), lambda l, j: (l, 0, j)),
        out_shape=jax.ShapeDtypeStruct((depth, rows, width), F32),
        compiler_params=pltpu.CompilerParams(
            dimension_semantics=("arbitrary", "arbitrary"), vmem_limit_bytes=VMEM_LIMIT_BYTES),
        name="ada_mod",
    )(c_all, ada_w, ada_b.reshape(depth, 1, width))


def _in_proj_kernel(x_ref, sh_ref, sc_ref, g_ref, w_ref, o_ref):
    h = _rms_norm(x_ref[...], g_ref[...]) * (1.0 + sc_ref[...]) + sh_ref[...]
    o_ref[...] = _mm(h, w_ref[...])


def _mod_spec(layer, per_row, rows_per_seq_tiles, col):
    if per_row:
        return pl.BlockSpec((None, per_row, D_MODEL), lambda i: (layer, 0, col))
    return pl.BlockSpec((None, None, 1, D_MODEL), lambda i: (layer, i // rows_per_seq_tiles, 0, col))


def _layer_spec(layer, tail, single_buffer=False):
    idx = lambda *_: (layer,) + (0,) * len(tail)
    if single_buffer:
        return pl.BlockSpec((None,) + tuple(tail), idx, pipeline_mode=pl.Buffered(1))
    return pl.BlockSpec((None,) + tuple(tail), idx)


def _in_proj(x, mod, norm_g, w_in, layer, tm, per_row, tiles_per_seq):
    rows = x.shape[0]
    return pl.pallas_call(
        _in_proj_kernel,
        grid=(rows // tm,),
        in_specs=[
            pl.BlockSpec((tm, D_MODEL), lambda i: (i, 0)),
            _mod_spec(layer, per_row, tiles_per_seq, 0),
            _mod_spec(layer, per_row, tiles_per_seq, 1),
            _layer_spec(layer, (1, D_MODEL)),
            _layer_spec(layer, (D_MODEL, PROJ_PAD), single_buffer=True),
        ],
        out_specs=pl.BlockSpec((tm, PROJ_PAD), lambda i: (i, 0)),
        out_shape=jax.ShapeDtypeStruct((rows, PROJ_PAD), F32),
        compiler_params=pltpu.CompilerParams(
            dimension_semantics=("arbitrary",), vmem_limit_bytes=VMEM_LIMIT_BYTES),
        name="in_proj",
    )(x, mod, mod, norm_g, w_in)


_RWKV_PARAMS = ("mu", "w0", "w2", "a0", "a2", "g2", "k_k", "k_a", "r_k", "ln_g", "ln_b")


def _rwkv_prep(xs, p):
    r = xs[:, 0:A_WIDTH]
    k = xs[:, A_WIDTH:2 * A_WIDTH]
    v = xs[:, 2 * A_WIDTH:3 * A_WIDTH]
    lora_in = xs[:, 3 * A_WIDTH:3 * A_WIDTH + LANES]
    lg = xs[:, 3 * A_WIDTH + LANES:A_PROJ]
    w_log = -_softplus(-(p["w0"] + _mm(jnp.tanh(lora_in), p["w2"]))) - 0.5
    log_decay = -jnp.exp(w_log)
    a = _sigmoid(p["a0"] + _mm(lora_in, p["a2"]))
    g = _mm(_sigmoid(lg), p["g2"])
    kk = k * p["k_k"]
    kk = kk / jnp.maximum(jnp.sqrt(_group_sum(kk * kk, HEAD_DIM)), 1e-12)
    k = k * (1.0 + (a - 1.0) * p["k_a"])
    return r, k, v, log_decay, -kk, kk * a, g


def _rwkv_post(y, r, k, v, g, p):
    mean = _group_sum(y, HEAD_DIM) * (1.0 / HEAD_DIM)
    c = y - mean
    var = _group_sum(c * c, HEAD_DIM) * (1.0 / HEAD_DIM)
    yn = c * lax.rsqrt(var + GN_EPS) * p["ln_g"] + p["ln_b"]
    bonus = _group_sum(r * k * p["r_k"], HEAD_DIM) * v
    return (yn + bonus) * g


def _block_diag(x, mask01):
    return jnp.concatenate([x] * (x.shape[1] // x.shape[0]), axis=0) * mask01


def _rwkv_chunk_setup(r, k, v, logd, cum, av, bv, cst):
    bd_mask, strict, incl, eye_cat, level_masks = cst
    excl = cum - logd
    cum_last = cum[RWKV_CHUNK - 1:RWKV_CHUNK, :]
    e_neg = jnp.exp(-cum)
    e_end = jnp.exp(cum_last - cum)
    at = (av * jnp.exp(excl)).astype(BF16)
    rt = (r * jnp.exp(cum)).astype(BF16)
    bt = (bv * e_neg).astype(BF16)
    kt = (k * e_neg).astype(BF16)
    vb = v.astype(BF16)
    bd = lambda x: _block_diag(x, bd_mask)

    lhs = jnp.concatenate([at, rt], axis=0)
    rhs = jnp.concatenate([bd(bt), bd(kt)], axis=0)
    amat = lax.dot_general(lhs, rhs, (((1,), (1,)), ((), ())), preferred_element_type=F32)
    yield
    n_ab = jnp.where(strict, amat[0:RWKV_CHUNK, 0:A_WIDTH], 0.0)
    n_ak = jnp.where(strict, amat[0:RWKV_CHUNK, A_WIDTH:], 0.0).astype(BF16)
    n_rb = jnp.where(incl, amat[RWKV_CHUNK:, 0:A_WIDTH], 0.0).astype(BF16)
    n_rk = jnp.where(incl, amat[RWKV_CHUNK:, A_WIDTH:], 0.0).astype(BF16)
    wy0 = jnp.dot(jnp.concatenate([n_ak, n_rk], axis=0), bd(vb), preferred_element_type=F32)
    yield
    inv = yield from _unit_lower_inverse(n_ab, eye_cat, level_masks, bd)
    return dict(
        lhs=lhs, inv=inv, w0=wy0[0:RWKV_CHUNK], y0=wy0[RWKV_CHUNK:], n_rb=n_rb, vb=vb,
        bk_end=jnp.concatenate([(bv * e_end).astype(BF16), (k * e_end).astype(BF16)], axis=0),
        decay_end=jnp.exp(cum_last),
    )


def _rwkv_chunk_apply(c, state, bd_mask):
    bd = lambda x: _block_diag(x, bd_mask)
    both = lax.dot_general(c["lhs"], state.astype(BF16), (((1,), (1,)), ((), ())), preferred_element_type=F32)
    yield
    w = both[0:RWKV_CHUNK] + c["w0"]
    ub = jnp.dot(c["inv"], bd(w.astype(BF16)), preferred_element_type=F32).astype(BF16)
    yield
    y = both[RWKV_CHUNK:] + c["y0"] + jnp.dot(c["n_rb"], bd(ub), preferred_element_type=F32)
    upd = lax.dot_general(jnp.concatenate([ub, c["vb"]], axis=0), c["bk_end"],
                          (((0,), (0,)), ((), ())), preferred_element_type=F32)
    return y, state * c["decay_end"] + upd * bd_mask.astype(F32)


def _interleave(generators):
    results = [None] * len(generators)
    active = list(enumerate(generators))
    while active:
        still = []
        for i, gen in active:
            try:
                next(gen)
                still.append((i, gen))
            except StopIteration as done:
                results[i] = done.value
        active = still
    return results


def _split2(x):
    hi = x.astype(BF16)
    return hi, (x - hi.astype(F32)).astype(BF16)


def _mm_cat2(lhs_list, b, bd):
    rows = lhs_list[0].shape[0]
    width = b.shape[1]
    b_hi, b_lo = _split2(b)
    lhs = jnp.concatenate([part for a in lhs_list for part in _split2(a)], axis=0)
    prod = jnp.dot(lhs, jnp.concatenate([bd(b_hi), bd(b_lo)], axis=1), preferred_element_type=F32)
    outs = []
    for i in range(len(lhs_list)):
        hi = prod[2 * i * rows:(2 * i + 1) * rows]
        lo = prod[(2 * i + 1) * rows:(2 * i + 2) * rows]
        outs.append((lo[:, 0:width] + hi[:, width:]) + hi[:, 0:width])
    return outs


def _unit_lower_inverse(n, eye_cat, level_masks, bd):
    m = jnp.where(level_masks[0], n, 0.0)
    inv = eye_cat + m
    if INV_BLOCK == 4:
        mb = m.astype(BF16)
        m2 = jnp.dot(mb, bd(mb), preferred_element_type=F32)
        yield
        inv = (inv + m2) + jnp.dot(mb, bd(m2.astype(BF16)), preferred_element_type=F32)
        yield
    else:
        (m,) = _mm_cat2([m], m, bd)
        yield
        for step in range(INV_BLOCK.bit_length() - 2):
            if step + 1 < INV_BLOCK.bit_length() - 2:
                delta, m = _mm_cat2([inv, m], m, bd)
            else:
                (delta,) = _mm_cat2([inv], m, bd)
            yield
            inv = inv + delta
    inv = inv.astype(BF16)
    for mask in level_masks[1:]:
        off = jnp.where(mask, n, 0.0).astype(BF16)
        g = jnp.dot(inv, bd(off), preferred_element_type=F32)
        yield
        inv = inv + jnp.dot(g.astype(BF16), bd(inv), preferred_element_type=F32).astype(BF16)
        yield
    return inv


def _rwkv_consts():
    w = A_WIDTH
    bd_mask = ((_iota((w, w), 0) // HEAD_DIM) == (_iota((w, w), 1) // HEAD_DIM)).astype(BF16)
    t = _iota((RWKV_CHUNK, w), 0)
    s = _iota((RWKV_CHUNK, w), 1) % RWKV_CHUNK
    level_masks = [t // INV_BLOCK == s // INV_BLOCK]
    size = INV_BLOCK
    while size < RWKV_CHUNK:
        level_masks.append((t // (2 * size) == s // (2 * size)) & (t // size != s // size))
        size *= 2
    return bd_mask, s < t, s <= t, (s == t).astype(F32), level_masks


def _load_params(refs):
    return {name: ref[...] for name, ref in zip(_RWKV_PARAMS, refs)}


def _rwkv_prompt_kernel(pa_ref, *rest):
    prm_refs = rest[:len(_RWKV_PARAMS)]
    ya_ref, shift_ref, wkv_ref, carry_ref, state_ref = rest[len(_RWKV_PARAMS):]
    i = pl.program_id(0)
    nseq, tb, _ = pa_ref.shape

    @pl.when(i == 0)
    def _():
        carry_ref[...] = jnp.zeros_like(carry_ref)
        state_ref[...] = jnp.zeros_like(state_ref)

    p = _load_params(prm_refs)
    cst = _rwkv_consts()
    n_chunks = tb // RWKV_CHUNK
    group = min(n_chunks, RWKV_GROUP)
    n_groups = n_chunks // group
    slab = group * RWKV_CHUNK
    ri = _iota((slab, slab), 0)
    ci = _iota((slab, slab), 1)
    chunk_tril = ((ri // RWKV_CHUNK == ci // RWKV_CHUNK) & (ri >= ci)).astype(F32)
    first_row = _iota((slab, A_PROJ), 0) == 0

    def prepare(n, gi):
        r0 = gi * slab
        pa = pa_ref[n, r0:r0 + slab, :]
        before = carry_ref[n] if gi == 0 else pa_ref[n, r0 - 1:r0, :]
        prev = jnp.where(first_row, before, pltpu.roll(pa, 1, axis=0))
        r, k, v, logd, av, bv, g = _rwkv_prep(pa + (prev - pa) * p["mu"], p)
        yield
        cum = _mm_sel_left(chunk_tril, logd)
        yield
        return dict(r=r, k=k, v=v, logd=logd, cum=cum, av=av, bv=bv, g=g)

    def setup(s, c):
        rows = slice(c * RWKV_CHUNK, (c + 1) * RWKV_CHUNK)
        return _rwkv_chunk_setup(s["r"][rows], s["k"][rows], s["v"][rows], s["logd"][rows], s["cum"][rows],
                                 s["av"][rows], s["bv"][rows], cst)

    def apply_group(state, group_setups):
        ys = []
        for chunk in group_setups:
            y, state = yield from _rwkv_chunk_apply(chunk, state, cst[0])
            ys.append(y)
        return jnp.concatenate(ys, axis=0), state

    states = [state_ref[n] for n in range(nseq)]
    slabs = {0: _interleave([prepare(n, 0) for n in range(nseq)])}
    setups = {}
    for gi in range(n_groups + 1):
        jobs = []
        if gi >= 1:
            jobs += [apply_group(states[n], setups[gi - 1][n]) for n in range(nseq)]
        if gi < n_groups:
            jobs += [setup(slabs[gi][n], c) for n in range(nseq) for c in range(group)]
        if gi + 1 < n_groups:
            jobs += [prepare(n, gi + 1) for n in range(nseq)]
        done = _interleave(jobs)
        if gi >= 1:
            for n in range(nseq):
                y, states[n] = done[n]
                s = slabs[gi - 1][n]
                ya_ref[n, (gi - 1) * slab:gi * slab, :] = _rwkv_post(y, s["r"], s["k"], s["v"], s["g"], p)
            done = done[nseq:]
        if gi < n_groups:
            setups[gi] = [done[n * group:(n + 1) * group] for n in range(nseq)]
            done = done[nseq * group:]
        if gi + 1 < n_groups:
            slabs[gi + 1] = done
    for n in range(nseq):
        state_ref[n] = states[n]
        last = pa_ref[n, tb - 1:tb, :]
        carry_ref[n] = last
        shift_ref[n] = last

    @pl.when(i == pl.num_programs(0) - 1)
    def _():
        for n in range(nseq):
            for h in range(A_HEADS):
                sl = slice(h * HEAD_DIM, (h + 1) * HEAD_DIM)
                wkv_ref[n, h] = state_ref[n, sl, sl]


def _rwkv_param_specs(layer):
    shapes = {"mu": (1, A_PROJ), "w2": (LANES, A_WIDTH), "a2": (LANES, A_WIDTH), "g2": (A_GATE_LORA, A_WIDTH)}
    return [_layer_spec(layer, shapes.get(name, (1, A_WIDTH))) for name in _RWKV_PARAMS]


def _rwkv_prompt(proj, prm, layer, tb):
    nseq, t, _ = proj.shape
    return pl.pallas_call(
        _rwkv_prompt_kernel,
        grid=(t // tb,),
        in_specs=[pl.BlockSpec((nseq, tb, A_PROJ), lambda i: (0, i, 0))] + _rwkv_param_specs(layer),
        out_specs=[
            pl.BlockSpec((nseq, tb, A_WIDTH), lambda i: (0, i, 0)),
            pl.BlockSpec((nseq, 1, A_PROJ), lambda i: (0, 0, 0)),
            pl.BlockSpec((nseq, A_HEADS, HEAD_DIM, HEAD_DIM), lambda i: (0, 0, 0, 0)),
        ],
        out_shape=[
            jax.ShapeDtypeStruct((nseq, t, A_WIDTH), F32),
            jax.ShapeDtypeStruct((nseq, 1, A_PROJ), F32),
            jax.ShapeDtypeStruct((nseq, A_HEADS, HEAD_DIM, HEAD_DIM), F32),
        ],
        scratch_shapes=[pltpu.VMEM((nseq, 1, A_PROJ), F32), pltpu.VMEM((nseq, A_WIDTH, A_WIDTH), F32)],
        compiler_params=pltpu.CompilerParams(
            dimension_semantics=("arbitrary",), vmem_limit_bytes=VMEM_LIMIT_BYTES),
        name="rwkv_prompt",
    )(proj, *[prm[name] for name in _RWKV_PARAMS])


def _rwkv_decode_kernel(pa_ref, shift_ref, wkv0_ref, *rest):
    prm_refs = rest[:len(_RWKV_PARAMS)]
    _, ya_ref, wkv_ref, cols_ref, rows_ref, y_ref = rest[len(_RWKV_PARAMS):]
    h = pl.program_id(0)
    p = _load_params(prm_refs)

    @pl.when(h == 0)
    def _():
        pa = pa_ref[...]
        r, k, v, logd, av, bv, g = _rwkv_prep(pa + (shift_ref[...] - pa) * p["mu"], p)
        for idx, x in enumerate((r, k, v, jnp.exp(logd), av, bv)):
            cols_ref[idx] = x.T
        for idx, x in enumerate((r, k, v, g)):
            rows_ref[idx] = x

    hs = pl.ds(pl.multiple_of(h * HEAD_DIM, HEAD_DIM), HEAD_DIM)
    r_h, k_h, v_h, d_h, a_h, b_h = [cols_ref[idx, hs, :] for idx in range(6)]
    ys = []
    for vi in range(HEAD_DIM):
        s = wkv0_ref[vi]
        u = jnp.sum(s * a_h, axis=0, keepdims=True)
        s = s * d_h + u * b_h + v_h[vi:vi + 1, :] * k_h
        wkv_ref[vi] = s
        ys.append(jnp.sum(s * r_h, axis=0, keepdims=True))
    y_ref[hs, :] = jnp.concatenate(ys, axis=0)

    @pl.when(h == A_HEADS - 1)
    def _():
        ya_ref[...] = _rwkv_post(y_ref[...].T, rows_ref[0], rows_ref[1], rows_ref[2], rows_ref[3], p)


def _rwkv_decode(proj, shift_prev, wkv0_t, prm, layer, wkv_all_t):
    nd = proj.shape[0]
    state_spec = pl.BlockSpec((None, None, HEAD_DIM, HEAD_DIM, nd), lambda h: (layer, h, 0, 0, 0))
    n_in = 3 + len(_RWKV_PARAMS)
    return pl.pallas_call(
        _rwkv_decode_kernel,
        grid=(A_HEADS,),
        in_specs=[
            pl.BlockSpec((nd, A_PROJ), lambda h: (0, 0)),
            pl.BlockSpec((None, nd, A_PROJ), lambda h: (layer, 0, 0)),
            state_spec,
        ] + _rwkv_param_specs(layer) + [pl.BlockSpec(memory_space=pl.ANY)],
        out_specs=[pl.BlockSpec((nd, A_WIDTH), lambda h: (0, 0)), state_spec],
        out_shape=[jax.ShapeDtypeStruct((nd, A_WIDTH), F32), jax.ShapeDtypeStruct(wkv_all_t.shape, F32)],
        input_output_aliases={n_in: 1},
        scratch_shapes=[pltpu.VMEM((6, A_WIDTH, nd), F32), pltpu.VMEM((4, nd, A_WIDTH), F32),
                        pltpu.VMEM((A_WIDTH, nd), F32)],
        compiler_params=pltpu.CompilerParams(
            dimension_semantics=("arbitrary",), vmem_limit_bytes=VMEM_LIMIT_BYTES),
        name="rwkv_decode",
    )(proj, shift_prev, wkv0_t, *[prm[name] for name in _RWKV_PARAMS], wkv_all_t)


def _place_q_heads(qn):
    lane_half = _iota((qn.shape[0], B_KV_WIDTH), 1) // HEAD_DIM
    placed = []
    for h in range(B_HEADS):
        kh, g = divmod(h, B_GROUP)
        chunk = qn[:, kh * B_KV_WIDTH:(kh + 1) * B_KV_WIDTH]
        if g != kh:
            chunk = pltpu.roll(chunk, HEAD_DIM, axis=1)
        placed.append(jnp.where(lane_half == kh, chunk, 0.0))
    return placed


def _gather_o_heads(o_heads):
    lane_half = _iota(o_heads[0].shape, 1) // HEAD_DIM
    chunks = []
    for kh in range(B_KV_HEADS):
        parts = []
        for g in range(B_GROUP):
            o = o_heads[kh * B_GROUP + g]
            parts.append(o if g == kh else pltpu.roll(o, HEAD_DIM, axis=1))
        chunks.append(jnp.where(lane_half == 0, parts[0], parts[1]))
    return jnp.concatenate(chunks, axis=1)


def _swa_prompt_kernel(layer, sinks_ref, pb_ref, gq_ref, gk_ref, yb_ref, nk_ref, nv_ref, kprev_ref, vprev_ref):
    j = pl.program_id(1)
    tq = pb_ref.shape[0]
    blk = SWA_BLOCK

    @pl.when(j == 0)
    def _():
        kprev_ref[...] = jnp.zeros_like(kprev_ref)
        vprev_ref[...] = jnp.zeros_like(vprev_ref)

    scale = HEAD_DIM ** -0.5
    rows = _iota((B_GROUP * blk, 2 * blk), 0)
    cols = _iota((B_GROUP * blk, 2 * blk), 1)
    qi = rows % blk
    band = (cols >= qi) & (cols <= qi + WINDOW)
    n_sub = tq // blk
    x = pb_ref[...]
    qn = _head_rms_norm(x[:, 0:B_WIDTH], gq_ref[...])
    kn = _head_rms_norm(x[:, B_WIDTH:B_WIDTH + B_KV_WIDTH], gk_ref[...])
    v = x[:, B_WIDTH + B_KV_WIDTH:B_PROJ]
    kfull = jnp.concatenate([kprev_ref[...], kn], axis=0)
    vfull = jnp.concatenate([vprev_ref[...], v], axis=0)
    placed = _place_q_heads(qn)

    def attend(sb, kh):
        q2 = jnp.concatenate([pl_h[sb * blk:(sb + 1) * blk] for pl_h in placed[kh * B_GROUP:(kh + 1) * B_GROUP]],
                             axis=0)
        s = _mm_nt(q2, kfull[sb * blk:(sb + 2) * blk])
        yield
        first_key = jnp.where(j * n_sub + sb > 0, 0, blk)
        s = jnp.where(band & (cols >= first_key), s * scale, -jnp.inf)
        sink = jnp.where(rows[:, 0:1] < blk, sinks_ref[layer, kh * B_GROUP], sinks_ref[layer, kh * B_GROUP + 1])
        m = jnp.maximum(jnp.max(s, axis=-1, keepdims=True), sink)
        pr = jnp.exp(s - m)
        den = jnp.sum(pr, axis=-1, keepdims=True) + jnp.exp(sink - m)
        o = _mm(pr, vfull[sb * blk:(sb + 2) * blk])
        yield
        return o / den

    outs = _interleave([attend(sb, kh) for sb in range(n_sub) for kh in range(B_KV_HEADS)])
    for sb in range(n_sub):
        heads = [outs[sb * B_KV_HEADS + kh][g * blk:(g + 1) * blk]
                 for kh in range(B_KV_HEADS) for g in range(B_GROUP)]
        yb_ref[sb * blk:(sb + 1) * blk, :] = _gather_o_heads(heads)
    kprev_ref[...] = kn[tq - blk:tq]
    vprev_ref[...] = v[tq - blk:tq]
    nk_ref[...] = kn[tq - blk:tq]
    nv_ref[...] = v[tq - blk:tq]


def _swa_prompt(proj, gq, gk, sinks, layer, tq):
    nseq, t, _ = proj.shape
    col = A_PROJ // B_PROJ
    return pl.pallas_call(
        functools.partial(_swa_prompt_kernel, layer),
        grid=(nseq, t // tq),
        in_specs=[
            pl.BlockSpec(memory_space=pltpu.SMEM),
            pl.BlockSpec((None, tq, B_PROJ), lambda n, j: (n, j, col)),
            _layer_spec(layer, (1, B_WIDTH)),
            _layer_spec(layer, (1, B_KV_WIDTH)),
        ],
        out_specs=[
            pl.BlockSpec((None, tq, B_WIDTH), lambda n, j: (n, j, 0)),
            pl.BlockSpec((None, WINDOW, B_KV_WIDTH), lambda n, j: (n, 0, 0)),
            pl.BlockSpec((None, WINDOW, B_KV_WIDTH), lambda n, j: (n, 0, 0)),
        ],
        out_shape=[
            jax.ShapeDtypeStruct((nseq, t, B_WIDTH), F32),
            jax.ShapeDtypeStruct((nseq, WINDOW, B_KV_WIDTH), F32),
            jax.ShapeDtypeStruct((nseq, WINDOW, B_KV_WIDTH), F32),
        ],
        scratch_shapes=[pltpu.VMEM((SWA_BLOCK, B_KV_WIDTH), F32), pltpu.VMEM((SWA_BLOCK, B_KV_WIDTH), F32)],
        compiler_params=pltpu.CompilerParams(
            dimension_semantics=("arbitrary", "arbitrary"), vmem_limit_bytes=VMEM_LIMIT_BYTES),
        name="swa_prompt",
    )(sinks, proj, gq, gk)


def _swa_decode_kernel(layer, sinks_ref, pb_ref, ck_ref, cv_ref, gq_ref, gk_ref, _k_all, _v_all,
                       yb_ref, nk_ref, nv_ref):
    x = pb_ref[...]
    nb = x.shape[0]
    qn = _head_rms_norm(x[:, 0:B_WIDTH], gq_ref[...])
    kn = _head_rms_norm(x[:, B_WIDTH:B_WIDTH + B_KV_WIDTH], gk_ref[...])
    v = x[:, B_WIDTH + B_KV_WIDTH:B_PROJ]
    placed = _place_q_heads(qn)
    scale = HEAD_DIM ** -0.5
    head = _iota((B_HEADS, 1), 0)
    sink = jnp.zeros((B_HEADS, 1), F32)
    for h in range(B_HEADS):
        sink = jnp.where(head == h, sinks_ref[layer, h], sink)
    last_key = _iota((B_KV_WIDTH, WINDOW), 1) == WINDOW - 1
    k_cols = _column_broadcasts(kn, WINDOW)
    v_cols = _column_broadcasts(v, WINDOW)

    def attend(j):
        kc = ck_ref[j]
        vc = cv_ref[j]
        group = slice(j * WINDOW, (j + 1) * WINDOW)
        q4 = jnp.concatenate([pl_h[j:j + 1] for pl_h in placed], axis=0)
        s = _mm(q4, kc) * scale
        nk_ref[j] = jnp.where(last_key, k_cols[:, group], pltpu.roll(kc, WINDOW - 1, axis=1))
        nv_ref[j] = jnp.where(last_key, v_cols[:, group], pltpu.roll(vc, WINDOW - 1, axis=1))
        yield
        s_new = jnp.sum(q4 * kn[j:j + 1], axis=-1, keepdims=True) * scale
        m = jnp.maximum(jnp.maximum(jnp.max(s, axis=-1, keepdims=True), s_new), sink)
        pr = jnp.exp(s - m)
        p_new = jnp.exp(s_new - m)
        den = jnp.sum(pr, axis=-1, keepdims=True) + p_new + jnp.exp(sink - m)
        o = _mm_nt(pr, vc)
        yield
        return (o + p_new * v[j:j + 1]) / den

    o_rows = _interleave([attend(j) for j in range(nb)])
    outs = [jnp.concatenate([o[h:h + 1] for o in o_rows], axis=0) for h in range(B_HEADS)]
    yb_ref[...] = _gather_o_heads(outs)


def _swa_decode(proj, cache_k, cache_v, gq, gk, sinks, layer, k_all, v_all):
    nd = proj.shape[0]
    nb = DEC_BLOCK
    col = A_PROJ // B_PROJ
    cache_spec = pl.BlockSpec((None, nb, WINDOW, B_KV_WIDTH), lambda i: (layer, i, 0, 0))
    any_spec = pl.BlockSpec(memory_space=pl.ANY)
    return pl.pallas_call(
        functools.partial(_swa_decode_kernel, layer),
        grid=(nd // nb,),
        in_specs=[
            pl.BlockSpec(memory_space=pltpu.SMEM),
            pl.BlockSpec((nb, B_PROJ), lambda i: (i, col)),
            cache_spec, cache_spec,
            _layer_spec(layer, (1, B_WIDTH)),
            _layer_spec(layer, (1, B_KV_WIDTH)),
            any_spec, any_spec,
        ],
        out_specs=[pl.BlockSpec((nb, B_WIDTH), lambda i: (i, 0)), cache_spec, cache_spec],
        out_shape=[
            jax.ShapeDtypeStruct((nd, B_WIDTH), F32),
            jax.ShapeDtypeStruct(k_all.shape, F32),
            jax.ShapeDtypeStruct(v_all.shape, F32),
        ],
        input_output_aliases={6: 1, 7: 2},
        compiler_params=pltpu.CompilerParams(
            dimension_semantics=("arbitrary",), vmem_limit_bytes=VMEM_LIMIT_BYTES),
        name="swa_decode",
    )(sinks, proj, cache_k, cache_v, gq, gk, k_all, v_all)


def _head_expand(width_per_head):
    n = C_HEADS * width_per_head
    return (_iota((LANES, n), 0) == _iota((LANES, n), 1) // width_per_head).astype(F32)


def _ssd_prompt_kernel(z_ref, xbc_ref, dt_ref, cw_ref, cb_ref, dtb_ref, aneg_ref, dskip_ref, ng_ref,
                       yc_ref, tail_ref, hout_ref, xpad_ref, h_ref):
    j = pl.program_id(1)
    tb = xbc_ref.shape[0]

    @pl.when(j == 0)
    def _():
        xpad_ref[0:SUBLANES, :] = jnp.zeros((SUBLANES, CONV_DIM), F32)
        h_ref[...] = jnp.zeros_like(h_ref)

    xpad_ref[SUBLANES:SUBLANES + tb, :] = xbc_ref[...]
    cw = cw_ref[...]
    conv = cb_ref[...]
    for k in range(SSM_CONV):
        conv = conv + cw[SSM_CONV - 1 - k:SSM_CONV - k] * xpad_ref[SUBLANES - k:SUBLANES - k + tb, :]
    last_rows = xbc_ref[tb - SUBLANES:tb, :]
    tail_ref[...] = last_rows
    xpad_ref[0:SUBLANES, :] = last_rows
    act = _silu(conv)
    x = act[:, 0:C_WIDTH]
    bm = act[:, C_WIDTH:C_WIDTH + C_GROUPS * D_STATE]
    cm = act[:, C_WIDTH + C_GROUPS * D_STATE:CONV_DIM]
    dt = _softplus(dt_ref[...] + dtb_ref[...])
    xdt = x * _mm_sel_right(dt, _head_expand(HEAD_DIM))

    ch = SSD_CHUNK
    n_chunks = tb // ch
    ri = _iota((tb, tb), 0)
    ci = _iota((tb, tb), 1)
    chunk_tril = ((ri // ch == ci // ch) & (ri >= ci)).astype(F32)
    acum = _mm_sel_left(chunk_tril, dt * aneg_ref[...])
    acum_x = _mm_sel_right(acum, _head_expand(D_STATE))
    causal = _iota((ch, ch), 0) >= _iota((ch, ch), 1)
    lane_lo = _iota((ch, 2 * HEAD_DIM), 1) < HEAD_DIM
    row_lo = _iota((2 * HEAD_DIM, D_STATE), 0) < HEAD_DIM
    heads_per_group = C_HEADS // C_GROUPS
    n_pairs = C_HEADS // 2
    head_lanes = lambda e: slice(e * D_STATE, (e + 1) * D_STATE)
    pair_lanes = lambda pr: slice(pr * 2 * HEAD_DIM, (pr + 1) * 2 * HEAD_DIM)
    group_lanes = lambda pr: slice(((2 * pr) // heads_per_group) * D_STATE, ((2 * pr) // heads_per_group + 1) * D_STATE)

    def chunk_local(c):
        rows = slice(c * ch, (c + 1) * ch)
        acx = acum_x[rows]
        ac_t = acum[rows].T
        last = acx[ch - 1:ch]
        decay_to_end = jnp.exp(last - acx)
        cbs = [_mm_nt(cm[rows, g * D_STATE:(g + 1) * D_STATE], bm[rows, g * D_STATE:(g + 1) * D_STATE])
               for g in range(C_GROUPS)]
        yield
        y_diag, states = [], []
        for pr in range(n_pairs):
            e0, e1 = 2 * pr, 2 * pr + 1
            xp = xdt[rows, pair_lanes(pr)]
            yd = []
            for e in (e0, e1):
                seg = acx[:, head_lanes(e)] - ac_t[e:e + 1, :]
                lmat = jnp.exp(jnp.where(causal, seg, -jnp.inf))
                yd.append(_mm(cbs[(2 * pr) // heads_per_group] * lmat, xp))
            y_diag.append(jnp.where(lane_lo, yd[0], yd[1]))
            scale = jnp.where(lane_lo, decay_to_end[:, head_lanes(e0)], decay_to_end[:, head_lanes(e1)])
            states.append(_mm_tn(xp * scale, bm[rows, group_lanes(pr)]))
        yield
        chunk_decay = jnp.exp(last)
        cds = [jnp.where(row_lo, chunk_decay[:, head_lanes(2 * pr)], chunk_decay[:, head_lanes(2 * pr + 1)])
               for pr in range(n_pairs)]
        return y_diag, states, cds, jnp.exp(acx)

    local = _interleave([chunk_local(c) for c in range(n_chunks)])
    h = [h_ref[pair_lanes(pr), :] for pr in range(n_pairs)]
    y_chunks = []
    for c in range(n_chunks):
        rows = slice(c * ch, (c + 1) * ch)
        y_diag, states, cds, ea = local[c]
        pairs = []
        for pr in range(n_pairs):
            y_off = _mm_nt(cm[rows, group_lanes(pr)], h[pr])
            pairs.append(y_diag[pr] + y_off * jnp.where(lane_lo, ea[:, head_lanes(2 * pr)], ea[:, head_lanes(2 * pr + 1)]))
            h[pr] = h[pr] * cds[pr] + states[pr]
        y_chunks.append(jnp.concatenate(pairs, axis=1))
    for pr in range(n_pairs):
        h_ref[pair_lanes(pr), :] = h[pr]
        hout_ref[pair_lanes(pr), :] = h[pr]

    y = (jnp.concatenate(y_chunks, axis=0) + dskip_ref[...] * x) * _silu(z_ref[...])
    yc_ref[...] = _rms_norm(y, ng_ref[...])


def _ssd_prompt(proj, prm, layer, tb):
    nseq, t, _ = proj.shape
    return pl.pallas_call(
        _ssd_prompt_kernel,
        grid=(nseq, t // tb),
        in_specs=[
            pl.BlockSpec((None, tb, C_WIDTH), lambda n, j: (n, j, Z_OFF // C_WIDTH)),
            pl.BlockSpec((None, tb, CONV_DIM), lambda n, j: (n, j, XBC_OFF // CONV_DIM)),
            pl.BlockSpec((None, tb, LANES), lambda n, j: (n, j, DT_OFF // LANES)),
            _layer_spec(layer, (SSM_CONV, CONV_DIM)),
            _layer_spec(layer, (1, CONV_DIM)),
            _layer_spec(layer, (1, LANES)),
            _layer_spec(layer, (1, LANES)),
            _layer_spec(layer, (1, C_WIDTH)),
            _layer_spec(layer, (1, C_WIDTH)),
        ],
        out_specs=[
            pl.BlockSpec((None, tb, C_WIDTH), lambda n, j: (n, j, 0)),
            pl.BlockSpec((None, SUBLANES, CONV_DIM), lambda n, j: (n, 0, 0)),
            pl.BlockSpec((None, C_WIDTH, D_STATE), lambda n, j: (n, 0, 0)),
        ],
        out_shape=[
            jax.ShapeDtypeStruct((nseq, t, C_WIDTH), F32),
            jax.ShapeDtypeStruct((nseq, SUBLANES, CONV_DIM), F32),
            jax.ShapeDtypeStruct((nseq, C_WIDTH, D_STATE), F32),
        ],
        scratch_shapes=[
            pltpu.VMEM((SUBLANES + tb, CONV_DIM), F32),
            pltpu.VMEM((C_WIDTH, D_STATE), F32),
        ],
        compiler_params=pltpu.CompilerParams(
            dimension_semantics=("arbitrary", "arbitrary"), vmem_limit_bytes=VMEM_LIMIT_BYTES),
        name="ssd_prompt",
    )(proj, proj, proj, prm["conv_w"], prm["conv_b"], prm["dt_bias"], prm["a_neg"], prm["d_skip"], prm["norm_g"])


def _ssd_decode_kernel(z_ref, xbc_ref, dt_ref, buf_ref, h0_ref, cw_ref, cb_ref, dtb_ref, aneg_ref, dskip_ref, ng_ref,
                       _h_all, yc_ref, nbuf_ref, hout_ref):
    xbc = xbc_ref[...]
    nb = xbc.shape[0]
    cw = cw_ref[...]
    conv = cb_ref[...] + cw[SSM_CONV - 1:SSM_CONV] * xbc
    for k in range(SSM_CONV - 1):
        conv = conv + cw[k:k + 1] * buf_ref[k]
        nbuf_ref[k] = buf_ref[k + 1] if k + 1 < SSM_CONV - 1 else xbc
    act = _silu(conv)
    x = act[:, 0:C_WIDTH]
    bm = act[:, C_WIDTH:C_WIDTH + C_GROUPS * D_STATE]
    cm = act[:, C_WIDTH + C_GROUPS * D_STATE:CONV_DIM]
    dt = _softplus(dt_ref[...] + dtb_ref[...])
    decay = _mm_sel_right(jnp.exp(dt * aneg_ref[...]), _head_expand(D_STATE))
    xdt = x * _mm_sel_right(dt, _head_expand(HEAD_DIM))
    heads_per_group = C_HEADS // C_GROUPS
    x_cols = _column_broadcasts(xdt, D_STATE)
    hc = []
    for j in range(nb):
        for e in range(C_HEADS):
            gs = slice((e // heads_per_group) * D_STATE, (e // heads_per_group + 1) * D_STATE)
            hs = slice(e * HEAD_DIM, (e + 1) * HEAD_DIM)
            h = (h0_ref[j, hs, :] * decay[j:j + 1, e * D_STATE:(e + 1) * D_STATE]
                 + x_cols[hs, j * D_STATE:(j + 1) * D_STATE] * bm[j:j + 1, gs])
            hout_ref[j, hs, :] = h
            hc.append(h * cm[j:j + 1, gs])
    y_cols = _row_sums(hc, D_STATE)
    lane = _iota((C_WIDTH, nb), 1)
    y_t = jnp.zeros((C_WIDTH, nb), F32)
    for j in range(nb):
        y_t = jnp.where(lane == j, y_cols[j * C_WIDTH:(j + 1) * C_WIDTH, 0:nb], y_t)
    y = _mm_nt_sel_left(_eye(nb), y_t)
    y = (y + dskip_ref[...] * x) * _silu(z_ref[...])
    yc_ref[...] = _rms_norm(y, ng_ref[...])


def _ssd_decode(proj, conv_buf_t, h0, prm, layer, h_all):
    nd = proj.shape[0]
    nb = 2 * DEC_BLOCK if nd % (2 * DEC_BLOCK) == 0 else DEC_BLOCK
    state_spec = pl.BlockSpec((None, nb, C_WIDTH, D_STATE), lambda i: (layer, i, 0, 0))
    return pl.pallas_call(
        _ssd_decode_kernel,
        grid=(nd // nb,),
        in_specs=[
            pl.BlockSpec((nb, C_WIDTH), lambda i: (i, Z_OFF // C_WIDTH)),
            pl.BlockSpec((nb, CONV_DIM), lambda i: (i, XBC_OFF // CONV_DIM)),
            pl.BlockSpec((nb, LANES), lambda i: (i, DT_OFF // LANES)),
            pl.BlockSpec((None, SSM_CONV - 1, nb, CONV_DIM), lambda i: (layer, 0, i, 0)),
            state_spec,
            _layer_spec(layer, (SSM_CONV, CONV_DIM)),
            _layer_spec(layer, (1, CONV_DIM)),
            _layer_spec(layer, (1, LANES)),
            _layer_spec(layer, (1, LANES)),
            _layer_spec(layer, (1, C_WIDTH)),
            _layer_spec(layer, (1, C_WIDTH)),
            pl.BlockSpec(memory_space=pl.ANY),
        ],
        out_specs=[
            pl.BlockSpec((nb, C_WIDTH), lambda i: (i, 0)),
            pl.BlockSpec((SSM_CONV - 1, nb, CONV_DIM), lambda i: (0, i, 0)),
            state_spec,
        ],
        out_shape=[
            jax.ShapeDtypeStruct((nd, C_WIDTH), F32),
            jax.ShapeDtypeStruct((SSM_CONV - 1, nd, CONV_DIM), F32),
            jax.ShapeDtypeStruct(h_all.shape, F32),
        ],
        input_output_aliases={11: 2},
        compiler_params=pltpu.CompilerParams(
            dimension_semantics=("arbitrary",), vmem_limit_bytes=VMEM_LIMIT_BYTES),
        name="ssd_decode",
    )(proj, proj, proj, conv_buf_t, h0, prm["conv_w"], prm["conv_b"], prm["dt_bias"], prm["a_neg"],
      prm["d_skip"], prm["norm_g"], h_all)


def _out_ffn_core(x, mix, g1, sh2, sc2, g2, gn, wo_ref, wu_ref, cw_ref, cb_ref, wd_ref, conv_fn):
    x1 = x + g1 * _mm(mix, wo_ref[...])
    h = (_rms_norm(x1, gn) * (1.0 + sc2) + sh2).astype(BF16)
    acc = jnp.zeros(x.shape, F32)
    for c in range(D_FF // FF_CHUNK):
        cs = slice(c * FF_CHUNK, (c + 1) * FF_CHUNK)
        gate = jnp.dot(h, wu_ref[:, cs], preferred_element_type=F32)
        val = jnp.dot(h, wu_ref[:, D_FF + c * FF_CHUNK:D_FF + (c + 1) * FF_CHUNK], preferred_element_type=F32)
        gate = conv_fn(gate, cs, cw_ref[:, cs], cb_ref[:, cs])
        acc = acc + _mm(_silu(gate) * val, wd_ref[cs, :])
    return x1 + g2 * acc


def _out_ffn_prompt_kernel(tiles_per_seq, x_ref, ya_ref, yb_ref, yc_ref, g1_ref, sh2_ref, sc2_ref, g2_ref, gn_ref,
                           wo_ref, wu_ref, cw_ref, cb_ref, wd_ref, o_ref, tail_ref, carry_ref):
    i = pl.program_id(0)
    tm = x_ref.shape[0]

    @pl.when(i % tiles_per_seq == 0)
    def _():
        carry_ref[...] = jnp.zeros_like(carry_ref)

    def conv_fn(gate, cs, cw, cb):
        tail = carry_ref[:, cs]
        out = cb + cw[FFN_CONV - 1:FFN_CONV] * gate
        for k in range(1, FFN_CONV):
            out = out + cw[FFN_CONV - 1 - k:FFN_CONV - k] * _shift_rows(gate, k, tail)
        last_rows = gate[tm - SUBLANES:tm, :]
        carry_ref[:, cs] = last_rows
        tail_ref[:, cs] = last_rows
        return out

    mix = jnp.concatenate([ya_ref[...], yb_ref[...], yc_ref[...]], axis=1)
    o_ref[...] = _out_ffn_core(x_ref[...], mix, g1_ref[...], sh2_ref[...], sc2_ref[...], g2_ref[...], gn_ref[...],
                               wo_ref, wu_ref, cw_ref, cb_ref, wd_ref, conv_fn)


def _out_ffn_decode_kernel(x_ref, ya_ref, yb_ref, yc_ref, g1_ref, sh2_ref, sc2_ref, g2_ref, gn_ref,
                           wo_ref, wu_ref, cw_ref, cb_ref, wd_ref, buf_ref, o_ref, nbuf_ref):
    def conv_fn(gate, cs, cw, cb):
        out = cb + cw[FFN_CONV - 1:FFN_CONV] * gate
        for k in range(FFN_CONV - 1):
            out = out + cw[k:k + 1] * buf_ref[k, :, cs]
            nbuf_ref[k, :, cs] = buf_ref[k + 1, :, cs] if k + 1 < FFN_CONV - 1 else gate
        return out

    mix = jnp.concatenate([ya_ref[...], yb_ref[...], yc_ref[...]], axis=1)
    o_ref[...] = _out_ffn_core(x_ref[...], mix, g1_ref[...], sh2_ref[...], sc2_ref[...], g2_ref[...], gn_ref[...],
                               wo_ref, wu_ref, cw_ref, cb_ref, wd_ref, conv_fn)


def _out_ffn(x, ya, yb, yc, mod, prm, layer, tm, per_row, tiles_per_seq, conv_buf_t=None):
    rows = x.shape[0]
    nseq = rows // (tm * tiles_per_seq) if not per_row else None
    row_spec = lambda w: pl.BlockSpec((tm, w), lambda i: (i, 0))
    in_specs = [row_spec(D_MODEL), row_spec(A_WIDTH), row_spec(B_WIDTH), row_spec(C_WIDTH)]
    in_specs += [_mod_spec(layer, per_row, tiles_per_seq, c) for c in (2, 3, 4, 5)]
    in_specs += [
        _layer_spec(layer, (1, D_MODEL)),
        _layer_spec(layer, (D_MODEL, D_MODEL), single_buffer=True),
        _layer_spec(layer, (D_MODEL, 2 * D_FF), single_buffer=True),
        _layer_spec(layer, (FFN_CONV, D_FF)),
        _layer_spec(layer, (1, D_FF)),
        _layer_spec(layer, (D_FF, D_MODEL), single_buffer=True),
    ]
    args = [x, ya, yb, yc, mod, mod, mod, mod, prm["norm_ffn_g"], prm["w_out"], prm["w_up"], prm["ffn_conv_w"],
            prm["ffn_conv_b"], prm["w_down"]]
    params = pltpu.CompilerParams(dimension_semantics=("arbitrary",), vmem_limit_bytes=VMEM_LIMIT_BYTES)
    if per_row:
        buf_shape = (FFN_CONV - 1, rows, D_FF)
        return pl.pallas_call(
            _out_ffn_decode_kernel,
            grid=(rows // tm,),
            in_specs=in_specs + [pl.BlockSpec((None,) + buf_shape, lambda i: (layer, 0, i, 0))],
            out_specs=[row_spec(D_MODEL), pl.BlockSpec(buf_shape, lambda i: (0, i, 0))],
            out_shape=[jax.ShapeDtypeStruct((rows, D_MODEL), F32), jax.ShapeDtypeStruct(buf_shape, F32)],
            compiler_params=params,
            name="out_ffn_decode",
        )(*args, conv_buf_t)
    return pl.pallas_call(
        functools.partial(_out_ffn_prompt_kernel, tiles_per_seq),
        grid=(rows // tm,),
        in_specs=in_specs,
        out_specs=[row_spec(D_MODEL), pl.BlockSpec((None, SUBLANES, D_FF), lambda i: (i // tiles_per_seq, 0, 0))],
        out_shape=[jax.ShapeDtypeStruct((rows, D_MODEL), F32), jax.ShapeDtypeStruct((nseq, SUBLANES, D_FF), F32)],
        scratch_shapes=[pltpu.VMEM((SUBLANES, D_FF), F32)],
        compiler_params=params,
        name="out_ffn_prompt",
    )(*args)


def _row_param(p, width=None):
    if width is not None and p.shape[-1] != width:
        p = jnp.pad(p, ((0, 0), (0, width - p.shape[-1])))
    return p.reshape(p.shape[0], 1, p.shape[-1])


def _largest_tile(total, cap, quantum):
    tile = min(total, cap)
    while total % tile or tile % quantum:
        tile -= quantum
    return tile


def kernel(x_prompt, x_sample, c_prompt, c_sample, state_rwkv_shift, state_rwkv_wkv, cache_swa_k, cache_swa_v, state_ssm_conv, state_ssm, state_ffn_conv, ada_w, ada_b, norm_mix_g, norm_ffn_g, w_in, w_out, rwkv_mu, rwkv_w0, rwkv_w2, rwkv_a0, rwkv_a2, rwkv_g2, rwkv_k_k, rwkv_k_a, rwkv_r_k, rwkv_ln_g, rwkv_ln_b, attn_q_norm_g, attn_k_norm_g, attn_sinks, ssm_conv_w, ssm_conv_b, ssm_dt_bias, ssm_a_log, ssm_d, ssm_norm_g, ffn_w_up, ffn_conv_w, ffn_conv_b, ffn_w_down):
    depth = w_in.shape[0]
    nseq, t, _ = x_prompt.shape
    nd = x_sample.shape[0]
    assert x_sample.shape[1] == 1 and nd % DEC_BLOCK == 0 and t % SSD_CHUNK == 0

    tm = _largest_tile(t, 512, SUBLANES)
    tiles_per_seq = t // tm
    tb_rwkv = _largest_tile(t, 4 * RWKV_GROUP * RWKV_CHUNK, RWKV_CHUNK)
    tq_swa = _largest_tile(t, 512, SWA_BLOCK)
    tb_ssd = _largest_tile(t, 512, SSD_CHUNK)

    w_in_p = jnp.pad(w_in, ((0, 0), (0, 0), (0, PROJ_PAD - PROJ))).astype(BF16)
    rwkv_prm = {
        "mu": _row_param(rwkv_mu), "w0": _row_param(rwkv_w0), "a0": _row_param(rwkv_a0),
        "w2": jnp.pad(rwkv_w2, ((0, 0), (0, LANES - A_DECAY_LORA), (0, 0))),
        "a2": jnp.pad(rwkv_a2, ((0, 0), (A_DECAY_LORA, 0), (0, 0))),
        "g2": rwkv_g2,
        "k_k": _row_param(rwkv_k_k), "k_a": _row_param(rwkv_k_a),
        "r_k": _row_param(rwkv_r_k.reshape(depth, A_WIDTH)),
        "ln_g": _row_param(rwkv_ln_g), "ln_b": _row_param(rwkv_ln_b),
    }
    gq = _row_param(jnp.tile(attn_q_norm_g, (1, B_HEADS)))
    gk = _row_param(jnp.tile(attn_k_norm_g, (1, B_KV_HEADS)))
    ssd_prm = {
        "conv_w": ssm_conv_w, "conv_b": _row_param(ssm_conv_b),
        "dt_bias": _row_param(ssm_dt_bias, LANES),
        "a_neg": _row_param(-jnp.exp(ssm_a_log), LANES),
        "d_skip": _row_param(jnp.repeat(ssm_d, HEAD_DIM, axis=1)),
        "norm_g": _row_param(ssm_norm_g),
    }
    ffn_prm = {
        "norm_ffn_g": _row_param(norm_ffn_g), "w_out": w_out.astype(BF16), "w_up": ffn_w_up.astype(BF16),
        "ffn_conv_w": ffn_conv_w, "ffn_conv_b": _row_param(ffn_conv_b), "w_down": ffn_w_down.astype(BF16),
    }
    norm_mix = _row_param(norm_mix_g)

    c_all = jnp.concatenate([jnp.pad(c_prompt, ((0, SUBLANES - nseq), (0, 0))), c_sample], axis=0)
    mod = _ada(c_all, ada_w, ada_b)
    mod_p = mod[:, 0:nseq].reshape(depth, nseq, 1, N_MOD * D_MODEL)
    mod_s = mod[:, SUBLANES:]

    to_keys_minor = lambda c: jnp.transpose(c, (0, 1, 3, 4, 2)).reshape(depth, nd, B_KV_WIDTH, WINDOW)
    from_keys_minor = lambda c: jnp.transpose(c.reshape(depth, nd, B_KV_HEADS, HEAD_DIM, WINDOW), (0, 1, 4, 2, 3))
    cache_k = to_keys_minor(cache_swa_k)
    cache_v = to_keys_minor(cache_swa_v)
    wkv0_t = jnp.transpose(state_rwkv_wkv, (0, 2, 3, 4, 1))
    ssm_conv_t = jnp.swapaxes(state_ssm_conv, 1, 2)
    ssm_h0 = state_ssm.reshape(depth, nd, C_WIDTH, D_STATE)
    ffn_conv_t = jnp.swapaxes(state_ffn_conv, 1, 2)

    xp = x_prompt.reshape(nseq * t, D_MODEL)
    xs = x_sample.reshape(nd, D_MODEL)
    prompt_new = [[] for _ in range(7)]
    sample_new = [[] for _ in range(3)]
    s_wkv = jnp.zeros(wkv0_t.shape, F32)
    s_k = jnp.zeros(cache_k.shape, F32)
    s_v = jnp.zeros(cache_v.shape, F32)
    s_ssm = jnp.zeros(ssm_h0.shape, F32)
    for l in range(depth):
        proj = _in_proj(xp, mod_p, norm_mix, w_in_p, l, tm, 0, tiles_per_seq).reshape(nseq, t, PROJ_PAD)
        ya, p_shift, p_wkv = _rwkv_prompt(proj, rwkv_prm, l, tb_rwkv)
        yb, p_k, p_v = _swa_prompt(proj, gq, gk, attn_sinks, l, tq_swa)
        yc, p_conv, p_ssm = _ssd_prompt(proj, ssd_prm, l, tb_ssd)
        xp, p_ffn = _out_ffn(xp, ya.reshape(nseq * t, A_WIDTH), yb.reshape(nseq * t, B_WIDTH),
                             yc.reshape(nseq * t, C_WIDTH), mod_p, ffn_prm, l, tm, 0, tiles_per_seq)
        new_p = (p_shift.reshape(nseq, A_PROJ), p_wkv,
                 p_k.reshape(nseq, WINDOW, B_KV_HEADS, HEAD_DIM), p_v.reshape(nseq, WINDOW, B_KV_HEADS, HEAD_DIM),
                 p_conv[:, SUBLANES - (SSM_CONV - 1):], p_ssm.reshape(nseq, C_HEADS, HEAD_DIM, D_STATE),
                 p_ffn[:, SUBLANES - (FFN_CONV - 1):])
        proj_s = _in_proj(xs, mod_s, norm_mix, w_in_p, l, nd, nd, 1)
        ya, s_wkv = _rwkv_decode(proj_s, state_rwkv_shift, wkv0_t, rwkv_prm, l, s_wkv)
        yb, s_k, s_v = _swa_decode(proj_s, cache_k, cache_v, gq, gk, attn_sinks, l, s_k, s_v)
        yc, s_conv_t, s_ssm = _ssd_decode(proj_s, ssm_conv_t, ssm_h0, ssd_prm, l, s_ssm)
        xs, s_ffn_t = _out_ffn(xs, ya, yb, yc, mod_s, ffn_prm, l, nd, nd, 1, ffn_conv_t)
        new_s = (proj_s[:, 0:A_PROJ], jnp.swapaxes(s_conv_t, 0, 1), jnp.swapaxes(s_ffn_t, 0, 1))
        for lst, arr in zip(prompt_new, new_p):
            lst.append(arr)
        for lst, arr in zip(sample_new, new_s):
            lst.append(arr)
    outs_p = [jnp.stack(lst) for lst in prompt_new]
    s_shift, s_conv, s_ffn = [jnp.stack(lst) for lst in sample_new]
    outs_s = (s_shift, jnp.transpose(s_wkv, (0, 4, 1, 2, 3)), from_keys_minor(s_k), from_keys_minor(s_v), s_conv,
              s_ssm.reshape(depth, nd, C_HEADS, HEAD_DIM, D_STATE), s_ffn)
    return (xp.reshape(nseq, t, D_MODEL), xs.reshape(nd, 1, D_MODEL), *outs_p, *outs_s)
```

```python
import functools

import jax
import jax.numpy as jnp
from jax import lax
from jax.experimental import pallas as pl
from jax.experimental.pallas import tpu as pltpu

F32 = jnp.float32
BF16 = jnp.bfloat16

D_MODEL = 1024
HEAD_DIM = 64
A_WIDTH = 256
A_HEADS = A_WIDTH // HEAD_DIM
A_DECAY_LORA = 64
A_AAA_LORA = 64
A_GATE_LORA = 128
A_PROJ = 3 * A_WIDTH + A_DECAY_LORA + A_AAA_LORA + A_GATE_LORA
B_WIDTH = 256
B_HEADS = B_WIDTH // HEAD_DIM
B_KV_HEADS = 2
B_GROUP = B_HEADS // B_KV_HEADS
B_KV_WIDTH = B_KV_HEADS * HEAD_DIM
B_PROJ = B_WIDTH + 2 * B_KV_WIDTH
WINDOW = 128
C_WIDTH = 512
C_HEADS = C_WIDTH // HEAD_DIM
C_GROUPS = 2
D_STATE = 128
SSM_CONV = 4
CONV_DIM = C_WIDTH + 2 * C_GROUPS * D_STATE
C_PROJ = C_WIDTH + CONV_DIM + C_HEADS
PROJ = A_PROJ + B_PROJ + C_PROJ
D_FF = 2816
FFN_CONV = 3
NORM_EPS = 1e-6
GN_EPS = 64e-5
N_MOD = 6

LANES = 128
SUBLANES = 8
VMEM_LIMIT_BYTES = 56 * 1024 * 1024

DT_OFF = A_PROJ + B_PROJ + C_WIDTH + CONV_DIM
PROJ_PAD = DT_OFF + LANES
Z_OFF = A_PROJ + B_PROJ
XBC_OFF = Z_OFF + C_WIDTH

RWKV_CHUNK = 64
INV_BLOCK = 4
RWKV_GROUP = 4
SSD_CHUNK = 128
SWA_BLOCK = 128
DEC_BLOCK = 8
FF_CHUNK = D_FF // 2


def _mm(a, b):
    return jnp.dot(a.astype(BF16), b.astype(BF16), preferred_element_type=F32)


def _mm_nt(a, b):
    return lax.dot_general(a.astype(BF16), b.astype(BF16), (((1,), (1,)), ((), ())), preferred_element_type=F32)


def _mm_tn(a, b):
    return lax.dot_general(a.astype(BF16), b.astype(BF16), (((0,), (0,)), ((), ())), preferred_element_type=F32)


def _split3(x):
    hi = x.astype(BF16)
    r1 = x - hi.astype(F32)
    mid = r1.astype(BF16)
    lo = (r1 - mid.astype(F32)).astype(BF16)
    return hi, mid, lo


def _mm_sel_left(sel, x):
    s = sel.astype(BF16)
    return jnp.dot(jnp.concatenate([s, s, s], axis=1), jnp.concatenate(_split3(x), axis=0),
                   preferred_element_type=F32)


def _mm_sel_right(x, sel):
    s = sel.astype(BF16)
    return jnp.dot(jnp.concatenate(_split3(x), axis=1), jnp.concatenate([s, s, s], axis=0),
                   preferred_element_type=F32)


def _mm_nt_sel_left(sel, x):
    s = sel.astype(BF16)
    return lax.dot_general(jnp.concatenate([s, s, s], axis=1), jnp.concatenate(_split3(x), axis=1),
                           (((1,), (1,)), ((), ())), preferred_element_type=F32)


def _row_sums(tiles, width):
    ones = jnp.ones((width, width), BF16)
    half = len(tiles) // 2
    dot = lambda xs: jnp.dot(jnp.concatenate(xs, axis=0).astype(BF16), ones, preferred_element_type=F32)
    return jnp.concatenate([dot(tiles[:half]), dot(tiles[half:])], axis=0)


def _column_broadcasts(x, group):
    nb, w = x.shape
    parts = jnp.concatenate(_split3(x), axis=0)
    parts_t = lax.dot_general(_eye(w).astype(BF16), parts, (((1,), (1,)), ((), ())),
                              preferred_element_type=F32).astype(BF16)
    src = _iota((3 * nb, nb * group), 0) % nb
    dst = _iota((3 * nb, nb * group), 1) // group
    return jnp.dot(parts_t, (src == dst).astype(BF16), preferred_element_type=F32)


def _sigmoid(x):
    return 1.0 / (1.0 + jnp.exp(-x))


def _silu(x):
    return x * _sigmoid(x)


def _softplus(x):
    return jnp.maximum(x, 0.0) + jnp.log(1.0 + jnp.exp(-jnp.abs(x)))


def _iota(shape, dim):
    return lax.broadcasted_iota(jnp.int32, shape, dim)


def _eye(n):
    return (_iota((n, n), 0) == _iota((n, n), 1)).astype(F32)


def _tril_ones(n):
    return (_iota((n, n), 0) >= _iota((n, n), 1)).astype(F32)


def _group_sum(x, group):
    tiles = []
    for t0 in range(0, x.shape[-1], LANES):
        tile = x[:, t0:t0 + LANES]
        gid = _iota(tile.shape, 1) // group
        sums = [jnp.sum(jnp.where(gid == h, tile, 0.0), axis=-1, keepdims=True) for h in range(LANES // group)]
        out = sums[-1]
        for h in range(LANES // group - 1):
            out = jnp.where(gid == h, sums[h], out)
        tiles.append(jnp.broadcast_to(out, tile.shape))
    return jnp.concatenate(tiles, axis=1) if len(tiles) > 1 else tiles[0]


def _rms_norm(x, g):
    return x * lax.rsqrt(jnp.mean(x * x, axis=-1, keepdims=True) + NORM_EPS) * g


def _head_rms_norm(x, g):
    return x * lax.rsqrt(_group_sum(x * x, HEAD_DIM) * (1.0 / HEAD_DIM) + NORM_EPS) * g


def _shift_rows(x, k, tail):
    rolled = pltpu.roll(x, k, axis=0)
    row = _iota((SUBLANES, x.shape[1]), 0)
    top = jnp.where(row < k, pltpu.roll(tail, k, axis=0), rolled[0:SUBLANES])
    return jnp.concatenate([top, rolled[SUBLANES:]], axis=0)


def _ada_kernel(c_ref, w_ref, b_ref, o_ref):
    o_ref[...] = _mm(_silu(c_ref[...]), w_ref[...]) + b_ref[...]


def _ada(c_all, ada_w, ada_b):
    depth = ada_w.shape[0]
    rows = c_all.shape[0]
    width = N_MOD * D_MODEL
    tn = 1536
    return pl.pallas_call(
        _ada_kernel,
        grid=(depth, width // tn),
        in_specs=[
            pl.BlockSpec((rows, D_MODEL), lambda l, j: (0, 0)),
            pl.BlockSpec((None, D_MODEL, tn), lambda l, j: (l, 0, j)),
            pl.BlockSpec((None, 1, tn), lambda l, j: (l, 0, j)),
        ],
        out_specs=pl.BlockSpec((None, rows, tn), lambda l, j: (l, 0, j)),
        out_shape=jax.ShapeDtypeStruct((depth, rows, width), F32),
        compiler_params=pltpu.CompilerParams(
            dimension_semantics=("arbitrary", "arbitrary"), vmem_limit_bytes=VMEM_LIMIT_BYTES),
        name="ada_mod",
    )(c_all, ada_w, ada_b.reshape(depth, 1, width))


def _in_proj_kernel(x_ref, sh_ref, sc_ref, g_ref, w_ref, o_ref):
    h = _rms_norm(x_ref[...], g_ref[...]) * (1.0 + sc_ref[...]) + sh_ref[...]
    o_ref[...] = _mm(h, w_ref[...])


def _mod_spec(layer, per_row, rows_per_seq_tiles, col):
    if per_row:
        return pl.BlockSpec((None, per_row, D_MODEL), lambda i: (layer, 0, col))
    return pl.BlockSpec((None, None, 1, D_MODEL), lambda i: (layer, i // rows_per_seq_tiles, 0, col))


def _layer_spec(layer, tail, single_buffer=False):
    idx = lambda *_: (layer,) + (0,) * len(tail)
    if single_buffer:
        return pl.BlockSpec((None,) + tuple(tail), idx, pipeline_mode=pl.Buffered(1))
    return pl.BlockSpec((None,) + tuple(tail), idx)


def _in_proj(x, mod, norm_g, w_in, layer, tm, per_row, tiles_per_seq):
    rows = x.shape[0]
    return pl.pallas_call(
        _in_proj_kernel,
        grid=(rows // tm,),
        in_specs=[
            pl.BlockSpec((tm, D_MODEL), lambda i: (i, 0)),
            _mod_spec(layer, per_row, tiles_per_seq, 0),
            _mod_spec(layer, per_row, tiles_per_seq, 1),
            _layer_spec(layer, (1, D_MODEL)),
            _layer_spec(layer, (D_MODEL, PROJ_PAD), single_buffer=True),
        ],
        out_specs=pl.BlockSpec((tm, PROJ_PAD), lambda i: (i, 0)),
        out_shape=jax.ShapeDtypeStruct((rows, PROJ_PAD), F32),
        compiler_params=pltpu.CompilerParams(
            dimension_semantics=("arbitrary",), vmem_limit_bytes=VMEM_LIMIT_BYTES),
        name="in_proj",
    )(x, mod, mod, norm_g, w_in)


_RWKV_PARAMS = ("mu", "w0", "w2", "a0", "a2", "g2", "k_k", "k_a", "r_k", "ln_g", "ln_b")


def _rwkv_prep(xs, p):
    r = xs[:, 0:A_WIDTH]
    k = xs[:, A_WIDTH:2 * A_WIDTH]
    v = xs[:, 2 * A_WIDTH:3 * A_WIDTH]
    lora_in = xs[:, 3 * A_WIDTH:3 * A_WIDTH + LANES]
    lg = xs[:, 3 * A_WIDTH + LANES:A_PROJ]
    w_log = -_softplus(-(p["w0"] + _mm(jnp.tanh(lora_in), p["w2"]))) - 0.5
    log_decay = -jnp.exp(w_log)
    a = _sigmoid(p["a0"] + _mm(lora_in, p["a2"]))
    g = _mm(_sigmoid(lg), p["g2"])
    kk = k * p["k_k"]
    kk = kk / jnp.maximum(jnp.sqrt(_group_sum(kk * kk, HEAD_DIM)), 1e-12)
    k = k * (1.0 + (a - 1.0) * p["k_a"])
    return r, k, v, log_decay, -kk, kk * a, g


def _rwkv_post(y, r, k, v, g, p):
    mean = _group_sum(y, HEAD_DIM) * (1.0 / HEAD_DIM)
    c = y - mean
    var = _group_sum(c * c, HEAD_DIM) * (1.0 / HEAD_DIM)
    yn = c * lax.rsqrt(var + GN_EPS) * p["ln_g"] + p["ln_b"]
    bonus = _group_sum(r * k * p["r_k"], HEAD_DIM) * v
    return (yn + bonus) * g


def _block_diag(x, mask01):
    return jnp.concatenate([x] * (x.shape[1] // x.shape[0]), axis=0) * mask01


def _rwkv_chunk_setup(r, k, v, logd, cum, av, bv, cst):
    bd_mask, strict, incl, eye_cat, level_masks = cst
    excl = cum - logd
    cum_last = cum[RWKV_CHUNK - 1:RWKV_CHUNK, :]
    e_neg = jnp.exp(-cum)
    e_end = jnp.exp(cum_last - cum)
    at = (av * jnp.exp(excl)).astype(BF16)
    rt = (r * jnp.exp(cum)).astype(BF16)
    bt = (bv * e_neg).astype(BF16)
    kt = (k * e_neg).astype(BF16)
    vb = v.astype(BF16)
    bd = lambda x: _block_diag(x, bd_mask)

    lhs = jnp.concatenate([at, rt], axis=0)
    rhs = jnp.concatenate([bd(bt), bd(kt)], axis=0)
    amat = lax.dot_general(lhs, rhs, (((1,), (1,)), ((), ())), preferred_element_type=F32)
    yield
    n_ab = jnp.where(strict, amat[0:RWKV_CHUNK, 0:A_WIDTH], 0.0)
    n_ak = jnp.where(strict, amat[0:RWKV_CHUNK, A_WIDTH:], 0.0).astype(BF16)
    n_rb = jnp.where(incl, amat[RWKV_CHUNK:, 0:A_WIDTH], 0.0).astype(BF16)
    n_rk = jnp.where(incl, amat[RWKV_CHUNK:, A_WIDTH:], 0.0).astype(BF16)
    wy0 = jnp.dot(jnp.concatenate([n_ak, n_rk], axis=0), bd(vb), preferred_element_type=F32)
    yield
    inv = yield from _unit_lower_inverse(n_ab, eye_cat, level_masks, bd)
    return dict(
        lhs=lhs, inv=inv, w0=wy0[0:RWKV_CHUNK], y0=wy0[RWKV_CHUNK:], n_rb=n_rb, vb=vb,
        bk_end=jnp.concatenate([(bv * e_end).astype(BF16), (k * e_end).astype(BF16)], axis=0),
        decay_end=jnp.exp(cum_last),
    )


def _rwkv_chunk_apply(c, state, bd_mask):
    bd = lambda x: _block_diag(x, bd_mask)
    both = lax.dot_general(c["lhs"], state.astype(BF16), (((1,), (1,)), ((), ())), preferred_element_type=F32)
    yield
    w = both[0:RWKV_CHUNK] + c["w0"]
    ub = jnp.dot(c["inv"], bd(w.astype(BF16)), preferred_element_type=F32).astype(BF16)
    yield
    y = both[RWKV_CHUNK:] + c["y0"] + jnp.dot(c["n_rb"], bd(ub), preferred_element_type=F32)
    upd = lax.dot_general(jnp.concatenate([ub, c["vb"]], axis=0), c["bk_end"],
                          (((0,), (0,)), ((), ())), preferred_element_type=F32)
    return y, state * c["decay_end"] + upd * bd_mask.astype(F32)


def _interleave(generators):
    results = [None] * len(generators)
    active = list(enumerate(generators))
    while active:
        still = []
        for i, gen in active:
            try:
                next(gen)
                still.append((i, gen))
            except StopIteration as done:
                results[i] = done.value
        active = still
    return results


def _split2(x):
    hi = x.astype(BF16)
    return hi, (x - hi.astype(F32)).astype(BF16)


def _mm_cat2(lhs_list, b, bd):
    rows = lhs_list[0].shape[0]
    width = b.shape[1]
    b_hi, b_lo = _split2(b)
    lhs = jnp.concatenate([part for a in lhs_list for part in _split2(a)], axis=0)
    prod = jnp.dot(lhs, jnp.concatenate([bd(b_hi), bd(b_lo)], axis=1), preferred_element_type=F32)
    outs = []
    for i in range(len(lhs_list)):
        hi = prod[2 * i * rows:(2 * i + 1) * rows]
        lo = prod[(2 * i + 1) * rows:(2 * i + 2) * rows]
        outs.append((lo[:, 0:width] + hi[:, width:]) + hi[:, 0:width])
    return outs


def _unit_lower_inverse(n, eye_cat, level_masks, bd):
    m = jnp.where(level_masks[0], n, 0.0)
    inv = eye_cat + m
    if INV_BLOCK == 4:
        mb = m.astype(BF16)
        m2 = jnp.dot(mb, bd(mb), preferred_element_type=F32)
        yield
        inv = (inv + m2) + jnp.dot(mb, bd(m2.astype(BF16)), preferred_element_type=F32)
        yield
    else:
        (m,) = _mm_cat2([m], m, bd)
        yield
        for step in range(INV_BLOCK.bit_length() - 2):
            if step + 1 < INV_BLOCK.bit_length() - 2:
                delta, m = _mm_cat2([inv, m], m, bd)
            else:
                (delta,) = _mm_cat2([inv], m, bd)
            yield
            inv = inv + delta
    inv = inv.astype(BF16)
    for mask in level_masks[1:]:
        off = jnp.where(mask, n, 0.0).astype(BF16)
        g = jnp.dot(inv, bd(off), preferred_element_type=F32)
        yield
        inv = inv + jnp.dot(g.astype(BF16), bd(inv), preferred_element_type=F32).astype(BF16)
        yield
    return inv


def _rwkv_consts():
    w = A_WIDTH
    bd_mask = ((_iota((w, w), 0) // HEAD_DIM) == (_iota((w, w), 1) // HEAD_DIM)).astype(BF16)
    t = _iota((RWKV_CHUNK, w), 0)
    s = _iota((RWKV_CHUNK, w), 1) % RWKV_CHUNK
    level_masks = [t // INV_BLOCK == s // INV_BLOCK]
    size = INV_BLOCK
    while size < RWKV_CHUNK:
        level_masks.append((t // (2 * size) == s // (2 * size)) & (t // size != s // size))
        size *= 2
    return bd_mask, s < t, s <= t, (s == t).astype(F32), level_masks


def _load_params(refs):
    return {name: ref[...] for name, ref in zip(_RWKV_PARAMS, refs)}


def _rwkv_prompt_kernel(pa_ref, *rest):
    prm_refs = rest[:len(_RWKV_PARAMS)]
    ya_ref, shift_ref, wkv_ref, carry_ref, state_ref = rest[len(_RWKV_PARAMS):]
    i = pl.program_id(0)
    nseq, tb, _ = pa_ref.shape

    @pl.when(i == 0)
    def _():
        carry_ref[...] = jnp.zeros_like(carry_ref)
        state_ref[...] = jnp.zeros_like(state_ref)

    p = _load_params(prm_refs)
    cst = _rwkv_consts()
    n_chunks = tb // RWKV_CHUNK
    group = min(n_chunks, RWKV_GROUP)
    n_groups = n_chunks // group
    slab = group * RWKV_CHUNK
    ri = _iota((slab, slab), 0)
    ci = _iota((slab, slab), 1)
    chunk_tril = ((ri // RWKV_CHUNK == ci // RWKV_CHUNK) & (ri >= ci)).astype(F32)
    first_row = _iota((slab, A_PROJ), 0) == 0

    def prepare(n, gi):
        r0 = gi * slab
        pa = pa_ref[n, r0:r0 + slab, :]
        before = carry_ref[n] if gi == 0 else pa_ref[n, r0 - 1:r0, :]
        prev = jnp.where(first_row, before, pltpu.roll(pa, 1, axis=0))
        r, k, v, logd, av, bv, g = _rwkv_prep(pa + (prev - pa) * p["mu"], p)
        yield
        cum = _mm_sel_left(chunk_tril, logd)
        yield
        return dict(r=r, k=k, v=v, logd=logd, cum=cum, av=av, bv=bv, g=g)

    def setup(s, c):
        rows = slice(c * RWKV_CHUNK, (c + 1) * RWKV_CHUNK)
        return _rwkv_chunk_setup(s["r"][rows], s["k"][rows], s["v"][rows], s["logd"][rows], s["cum"][rows],
                                 s["av"][rows], s["bv"][rows], cst)

    def apply_group(state, group_setups):
        ys = []
        for chunk in group_setups:
            y, state = yield from _rwkv_chunk_apply(chunk, state, cst[0])
            ys.append(y)
        return jnp.concatenate(ys, axis=0), state

    states = [state_ref[n] for n in range(nseq)]
    slabs = {0: _interleave([prepare(n, 0) for n in range(nseq)])}
    setups = {}
    for gi in range(n_groups + 1):
        jobs = []
        if gi >= 1:
            jobs += [apply_group(states[n], setups[gi - 1][n]) for n in range(nseq)]
        if gi < n_groups:
            jobs += [setup(slabs[gi][n], c) for n in range(nseq) for c in range(group)]
        if gi + 1 < n_groups:
            jobs += [prepare(n, gi + 1) for n in range(nseq)]
        done = _interleave(jobs)
        if gi >= 1:
            for n in range(nseq):
                y, states[n] = done[n]
                s = slabs[gi - 1][n]
                ya_ref[n, (gi - 1) * slab:gi * slab, :] = _rwkv_post(y, s["r"], s["k"], s["v"], s["g"], p)
            done = done[nseq:]
        if gi < n_groups:
            setups[gi] = [done[n * group:(n + 1) * group] for n in range(nseq)]
            done = done[nseq * group:]
        if gi + 1 < n_groups:
            slabs[gi + 1] = done
    for n in range(nseq):
        state_ref[n] = states[n]
        last = pa_ref[n, tb - 1:tb, :]
        carry_ref[n] = last
        shift_ref[n] = last

    @pl.when(i == pl.num_programs(0) - 1)
    def _():
        for n in range(nseq):
            for h in range(A_HEADS):
                sl = slice(h * HEAD_DIM, (h + 1) * HEAD_DIM)
                wkv_ref[n, h] = state_ref[n, sl, sl]


def _rwkv_param_specs(layer):
    shapes = {"mu": (1, A_PROJ), "w2": (LANES, A_WIDTH), "a2": (LANES, A_WIDTH), "g2": (A_GATE_LORA, A_WIDTH)}
    return [_layer_spec(layer, shapes.get(name, (1, A_WIDTH))) for name in _RWKV_PARAMS]


def _rwkv_prompt(proj, prm, layer, tb):
    nseq, t, _ = proj.shape
    return pl.pallas_call(
        _rwkv_prompt_kernel,
        grid=(t // tb,),
        in_specs=[pl.BlockSpec((nseq, tb, A_PROJ), lambda i: (0, i, 0))] + _rwkv_param_specs(layer),
        out_specs=[
            pl.BlockSpec((nseq, tb, A_WIDTH), lambda i: (0, i, 0)),
            pl.BlockSpec((nseq, 1, A_PROJ), lambda i: (0, 0, 0)),
            pl.BlockSpec((nseq, A_HEADS, HEAD_DIM, HEAD_DIM), lambda i: (0, 0, 0, 0)),
        ],
        out_shape=[
            jax.ShapeDtypeStruct((nseq, t, A_WIDTH), F32),
            jax.ShapeDtypeStruct((nseq, 1, A_PROJ), F32),
            jax.ShapeDtypeStruct((nseq, A_HEADS, HEAD_DIM, HEAD_DIM), F32),
        ],
        scratch_shapes=[pltpu.VMEM((nseq, 1, A_PROJ), F32), pltpu.VMEM((nseq, A_WIDTH, A_WIDTH), F32)],
        compiler_params=pltpu.CompilerParams(
            dimension_semantics=("arbitrary",), vmem_limit_bytes=VMEM_LIMIT_BYTES),
        name="rwkv_prompt",
    )(proj, *[prm[name] for name in _RWKV_PARAMS])


def _rwkv_decode_kernel(pa_ref, shift_ref, wkv0_ref, *rest):
    prm_refs = rest[:len(_RWKV_PARAMS)]
    _, ya_ref, wkv_ref, cols_ref, rows_ref, y_ref = rest[len(_RWKV_PARAMS):]
    h = pl.program_id(0)
    p = _load_params(prm_refs)

    @pl.when(h == 0)
    def _():
        pa = pa_ref[...]
        r, k, v, logd, av, bv, g = _rwkv_prep(pa + (shift_ref[...] - pa) * p["mu"], p)
        for idx, x in enumerate((r, k, v, jnp.exp(logd), av, bv)):
            cols_ref[idx] = x.T
        for idx, x in enumerate((r, k, v, g)):
            rows_ref[idx] = x

    hs = pl.ds(pl.multiple_of(h * HEAD_DIM, HEAD_DIM), HEAD_DIM)
    r_h, k_h, v_h, d_h, a_h, b_h = [cols_ref[idx, hs, :] for idx in range(6)]
    ys = []
    for vi in range(HEAD_DIM):
        s = wkv0_ref[vi]
        u = jnp.sum(s * a_h, axis=0, keepdims=True)
        s = s * d_h + u * b_h + v_h[vi:vi + 1, :] * k_h
        wkv_ref[vi] = s
        ys.append(jnp.sum(s * r_h, axis=0, keepdims=True))
    y_ref[hs, :] = jnp.concatenate(ys, axis=0)

    @pl.when(h == A_HEADS - 1)
    def _():
        ya_ref[...] = _rwkv_post(y_ref[...].T, rows_ref[0], rows_ref[1], rows_ref[2], rows_ref[3], p)


def _rwkv_decode(proj, shift_prev, wkv0_t, prm, layer, wkv_all_t):
    nd = proj.shape[0]
    state_spec = pl.BlockSpec((None, None, HEAD_DIM, HEAD_DIM, nd), lambda h: (layer, h, 0, 0, 0))
    n_in = 3 + len(_RWKV_PARAMS)
    return pl.pallas_call(
        _rwkv_decode_kernel,
        grid=(A_HEADS,),
        in_specs=[
            pl.BlockSpec((nd, A_PROJ), lambda h: (0, 0)),
            pl.BlockSpec((None, nd, A_PROJ), lambda h: (layer, 0, 0)),
            state_spec,
        ] + _rwkv_param_specs(layer) + [pl.BlockSpec(memory_space=pl.ANY)],
        out_specs=[pl.BlockSpec((nd, A_WIDTH), lambda h: (0, 0)), state_spec],
        out_shape=[jax.ShapeDtypeStruct((nd, A_WIDTH), F32), jax.ShapeDtypeStruct(wkv_all_t.shape, F32)],
        input_output_aliases={n_in: 1},
        scratch_shapes=[pltpu.VMEM((6, A_WIDTH, nd), F32), pltpu.VMEM((4, nd, A_WIDTH), F32),
                        pltpu.VMEM((A_WIDTH, nd), F32)],
        compiler_params=pltpu.CompilerParams(
            dimension_semantics=("arbitrary",), vmem_limit_bytes=VMEM_LIMIT_BYTES),
        name="rwkv_decode",
    )(proj, shift_prev, wkv0_t, *[prm[name] for name in _RWKV_PARAMS], wkv_all_t)


def _place_q_heads(qn):
    lane_half = _iota((qn.shape[0], B_KV_WIDTH), 1) // HEAD_DIM
    placed = []
    for h in range(B_HEADS):
        kh, g = divmod(h, B_GROUP)
        chunk = qn[:, kh * B_KV_WIDTH:(kh + 1) * B_KV_WIDTH]
        if g != kh:
            chunk = pltpu.roll(chunk, HEAD_DIM, axis=1)
        placed.append(jnp.where(lane_half == kh, chunk, 0.0))
    return placed


def _gather_o_heads(o_heads):
    lane_half = _iota(o_heads[0].shape, 1) // HEAD_DIM
    chunks = []
    for kh in range(B_KV_HEADS):
        parts = []
        for g in range(B_GROUP):
            o = o_heads[kh * B_GROUP + g]
            parts.append(o if g == kh else pltpu.roll(o, HEAD_DIM, axis=1))
        chunks.append(jnp.where(lane_half == 0, parts[0], parts[1]))
    return jnp.concatenate(chunks, axis=1)


def _swa_prompt_kernel(layer, sinks_ref, pb_ref, gq_ref, gk_ref, yb_ref, nk_ref, nv_ref, kprev_ref, vprev_ref):
    j = pl.program_id(1)
    tq = pb_ref.shape[0]
    blk = SWA_BLOCK

    @pl.when(j == 0)
    def _():
        kprev_ref[...] = jnp.zeros_like(kprev_ref)
        vprev_ref[...] = jnp.zeros_like(vprev_ref)

    scale = HEAD_DIM ** -0.5
    rows = _iota((B_GROUP * blk, 2 * blk), 0)
    cols = _iota((B_GROUP * blk, 2 * blk), 1)
    qi = rows % blk
    band = (cols >= qi) & (cols <= qi + WINDOW)
    n_sub = tq // blk
    x = pb_ref[...]
    qn = _head_rms_norm(x[:, 0:B_WIDTH], gq_ref[...])
    kn = _head_rms_norm(x[:, B_WIDTH:B_WIDTH + B_KV_WIDTH], gk_ref[...])
    v = x[:, B_WIDTH + B_KV_WIDTH:B_PROJ]
    kfull = jnp.concatenate([kprev_ref[...], kn], axis=0)
    vfull = jnp.concatenate([vprev_ref[...], v], axis=0)
    placed = _place_q_heads(qn)

    def attend(sb, kh):
        q2 = jnp.concatenate([pl_h[sb * blk:(sb + 1) * blk] for pl_h in placed[kh * B_GROUP:(kh + 1) * B_GROUP]],
                             axis=0)
        s = _mm_nt(q2, kfull[sb * blk:(sb + 2) * blk])
        yield
        first_key = jnp.where(j * n_sub + sb > 0, 0, blk)
        s = jnp.where(band & (cols >= first_key), s * scale, -jnp.inf)
        sink = jnp.where(rows[:, 0:1] < blk, sinks_ref[layer, kh * B_GROUP], sinks_ref[layer, kh * B_GROUP + 1])
        m = jnp.maximum(jnp.max(s, axis=-1, keepdims=True), sink)
        pr = jnp.exp(s - m)
        den = jnp.sum(pr, axis=-1, keepdims=True) + jnp.exp(sink - m)
        o = _mm(pr, vfull[sb * blk:(sb + 2) * blk])
        yield
        return o / den

    outs = _interleave([attend(sb, kh) for sb in range(n_sub) for kh in range(B_KV_HEADS)])
    for sb in range(n_sub):
        heads = [outs[sb * B_KV_HEADS + kh][g * blk:(g + 1) * blk]
                 for kh in range(B_KV_HEADS) for g in range(B_GROUP)]
        yb_ref[sb * blk:(sb + 1) * blk, :] = _gather_o_heads(heads)
    kprev_ref[...] = kn[tq - blk:tq]
    vprev_ref[...] = v[tq - blk:tq]
    nk_ref[...] = kn[tq - blk:tq]
    nv_ref[...] = v[tq - blk:tq]


def _swa_prompt(proj, gq, gk, sinks, layer, tq):
    nseq, t, _ = proj.shape
    col = A_PROJ // B_PROJ
    return pl.pallas_call(
        functools.partial(_swa_prompt_kernel, layer),
        grid=(nseq, t // tq),
        in_specs=[
            pl.BlockSpec(memory_space=pltpu.SMEM),
            pl.BlockSpec((None, tq, B_PROJ), lambda n, j: (n, j, col)),
            _layer_spec(layer, (1, B_WIDTH)),
            _layer_spec(layer, (1, B_KV_WIDTH)),
        ],
        out_specs=[
            pl.BlockSpec((None, tq, B_WIDTH), lambda n, j: (n, j, 0)),
            pl.BlockSpec((None, WINDOW, B_KV_WIDTH), lambda n, j: (n, 0, 0)),
            pl.BlockSpec((None, WINDOW, B_KV_WIDTH), lambda n, j: (n, 0, 0)),
        ],
        out_shape=[
            jax.ShapeDtypeStruct((nseq, t, B_WIDTH), F32),
            jax.ShapeDtypeStruct((nseq, WINDOW, B_KV_WIDTH), F32),
            jax.ShapeDtypeStruct((nseq, WINDOW, B_KV_WIDTH), F32),
        ],
        scratch_shapes=[pltpu.VMEM((SWA_BLOCK, B_KV_WIDTH), F32), pltpu.VMEM((SWA_BLOCK, B_KV_WIDTH), F32)],
        compiler_params=pltpu.CompilerParams(
            dimension_semantics=("arbitrary", "arbitrary"), vmem_limit_bytes=VMEM_LIMIT_BYTES),
        name="swa_prompt",
    )(sinks, proj, gq, gk)


def _swa_decode_kernel(layer, sinks_ref, pb_ref, ck_ref, cv_ref, gq_ref, gk_ref, _k_all, _v_all,
                       yb_ref, nk_ref, nv_ref):
    x = pb_ref[...]
    nb = x.shape[0]
    qn = _head_rms_norm(x[:, 0:B_WIDTH], gq_ref[...])
    kn = _head_rms_norm(x[:, B_WIDTH:B_WIDTH + B_KV_WIDTH], gk_ref[...])
    v = x[:, B_WIDTH + B_KV_WIDTH:B_PROJ]
    placed = _place_q_heads(qn)
    scale = HEAD_DIM ** -0.5
    head = _iota((B_HEADS, 1), 0)
    sink = jnp.zeros((B_HEADS, 1), F32)
    for h in range(B_HEADS):
        sink = jnp.where(head == h, sinks_ref[layer, h], sink)
    last_key = _iota((B_KV_WIDTH, WINDOW), 1) == WINDOW - 1
    k_cols = _column_broadcasts(kn, WINDOW)
    v_cols = _column_broadcasts(v, WINDOW)

    def attend(j):
        kc = ck_ref[j]
        vc = cv_ref[j]
        group = slice(j * WINDOW, (j + 1) * WINDOW)
        q4 = jnp.concatenate([pl_h[j:j + 1] for pl_h in placed], axis=0)
        s = _mm(q4, kc) * scale
        nk_ref[j] = jnp.where(last_key, k_cols[:, group], pltpu.roll(kc, WINDOW - 1, axis=1))
        nv_ref[j] = jnp.where(last_key, v_cols[:, group], pltpu.roll(vc, WINDOW - 1, axis=1))
        yield
        s_new = jnp.sum(q4 * kn[j:j + 1], axis=-1, keepdims=True) * scale
        m = jnp.maximum(jnp.maximum(jnp.max(s, axis=-1, keepdims=True), s_new), sink)
        pr = jnp.exp(s - m)
        p_new = jnp.exp(s_new - m)
        den = jnp.sum(pr, axis=-1, keepdims=True) + p_new + jnp.exp(sink - m)
        o = _mm_nt(pr, vc)
        yield
        return (o + p_new * v[j:j + 1]) / den

    o_rows = _interleave([attend(j) for j in range(nb)])
    outs = [jnp.concatenate([o[h:h + 1] for o in o_rows], axis=0) for h in range(B_HEADS)]
    yb_ref[...] = _gather_o_heads(outs)


def _swa_decode(proj, cache_k, cache_v, gq, gk, sinks, layer, k_all, v_all):
    nd = proj.shape[0]
    nb = 2 * DEC_BLOCK if nd % (2 * DEC_BLOCK) == 0 else DEC_BLOCK
    col = A_PROJ // B_PROJ
    cache_spec = pl.BlockSpec((None, nb, WINDOW, B_KV_WIDTH), lambda i: (layer, i, 0, 0))
    any_spec = pl.BlockSpec(memory_space=pl.ANY)
    return pl.pallas_call(
        functools.partial(_swa_decode_kernel, layer),
        grid=(nd // nb,),
        in_specs=[
            pl.BlockSpec(memory_space=pltpu.SMEM),
            pl.BlockSpec((nb, B_PROJ), lambda i: (i, col)),
            cache_spec, cache_spec,
            _layer_spec(layer, (1, B_WIDTH)),
            _layer_spec(layer, (1, B_KV_WIDTH)),
            any_spec, any_spec,
        ],
        out_specs=[pl.BlockSpec((nb, B_WIDTH), lambda i: (i, 0)), cache_spec, cache_spec],
        out_shape=[
            jax.ShapeDtypeStruct((nd, B_WIDTH), F32),
            jax.ShapeDtypeStruct(k_all.shape, F32),
            jax.ShapeDtypeStruct(v_all.shape, F32),
        ],
        input_output_aliases={6: 1, 7: 2},
        compiler_params=pltpu.CompilerParams(
            dimension_semantics=("arbitrary",), vmem_limit_bytes=VMEM_LIMIT_BYTES),
        name="swa_decode",
    )(sinks, proj, cache_k, cache_v, gq, gk, k_all, v_all)


def _head_expand(width_per_head):
    n = C_HEADS * width_per_head
    return (_iota((LANES, n), 0) == _iota((LANES, n), 1) // width_per_head).astype(F32)


def _ssd_prompt_kernel(z_ref, xbc_ref, dt_ref, cw_ref, cb_ref, dtb_ref, aneg_ref, dskip_ref, ng_ref,
                       yc_ref, tail_ref, hout_ref, xpad_ref, h_ref):
    j = pl.program_id(1)
    tb = xbc_ref.shape[0]

    @pl.when(j == 0)
    def _():
        xpad_ref[0:SUBLANES, :] = jnp.zeros((SUBLANES, CONV_DIM), F32)
        h_ref[...] = jnp.zeros_like(h_ref)

    ch = SSD_CHUNK
    n_chunks = tb // ch
    dt = _softplus(dt_ref[...] + dtb_ref[...])
    dt_x = _mm_sel_right(dt, _head_expand(HEAD_DIM))
    ri = _iota((tb, tb), 0)
    ci = _iota((tb, tb), 1)
    chunk_tril = ((ri // ch == ci // ch) & (ri >= ci)).astype(F32)
    acum = _mm_sel_left(chunk_tril, dt * aneg_ref[...])
    acum_x = _mm_sel_right(acum, _head_expand(D_STATE))

    xpad_ref[SUBLANES:SUBLANES + tb, :] = xbc_ref[...]
    cw = cw_ref[...]
    conv = cb_ref[...]
    for k in range(SSM_CONV):
        conv = conv + cw[SSM_CONV - 1 - k:SSM_CONV - k] * xpad_ref[SUBLANES - k:SUBLANES - k + tb, :]
    last_rows = xbc_ref[tb - SUBLANES:tb, :]
    tail_ref[...] = last_rows
    xpad_ref[0:SUBLANES, :] = last_rows
    act = _silu(conv)
    x = act[:, 0:C_WIDTH]
    bm = act[:, C_WIDTH:C_WIDTH + C_GROUPS * D_STATE]
    cm = act[:, C_WIDTH + C_GROUPS * D_STATE:CONV_DIM]
    xdt = x * dt_x
    causal = _iota((ch, ch), 0) >= _iota((ch, ch), 1)
    lane_lo = _iota((ch, 2 * HEAD_DIM), 1) < HEAD_DIM
    row_lo = _iota((2 * HEAD_DIM, D_STATE), 0) < HEAD_DIM
    heads_per_group = C_HEADS // C_GROUPS
    n_pairs = C_HEADS // 2
    head_lanes = lambda e: slice(e * D_STATE, (e + 1) * D_STATE)
    pair_lanes = lambda pr: slice(pr * 2 * HEAD_DIM, (pr + 1) * 2 * HEAD_DIM)
    group_lanes = lambda pr: slice(((2 * pr) // heads_per_group) * D_STATE, ((2 * pr) // heads_per_group + 1) * D_STATE)

    def chunk_local(c):
        rows = slice(c * ch, (c + 1) * ch)
        acx = acum_x[rows]
        ac_t = acum[rows].T
        last = acx[ch - 1:ch]
        decay_to_end = jnp.exp(last - acx)
        cbs = [_mm_nt(cm[rows, g * D_STATE:(g + 1) * D_STATE], bm[rows, g * D_STATE:(g + 1) * D_STATE])
               for g in range(C_GROUPS)]
        yield
        y_diag, states = [], []
        for pr in range(n_pairs):
            e0, e1 = 2 * pr, 2 * pr + 1
            xp = xdt[rows, pair_lanes(pr)]
            yd = []
            for e in (e0, e1):
                seg = acx[:, head_lanes(e)] - ac_t[e:e + 1, :]
                lmat = jnp.exp(jnp.where(causal, seg, -jnp.inf))
                yd.append(_mm(cbs[(2 * pr) // heads_per_group] * lmat, xp))
            y_diag.append(jnp.where(lane_lo, yd[0], yd[1]))
            scale = jnp.where(lane_lo, decay_to_end[:, head_lanes(e0)], decay_to_end[:, head_lanes(e1)])
            states.append(_mm_tn(xp * scale, bm[rows, group_lanes(pr)]))
        yield
        chunk_decay = jnp.exp(last)
        cds = [jnp.where(row_lo, chunk_decay[:, head_lanes(2 * pr)], chunk_decay[:, head_lanes(2 * pr + 1)])
               for pr in range(n_pairs)]
        return y_diag, states, cds, jnp.exp(acx)

    local = _interleave([chunk_local(c) for c in range(n_chunks)])
    h = [h_ref[pair_lanes(pr), :] for pr in range(n_pairs)]
    y_chunks = []
    for c in range(n_chunks):
        rows = slice(c * ch, (c + 1) * ch)
        y_diag, states, cds, ea = local[c]
        pairs = []
        for pr in range(n_pairs):
            y_off = _mm_nt(cm[rows, group_lanes(pr)], h[pr])
            pairs.append(y_diag[pr] + y_off * jnp.where(lane_lo, ea[:, head_lanes(2 * pr)], ea[:, head_lanes(2 * pr + 1)]))
            h[pr] = h[pr] * cds[pr] + states[pr]
        y_chunks.append(jnp.concatenate(pairs, axis=1))
    for pr in range(n_pairs):
        h_ref[pair_lanes(pr), :] = h[pr]
        hout_ref[pair_lanes(pr), :] = h[pr]

    y = (jnp.concatenate(y_chunks, axis=0) + dskip_ref[...] * x) * _silu(z_ref[...])
    yc_ref[...] = _rms_norm(y, ng_ref[...])


def _ssd_prompt(proj, prm, layer, tb):
    nseq, t, _ = proj.shape
    return pl.pallas_call(
        _ssd_prompt_kernel,
        grid=(nseq, t // tb),
        in_specs=[
            pl.BlockSpec((None, tb, C_WIDTH), lambda n, j: (n, j, Z_OFF // C_WIDTH)),
            pl.BlockSpec((None, tb, CONV_DIM), lambda n, j: (n, j, XBC_OFF // CONV_DIM)),
            pl.BlockSpec((None, tb, LANES), lambda n, j: (n, j, DT_OFF // LANES)),
            _layer_spec(layer, (SSM_CONV, CONV_DIM)),
            _layer_spec(layer, (1, CONV_DIM)),
            _layer_spec(layer, (1, LANES)),
            _layer_spec(layer, (1, LANES)),
            _layer_spec(layer, (1, C_WIDTH)),
            _layer_spec(layer, (1, C_WIDTH)),
        ],
        out_specs=[
            pl.BlockSpec((None, tb, C_WIDTH), lambda n, j: (n, j, 0)),
            pl.BlockSpec((None, SUBLANES, CONV_DIM), lambda n, j: (n, 0, 0)),
            pl.BlockSpec((None, C_WIDTH, D_STATE), lambda n, j: (n, 0, 0)),
        ],
        out_shape=[
            jax.ShapeDtypeStruct((nseq, t, C_WIDTH), F32),
            jax.ShapeDtypeStruct((nseq, SUBLANES, CONV_DIM), F32),
            jax.ShapeDtypeStruct((nseq, C_WIDTH, D_STATE), F32),
        ],
        scratch_shapes=[
            pltpu.VMEM((SUBLANES + tb, CONV_DIM), F32),
            pltpu.VMEM((C_WIDTH, D_STATE), F32),
        ],
        compiler_params=pltpu.CompilerParams(
            dimension_semantics=("arbitrary", "arbitrary"), vmem_limit_bytes=VMEM_LIMIT_BYTES),
        name="ssd_prompt",
    )(proj, proj, proj, prm["conv_w"], prm["conv_b"], prm["dt_bias"], prm["a_neg"], prm["d_skip"], prm["norm_g"])


def _ssd_decode_kernel(z_ref, xbc_ref, dt_ref, buf_ref, h0_ref, cw_ref, cb_ref, dtb_ref, aneg_ref, dskip_ref, ng_ref,
                       _h_all, yc_ref, nbuf_ref, hout_ref):
    xbc = xbc_ref[...]
    nb = xbc.shape[0]
    cw = cw_ref[...]
    conv = cb_ref[...] + cw[SSM_CONV - 1:SSM_CONV] * xbc
    for k in range(SSM_CONV - 1):
        conv = conv + cw[k:k + 1] * buf_ref[k]
        nbuf_ref[k] = buf_ref[k + 1] if k + 1 < SSM_CONV - 1 else xbc
    act = _silu(conv)
    x = act[:, 0:C_WIDTH]
    bm = act[:, C_WIDTH:C_WIDTH + C_GROUPS * D_STATE]
    cm = act[:, C_WIDTH + C_GROUPS * D_STATE:CONV_DIM]
    dt = _softplus(dt_ref[...] + dtb_ref[...])
    decay = _mm_sel_right(jnp.exp(dt * aneg_ref[...]), _head_expand(D_STATE))
    xdt = x * _mm_sel_right(dt, _head_expand(HEAD_DIM))
    heads_per_group = C_HEADS // C_GROUPS
    x_cols = _column_broadcasts(xdt, D_STATE)
    hc = []
    for j in range(nb):
        for e in range(C_HEADS):
            gs = slice((e // heads_per_group) * D_STATE, (e // heads_per_group + 1) * D_STATE)
            hs = slice(e * HEAD_DIM, (e + 1) * HEAD_DIM)
            h = (h0_ref[j, hs, :] * decay[j:j + 1, e * D_STATE:(e + 1) * D_STATE]
                 + x_cols[hs, j * D_STATE:(j + 1) * D_STATE] * bm[j:j + 1, gs])
            hout_ref[j, hs, :] = h
            hc.append(h * cm[j:j + 1, gs])
    y_cols = _row_sums(hc, D_STATE)
    lane = _iota((C_WIDTH, nb), 1)
    y_t = jnp.zeros((C_WIDTH, nb), F32)
    for j in range(nb):
        y_t = jnp.where(lane == j, y_cols[j * C_WIDTH:(j + 1) * C_WIDTH, 0:nb], y_t)
    y = _mm_nt_sel_left(_eye(nb), y_t)
    y = (y + dskip_ref[...] * x) * _silu(z_ref[...])
    yc_ref[...] = _rms_norm(y, ng_ref[...])


def _ssd_decode(proj, conv_buf_t, h0, prm, layer, h_all):
    nd = proj.shape[0]
    nb = 2 * DEC_BLOCK if nd % (2 * DEC_BLOCK) == 0 else DEC_BLOCK
    state_spec = pl.BlockSpec((None, nb, C_WIDTH, D_STATE), lambda i: (layer, i, 0, 0))
    return pl.pallas_call(
        _ssd_decode_kernel,
        grid=(nd // nb,),
        in_specs=[
            pl.BlockSpec((nb, C_WIDTH), lambda i: (i, Z_OFF // C_WIDTH)),
            pl.BlockSpec((nb, CONV_DIM), lambda i: (i, XBC_OFF // CONV_DIM)),
            pl.BlockSpec((nb, LANES), lambda i: (i, DT_OFF // LANES)),
            pl.BlockSpec((None, SSM_CONV - 1, nb, CONV_DIM), lambda i: (layer, 0, i, 0)),
            state_spec,
            _layer_spec(layer, (SSM_CONV, CONV_DIM)),
            _layer_spec(layer, (1, CONV_DIM)),
            _layer_spec(layer, (1, LANES)),
            _layer_spec(layer, (1, LANES)),
            _layer_spec(layer, (1, C_WIDTH)),
            _layer_spec(layer, (1, C_WIDTH)),
            pl.BlockSpec(memory_space=pl.ANY),
        ],
        out_specs=[
            pl.BlockSpec((nb, C_WIDTH), lambda i: (i, 0)),
            pl.BlockSpec((SSM_CONV - 1, nb, CONV_DIM), lambda i: (0, i, 0)),
            state_spec,
        ],
        out_shape=[
            jax.ShapeDtypeStruct((nd, C_WIDTH), F32),
            jax.ShapeDtypeStruct((SSM_CONV - 1, nd, CONV_DIM), F32),
            jax.ShapeDtypeStruct(h_all.shape, F32),
        ],
        input_output_aliases={11: 2},
        compiler_params=pltpu.CompilerParams(
            dimension_semantics=("arbitrary",), vmem_limit_bytes=VMEM_LIMIT_BYTES),
        name="ssd_decode",
    )(proj, proj, proj, conv_buf_t, h0, prm["conv_w"], prm["conv_b"], prm["dt_bias"], prm["a_neg"],
      prm["d_skip"], prm["norm_g"], h_all)


def _out_ffn_core(x, mix, g1, sh2, sc2, g2, gn, wo_ref, wu_ref, cw_ref, cb_ref, wd_ref, conv_fn):
    x1 = x + g1 * _mm(mix, wo_ref[...])
    h = (_rms_norm(x1, gn) * (1.0 + sc2) + sh2).astype(BF16)
    acc = jnp.zeros(x.shape, F32)
    for c in range(D_FF // FF_CHUNK):
        cs = slice(c * FF_CHUNK, (c + 1) * FF_CHUNK)
        gate = jnp.dot(h, wu_ref[:, cs], preferred_element_type=F32)
        val = jnp.dot(h, wu_ref[:, D_FF + c * FF_CHUNK:D_FF + (c + 1) * FF_CHUNK], preferred_element_type=F32)
        gate = conv_fn(gate, cs, cw_ref[:, cs], cb_ref[:, cs])
        acc = acc + _mm(_silu(gate) * val, wd_ref[cs, :])
    return x1 + g2 * acc


def _out_ffn_prompt_kernel(tiles_per_seq, x_ref, ya_ref, yb_ref, yc_ref, g1_ref, sh2_ref, sc2_ref, g2_ref, gn_ref,
                           wo_ref, wu_ref, cw_ref, cb_ref, wd_ref, o_ref, tail_ref, carry_ref):
    i = pl.program_id(0)
    tm = x_ref.shape[0]

    @pl.when(i % tiles_per_seq == 0)
    def _():
        carry_ref[...] = jnp.zeros_like(carry_ref)

    def conv_fn(gate, cs, cw, cb):
        tail = carry_ref[:, cs]
        out = cb + cw[FFN_CONV - 1:FFN_CONV] * gate
        for k in range(1, FFN_CONV):
            out = out + cw[FFN_CONV - 1 - k:FFN_CONV - k] * _shift_rows(gate, k, tail)
        last_rows = gate[tm - SUBLANES:tm, :]
        carry_ref[:, cs] = last_rows
        tail_ref[:, cs] = last_rows
        return out

    mix = jnp.concatenate([ya_ref[...], yb_ref[...], yc_ref[...]], axis=1)
    o_ref[...] = _out_ffn_core(x_ref[...], mix, g1_ref[...], sh2_ref[...], sc2_ref[...], g2_ref[...], gn_ref[...],
                               wo_ref, wu_ref, cw_ref, cb_ref, wd_ref, conv_fn)


def _out_ffn_decode_kernel(x_ref, ya_ref, yb_ref, yc_ref, g1_ref, sh2_ref, sc2_ref, g2_ref, gn_ref,
                           wo_ref, wu_ref, cw_ref, cb_ref, wd_ref, buf_ref, o_ref, nbuf_ref):
    def conv_fn(gate, cs, cw, cb):
        out = cb + cw[FFN_CONV - 1:FFN_CONV] * gate
        for k in range(FFN_CONV - 1):
            out = out + cw[k:k + 1] * buf_ref[k, :, cs]
            nbuf_ref[k, :, cs] = buf_ref[k + 1, :, cs] if k + 1 < FFN_CONV - 1 else gate
        return out

    mix = jnp.concatenate([ya_ref[...], yb_ref[...], yc_ref[...]], axis=1)
    o_ref[...] = _out_ffn_core(x_ref[...], mix, g1_ref[...], sh2_ref[...], sc2_ref[...], g2_ref[...], gn_ref[...],
                               wo_ref, wu_ref, cw_ref, cb_ref, wd_ref, conv_fn)


def _out_ffn(x, ya, yb, yc, mod, prm, layer, tm, per_row, tiles_per_seq, conv_buf_t=None):
    rows = x.shape[0]
    nseq = rows // (tm * tiles_per_seq) if not per_row else None
    row_spec = lambda w: pl.BlockSpec((tm, w), lambda i: (i, 0))
    in_specs = [row_spec(D_MODEL), row_spec(A_WIDTH), row_spec(B_WIDTH), row_spec(C_WIDTH)]
    in_specs += [_mod_spec(layer, per_row, tiles_per_seq, c) for c in (2, 3, 4, 5)]
    in_specs += [
        _layer_spec(layer, (1, D_MODEL)),
        _layer_spec(layer, (D_MODEL, D_MODEL), single_buffer=True),
        _layer_spec(layer, (D_MODEL, 2 * D_FF), single_buffer=True),
        _layer_spec(layer, (FFN_CONV, D_FF)),
        _layer_spec(layer, (1, D_FF)),
        _layer_spec(layer, (D_FF, D_MODEL), single_buffer=True),
    ]
    args = [x, ya, yb, yc, mod, mod, mod, mod, prm["norm_ffn_g"], prm["w_out"], prm["w_up"], prm["ffn_conv_w"],
            prm["ffn_conv_b"], prm["w_down"]]
    params = pltpu.CompilerParams(dimension_semantics=("arbitrary",), vmem_limit_bytes=VMEM_LIMIT_BYTES)
    if per_row:
        buf_shape = (FFN_CONV - 1, rows, D_FF)
        return pl.pallas_call(
            _out_ffn_decode_kernel,
            grid=(rows // tm,),
            in_specs=in_specs + [pl.BlockSpec((None,) + buf_shape, lambda i: (layer, 0, i, 0))],
            out_specs=[row_spec(D_MODEL), pl.BlockSpec(buf_shape, lambda i: (0, i, 0))],
            out_shape=[jax.ShapeDtypeStruct((rows, D_MODEL), F32), jax.ShapeDtypeStruct(buf_shape, F32)],
            compiler_params=params,
            name="out_ffn_decode",
        )(*args, conv_buf_t)
    return pl.pallas_call(
        functools.partial(_out_ffn_prompt_kernel, tiles_per_seq),
        grid=(rows // tm,),
        in_specs=in_specs,
        out_specs=[row_spec(D_MODEL), pl.BlockSpec((None, SUBLANES, D_FF), lambda i: (i // tiles_per_seq, 0, 0))],
        out_shape=[jax.ShapeDtypeStruct((rows, D_MODEL), F32), jax.ShapeDtypeStruct((nseq, SUBLANES, D_FF), F32)],
        scratch_shapes=[pltpu.VMEM((SUBLANES, D_FF), F32)],
        compiler_params=params,
        name="out_ffn_prompt",
    )(*args)


def _row_param(p, width=None):
    if width is not None and p.shape[-1] != width:
        p = jnp.pad(p, ((0, 0), (0, width - p.shape[-1])))
    return p.reshape(p.shape[0], 1, p.shape[-1])


def _largest_tile(total, cap, quantum):
    tile = min(total, cap)
    while total % tile or tile % quantum:
        tile -= quantum
    return tile


def kernel(x_prompt, x_sample, c_prompt, c_sample, state_rwkv_shift, state_rwkv_wkv, cache_swa_k, cache_swa_v, state_ssm_conv, state_ssm, state_ffn_conv, ada_w, ada_b, norm_mix_g, norm_ffn_g, w_in, w_out, rwkv_mu, rwkv_w0, rwkv_w2, rwkv_a0, rwkv_a2, rwkv_g2, rwkv_k_k, rwkv_k_a, rwkv_r_k, rwkv_ln_g, rwkv_ln_b, attn_q_norm_g, attn_k_norm_g, attn_sinks, ssm_conv_w, ssm_conv_b, ssm_dt_bias, ssm_a_log, ssm_d, ssm_norm_g, ffn_w_up, ffn_conv_w, ffn_conv_b, ffn_w_down):
    depth = w_in.shape[0]
    nseq, t, _ = x_prompt.shape
    nd = x_sample.shape[0]
    assert x_sample.shape[1] == 1 and nd % DEC_BLOCK == 0 and t % SSD_CHUNK == 0

    tm = _largest_tile(t, 512, SUBLANES)
    tiles_per_seq = t // tm
    tb_rwkv = _largest_tile(t, 4 * RWKV_GROUP * RWKV_CHUNK, RWKV_CHUNK)
    tq_swa = _largest_tile(t, 1024, SWA_BLOCK)
    tb_ssd = _largest_tile(t, 512, SSD_CHUNK)
    tm_in = _largest_tile(t, 1024, SUBLANES)

    w_in_p = jnp.pad(w_in, ((0, 0), (0, 0), (0, PROJ_PAD - PROJ))).astype(BF16)
    rwkv_prm = {
        "mu": _row_param(rwkv_mu), "w0": _row_param(rwkv_w0), "a0": _row_param(rwkv_a0),
        "w2": jnp.pad(rwkv_w2, ((0, 0), (0, LANES - A_DECAY_LORA), (0, 0))),
        "a2": jnp.pad(rwkv_a2, ((0, 0), (A_DECAY_LORA, 0), (0, 0))),
        "g2": rwkv_g2,
        "k_k": _row_param(rwkv_k_k), "k_a": _row_param(rwkv_k_a),
        "r_k": _row_param(rwkv_r_k.reshape(depth, A_WIDTH)),
        "ln_g": _row_param(rwkv_ln_g), "ln_b": _row_param(rwkv_ln_b),
    }
    gq = _row_param(jnp.tile(attn_q_norm_g, (1, B_HEADS)))
    gk = _row_param(jnp.tile(attn_k_norm_g, (1, B_KV_HEADS)))
    ssd_prm = {
        "conv_w": ssm_conv_w, "conv_b": _row_param(ssm_conv_b),
        "dt_bias": _row_param(ssm_dt_bias, LANES),
        "a_neg": _row_param(-jnp.exp(ssm_a_log), LANES),
        "d_skip": _row_param(jnp.repeat(ssm_d, HEAD_DIM, axis=1)),
        "norm_g": _row_param(ssm_norm_g),
    }
    ffn_prm = {
        "norm_ffn_g": _row_param(norm_ffn_g), "w_out": w_out.astype(BF16), "w_up": ffn_w_up.astype(BF16),
        "ffn_conv_w": ffn_conv_w, "ffn_conv_b": _row_param(ffn_conv_b), "w_down": ffn_w_down.astype(BF16),
    }
    norm_mix = _row_param(norm_mix_g)

    c_all = jnp.concatenate([jnp.pad(c_prompt, ((0, SUBLANES - nseq), (0, 0))), c_sample], axis=0)
    mod = _ada(c_all, ada_w, ada_b)
    mod_p = mod[:, 0:nseq].reshape(depth, nseq, 1, N_MOD * D_MODEL)
    mod_s = mod[:, SUBLANES:]

    to_keys_minor = lambda c: jnp.transpose(c, (0, 1, 3, 4, 2)).reshape(depth, nd, B_KV_WIDTH, WINDOW)
    from_keys_minor = lambda c: jnp.transpose(c.reshape(depth, nd, B_KV_HEADS, HEAD_DIM, WINDOW), (0, 1, 4, 2, 3))
    cache_k = to_keys_minor(cache_swa_k)
    cache_v = to_keys_minor(cache_swa_v)
    wkv0_t = jnp.transpose(state_rwkv_wkv, (0, 2, 3, 4, 1))
    ssm_conv_t = jnp.swapaxes(state_ssm_conv, 1, 2)
    ssm_h0 = state_ssm.reshape(depth, nd, C_WIDTH, D_STATE)
    ffn_conv_t = jnp.swapaxes(state_ffn_conv, 1, 2)

    xp = x_prompt.reshape(nseq * t, D_MODEL)
    xs = x_sample.reshape(nd, D_MODEL)
    prompt_new = [[] for _ in range(7)]
    sample_new = [[] for _ in range(3)]
    s_wkv = jnp.zeros(wkv0_t.shape, F32)
    s_k = jnp.zeros(cache_k.shape, F32)
    s_v = jnp.zeros(cache_v.shape, F32)
    s_ssm = jnp.zeros(ssm_h0.shape, F32)
    for l in range(depth):
        proj = _in_proj(xp, mod_p, norm_mix, w_in_p, l, tm_in, 0, t // tm_in).reshape(nseq, t, PROJ_PAD)
        ya, p_shift, p_wkv = _rwkv_prompt(proj, rwkv_prm, l, tb_rwkv)
        yb, p_k, p_v = _swa_prompt(proj, gq, gk, attn_sinks, l, tq_swa)
        yc, p_conv, p_ssm = _ssd_prompt(proj, ssd_prm, l, tb_ssd)
        xp, p_ffn = _out_ffn(xp, ya.reshape(nseq * t, A_WIDTH), yb.reshape(nseq * t, B_WIDTH),
                             yc.reshape(nseq * t, C_WIDTH), mod_p, ffn_prm, l, tm, 0, tiles_per_seq)
        new_p = (p_shift.reshape(nseq, A_PROJ), p_wkv,
                 p_k.reshape(nseq, WINDOW, B_KV_HEADS, HEAD_DIM), p_v.reshape(nseq, WINDOW, B_KV_HEADS, HEAD_DIM),
                 p_conv[:, SUBLANES - (SSM_CONV - 1):], p_ssm.reshape(nseq, C_HEADS, HEAD_DIM, D_STATE),
                 p_ffn[:, SUBLANES - (FFN_CONV - 1):])
        proj_s = _in_proj(xs, mod_s, norm_mix, w_in_p, l, nd, nd, 1)
        ya, s_wkv = _rwkv_decode(proj_s, state_rwkv_shift, wkv0_t, rwkv_prm, l, s_wkv)
        yb, s_k, s_v = _swa_decode(proj_s, cache_k, cache_v, gq, gk, attn_sinks, l, s_k, s_v)
        yc, s_conv_t, s_ssm = _ssd_decode(proj_s, ssm_conv_t, ssm_h0, ssd_prm, l, s_ssm)
        xs, s_ffn_t = _out_ffn(xs, ya, yb, yc, mod_s, ffn_prm, l, nd, nd, 1, ffn_conv_t)
        new_s = (proj_s[:, 0:A_PROJ], jnp.swapaxes(s_conv_t, 0, 1), jnp.swapaxes(s_ffn_t, 0, 1))
        for lst, arr in zip(prompt_new, new_p):
            lst.append(arr)
        for lst, arr in zip(sample_new, new_s):
            lst.append(arr)
    outs_p = [jnp.stack(lst) for lst in prompt_new]
    s_shift, s_conv, s_ffn = [jnp.stack(lst) for lst in sample_new]
    outs_s = (s_shift, jnp.transpose(s_wkv, (0, 4, 1, 2, 3)), from_keys_minor(s_k), from_keys_minor(s_v), s_conv,
              s_ssm.reshape(depth, nd, C_HEADS, HEAD_DIM, D_STATE), s_ffn)
    return (xp.reshape(nseq, t, D_MODEL), xs.reshape(nd, 1, D_MODEL), *outs_p, *outs_s)
```

```python
import functools

import jax
import jax.numpy as jnp
from jax import lax
from jax.experimental import pallas as pl
from jax.experimental.pallas import tpu as pltpu

F32 = jnp.float32
BF16 = jnp.bfloat16

D_MODEL = 1024
HEAD_DIM = 64
A_WIDTH = 256
A_HEADS = A_WIDTH // HEAD_DIM
A_DECAY_LORA = 64
A_AAA_LORA = 64
A_GATE_LORA = 128
A_PROJ = 3 * A_WIDTH + A_DECAY_LORA + A_AAA_LORA + A_GATE_LORA
B_WIDTH = 256
B_HEADS = B_WIDTH // HEAD_DIM
B_KV_HEADS = 2
B_GROUP = B_HEADS // B_KV_HEADS
B_KV_WIDTH = B_KV_HEADS * HEAD_DIM
B_PROJ = B_WIDTH + 2 * B_KV_WIDTH
WINDOW = 128
C_WIDTH = 512
C_HEADS = C_WIDTH // HEAD_DIM
C_GROUPS = 2
D_STATE = 128
SSM_CONV = 4
CONV_DIM = C_WIDTH + 2 * C_GROUPS * D_STATE
C_PROJ = C_WIDTH + CONV_DIM + C_HEADS
PROJ = A_PROJ + B_PROJ + C_PROJ
D_FF = 2816
FFN_CONV = 3
NORM_EPS = 1e-6
GN_EPS = 64e-5
N_MOD = 6

LANES = 128
SUBLANES = 8
VMEM_LIMIT_BYTES = 56 * 1024 * 1024

DT_OFF = A_PROJ + B_PROJ + C_WIDTH + CONV_DIM
PROJ_PAD = DT_OFF + LANES
Z_OFF = A_PROJ + B_PROJ
XBC_OFF = Z_OFF + C_WIDTH

RWKV_CHUNK = 64
INV_BLOCK = 4
RWKV_GROUP = 4
SSD_CHUNK = 128
SWA_BLOCK = 128
DEC_BLOCK = 8
FF_CHUNK = D_FF // 2


def _mm(a, b):
    return jnp.dot(a.astype(BF16), b.astype(BF16), preferred_element_type=F32)


def _mm_nt(a, b):
    return lax.dot_general(a.astype(BF16), b.astype(BF16), (((1,), (1,)), ((), ())), preferred_element_type=F32)


def _mm_tn(a, b):
    return lax.dot_general(a.astype(BF16), b.astype(BF16), (((0,), (0,)), ((), ())), preferred_element_type=F32)


def _split3(x):
    hi = x.astype(BF16)
    r1 = x - hi.astype(F32)
    mid = r1.astype(BF16)
    lo = (r1 - mid.astype(F32)).astype(BF16)
    return hi, mid, lo


def _mm_sel_left(sel, x):
    s = sel.astype(BF16)
    return jnp.dot(jnp.concatenate([s, s, s], axis=1), jnp.concatenate(_split3(x), axis=0),
                   preferred_element_type=F32)


def _mm_sel_right(x, sel):
    s = sel.astype(BF16)
    return jnp.dot(jnp.concatenate(_split3(x), axis=1), jnp.concatenate([s, s, s], axis=0),
                   preferred_element_type=F32)


def _mm_nt_sel_left(sel, x):
    s = sel.astype(BF16)
    return lax.dot_general(jnp.concatenate([s, s, s], axis=1), jnp.concatenate(_split3(x), axis=1),
                           (((1,), (1,)), ((), ())), preferred_element_type=F32)


def _row_sums(tiles, width):
    ones = jnp.ones((width, width), BF16)
    half = len(tiles) // 2
    dot = lambda xs: jnp.dot(jnp.concatenate(xs, axis=0).astype(BF16), ones, preferred_element_type=F32)
    return jnp.concatenate([dot(tiles[:half]), dot(tiles[half:])], axis=0)


def _column_broadcasts(x, group):
    nb, w = x.shape
    parts = jnp.concatenate(_split3(x), axis=0)
    parts_t = lax.dot_general(_eye(w).astype(BF16), parts, (((1,), (1,)), ((), ())),
                              preferred_element_type=F32).astype(BF16)
    src = _iota((3 * nb, nb * group), 0) % nb
    dst = _iota((3 * nb, nb * group), 1) // group
    return jnp.dot(parts_t, (src == dst).astype(BF16), preferred_element_type=F32)


def _sigmoid(x):
    return 1.0 / (1.0 + jnp.exp(-x))


def _silu(x):
    return x * _sigmoid(x)


def _softplus(x):
    return jnp.maximum(x, 0.0) + jnp.log(1.0 + jnp.exp(-jnp.abs(x)))


def _iota(shape, dim):
    return lax.broadcasted_iota(jnp.int32, shape, dim)


def _eye(n):
    return (_iota((n, n), 0) == _iota((n, n), 1)).astype(F32)


def _tril_ones(n):
    return (_iota((n, n), 0) >= _iota((n, n), 1)).astype(F32)


def _group_sum(x, group):
    tiles = []
    for t0 in range(0, x.shape[-1], LANES):
        tile = x[:, t0:t0 + LANES]
        gid = _iota(tile.shape, 1) // group
        sums = [jnp.sum(jnp.where(gid == h, tile, 0.0), axis=-1, keepdims=True) for h in range(LANES // group)]
        out = sums[-1]
        for h in range(LANES // group - 1):
            out = jnp.where(gid == h, sums[h], out)
        tiles.append(jnp.broadcast_to(out, tile.shape))
    return jnp.concatenate(tiles, axis=1) if len(tiles) > 1 else tiles[0]


def _rms_norm(x, g):
    return x * lax.rsqrt(jnp.mean(x * x, axis=-1, keepdims=True) + NORM_EPS) * g


def _head_rms_norm(x, g):
    return x * lax.rsqrt(_group_sum(x * x, HEAD_DIM) * (1.0 / HEAD_DIM) + NORM_EPS) * g


def _shift_rows(x, k, tail):
    rolled = pltpu.roll(x, k, axis=0)
    row = _iota((SUBLANES, x.shape[1]), 0)
    top = jnp.where(row < k, pltpu.roll(tail, k, axis=0), rolled[0:SUBLANES])
    return jnp.concatenate([top, rolled[SUBLANES:]], axis=0)


def _ada_kernel(c_ref, w_ref, b_ref, o_ref):
    o_ref[...] = _mm(_silu(c_ref[...]), w_ref[...]) + b_ref[...]


def _ada(c_all, ada_w, ada_b):
    depth = ada_w.shape[0]
    rows = c_all.shape[0]
    width = N_MOD * D_MODEL
    tn = 1536
    return pl.pallas_call(
        _ada_kernel,
        grid=(depth, width // tn),
        in_specs=[
            pl.BlockSpec((rows, D_MODEL), lambda l, j: (0, 0)),
            pl.BlockSpec((None, D_MODEL, tn), lambda l, j: (l, 0, j)),
            pl.BlockSpec((None, 1, tn), lambda l, j: (l, 0, j)),
        ],
        out_specs=pl.BlockSpec((None, rows, tn), lambda l, j: (l, 0, j)),
        out_shape=jax.ShapeDtypeStruct((depth, rows, width), F32),
        compiler_params=pltpu.CompilerParams(
            dimension_semantics=("arbitrary", "arbitrary"), vmem_limit_bytes=VMEM_LIMIT_BYTES),
        name="ada_mod",
    )(c_all, ada_w, ada_b.reshape(depth, 1, width))


def _in_proj_kernel(x_ref, sh_ref, sc_ref, g_ref, w_ref, o_ref):
    h = _rms_norm(x_ref[...], g_ref[...]) * (1.0 + sc_ref[...]) + sh_ref[...]
    o_ref[...] = _mm(h, w_ref[...])


def _mod_spec(layer, per_row, rows_per_seq_tiles, col):
    if per_row:
        return pl.BlockSpec((None, per_row, D_MODEL), lambda i: (layer, 0, col))
    return pl.BlockSpec((None, None, 1, D_MODEL), lambda i: (layer, i // rows_per_seq_tiles, 0, col))


def _layer_spec(layer, tail, single_buffer=False):
    idx = lambda *_: (layer,) + (0,) * len(tail)
    if single_buffer:
        return pl.BlockSpec((None,) + tuple(tail), idx, pipeline_mode=pl.Buffered(1))
    return pl.BlockSpec((None,) + tuple(tail), idx)


def _in_proj(x, mod, norm_g, w_in, layer, tm, per_row, tiles_per_seq):
    rows = x.shape[0]
    return pl.pallas_call(
        _in_proj_kernel,
        grid=(rows // tm,),
        in_specs=[
            pl.BlockSpec((tm, D_MODEL), lambda i: (i, 0)),
            _mod_spec(layer, per_row, tiles_per_seq, 0),
            _mod_spec(layer, per_row, tiles_per_seq, 1),
            _layer_spec(layer, (1, D_MODEL)),
            _layer_spec(layer, (D_MODEL, PROJ_PAD), single_buffer=True),
        ],
        out_specs=pl.BlockSpec((tm, PROJ_PAD), lambda i: (i, 0)),
        out_shape=jax.ShapeDtypeStruct((rows, PROJ_PAD), F32),
        compiler_params=pltpu.CompilerParams(
            dimension_semantics=("arbitrary",), vmem_limit_bytes=VMEM_LIMIT_BYTES),
        name="in_proj",
    )(x, mod, mod, norm_g, w_in)


_RWKV_PARAMS = ("mu", "w0", "w2", "a0", "a2", "g2", "k_k", "k_a", "r_k", "ln_g", "ln_b")


def _rwkv_prep(xs, p):
    r = xs[:, 0:A_WIDTH]
    k = xs[:, A_WIDTH:2 * A_WIDTH]
    v = xs[:, 2 * A_WIDTH:3 * A_WIDTH]
    lora_in = xs[:, 3 * A_WIDTH:3 * A_WIDTH + LANES]
    lg = xs[:, 3 * A_WIDTH + LANES:A_PROJ]
    w_log = -_softplus(-(p["w0"] + _mm(jnp.tanh(lora_in), p["w2"]))) - 0.5
    log_decay = -jnp.exp(w_log)
    a = _sigmoid(p["a0"] + _mm(lora_in, p["a2"]))
    g = _mm(_sigmoid(lg), p["g2"])
    kk = k * p["k_k"]
    kk = kk / jnp.maximum(jnp.sqrt(_group_sum(kk * kk, HEAD_DIM)), 1e-12)
    k = k * (1.0 + (a - 1.0) * p["k_a"])
    return r, k, v, log_decay, -kk, kk * a, g


def _rwkv_post(y, r, k, v, g, p):
    mean = _group_sum(y, HEAD_DIM) * (1.0 / HEAD_DIM)
    c = y - mean
    var = _group_sum(c * c, HEAD_DIM) * (1.0 / HEAD_DIM)
    yn = c * lax.rsqrt(var + GN_EPS) * p["ln_g"] + p["ln_b"]
    bonus = _group_sum(r * k * p["r_k"], HEAD_DIM) * v
    return (yn + bonus) * g


def _block_diag(x, mask01):
    return jnp.concatenate([x] * (x.shape[1] // x.shape[0]), axis=0) * mask01


def _rwkv_chunk_setup(r, k, v, logd, cum, av, bv, cst):
    bd_mask, strict, incl, eye_cat, level_masks = cst
    excl = cum - logd
    cum_last = cum[RWKV_CHUNK - 1:RWKV_CHUNK, :]
    e_neg = jnp.exp(-cum)
    e_end = jnp.exp(cum_last - cum)
    at = (av * jnp.exp(excl)).astype(BF16)
    rt = (r * jnp.exp(cum)).astype(BF16)
    bt = (bv * e_neg).astype(BF16)
    kt = (k * e_neg).astype(BF16)
    vb = v.astype(BF16)
    bd = lambda x: _block_diag(x, bd_mask)

    lhs = jnp.concatenate([at, rt], axis=0)
    rhs = jnp.concatenate([bd(bt), bd(kt)], axis=0)
    amat = lax.dot_general(lhs, rhs, (((1,), (1,)), ((), ())), preferred_element_type=F32)
    yield
    n_ab = jnp.where(strict, amat[0:RWKV_CHUNK, 0:A_WIDTH], 0.0)
    n_ak = jnp.where(strict, amat[0:RWKV_CHUNK, A_WIDTH:], 0.0).astype(BF16)
    n_rb = jnp.where(incl, amat[RWKV_CHUNK:, 0:A_WIDTH], 0.0).astype(BF16)
    n_rk = jnp.where(incl, amat[RWKV_CHUNK:, A_WIDTH:], 0.0).astype(BF16)
    wy0 = jnp.dot(jnp.concatenate([n_ak, n_rk], axis=0), bd(vb), preferred_element_type=F32)
    yield
    inv = yield from _unit_lower_inverse(n_ab, eye_cat, level_masks, bd)
    return dict(
        lhs=lhs, inv=inv, w0=wy0[0:RWKV_CHUNK], y0=wy0[RWKV_CHUNK:], n_rb=n_rb, vb=vb,
        bk_end=jnp.concatenate([(bv * e_end).astype(BF16), (k * e_end).astype(BF16)], axis=0),
        decay_end=jnp.exp(cum_last),
    )


def _rwkv_chunk_apply(c, state, bd_mask):
    bd = lambda x: _block_diag(x, bd_mask)
    both = lax.dot_general(c["lhs"], state.astype(BF16), (((1,), (1,)), ((), ())), preferred_element_type=F32)
    yield
    w = both[0:RWKV_CHUNK] + c["w0"]
    ub = jnp.dot(c["inv"], bd(w.astype(BF16)), preferred_element_type=F32).astype(BF16)
    yield
    y = both[RWKV_CHUNK:] + c["y0"] + jnp.dot(c["n_rb"], bd(ub), preferred_element_type=F32)
    upd = lax.dot_general(jnp.concatenate([ub, c["vb"]], axis=0), c["bk_end"],
                          (((0,), (0,)), ((), ())), preferred_element_type=F32)
    return y, state * c["decay_end"] + upd * bd_mask.astype(F32)


def _interleave(generators):
    results = [None] * len(generators)
    active = list(enumerate(generators))
    while active:
        still = []
        for i, gen in active:
            try:
                next(gen)
                still.append((i, gen))
            except StopIteration as done:
                results[i] = done.value
        active = still
    return results


def _split2(x):
    hi = x.astype(BF16)
    return hi, (x - hi.astype(F32)).astype(BF16)


def _mm_cat2(lhs_list, b, bd):
    rows = lhs_list[0].shape[0]
    width = b.shape[1]
    b_hi, b_lo = _split2(b)
    lhs = jnp.concatenate([part for a in lhs_list for part in _split2(a)], axis=0)
    prod = jnp.dot(lhs, jnp.concatenate([bd(b_hi), bd(b_lo)], axis=1), preferred_element_type=F32)
    outs = []
    for i in range(len(lhs_list)):
        hi = prod[2 * i * rows:(2 * i + 1) * rows]
        lo = prod[(2 * i + 1) * rows:(2 * i + 2) * rows]
        outs.append((lo[:, 0:width] + hi[:, width:]) + hi[:, 0:width])
    return outs


def _unit_lower_inverse(n, eye_cat, level_masks, bd):
    m = jnp.where(level_masks[0], n, 0.0)
    inv = eye_cat + m
    if INV_BLOCK == 4:
        mb = m.astype(BF16)
        m2 = jnp.dot(mb, bd(mb), preferred_element_type=F32)
        yield
        inv = (inv + m2) + jnp.dot(mb, bd(m2.astype(BF16)), preferred_element_type=F32)
        yield
    else:
        (m,) = _mm_cat2([m], m, bd)
        yield
        for step in range(INV_BLOCK.bit_length() - 2):
            if step + 1 < INV_BLOCK.bit_length() - 2:
                delta, m = _mm_cat2([inv, m], m, bd)
            else:
                (delta,) = _mm_cat2([inv], m, bd)
            yield
            inv = inv + delta
    inv = inv.astype(BF16)
    for mask in level_masks[1:]:
        off = jnp.where(mask, n, 0.0).astype(BF16)
        g = jnp.dot(inv, bd(off), preferred_element_type=F32)
        yield
        inv = inv + jnp.dot(g.astype(BF16), bd(inv), preferred_element_type=F32).astype(BF16)
        yield
    return inv


def _rwkv_consts():
    w = A_WIDTH
    bd_mask = ((_iota((w, w), 0) // HEAD_DIM) == (_iota((w, w), 1) // HEAD_DIM)).astype(BF16)
    t = _iota((RWKV_CHUNK, w), 0)
    s = _iota((RWKV_CHUNK, w), 1) % RWKV_CHUNK
    level_masks = [t // INV_BLOCK == s // INV_BLOCK]
    size = INV_BLOCK
    while size < RWKV_CHUNK:
        level_masks.append((t // (2 * size) == s // (2 * size)) & (t // size != s // size))
        size *= 2
    return bd_mask, s < t, s <= t, (s == t).astype(F32), level_masks


def _load_params(refs):
    return {name: ref[...] for name, ref in zip(_RWKV_PARAMS, refs)}


def _rwkv_prompt_kernel(pa_ref, *rest):
    prm_refs = rest[:len(_RWKV_PARAMS)]
    ya_ref, shift_ref, wkv_ref, carry_ref, state_ref = rest[len(_RWKV_PARAMS):]
    i = pl.program_id(0)
    nseq, tb, _ = pa_ref.shape

    @pl.when(i == 0)
    def _():
        carry_ref[...] = jnp.zeros_like(carry_ref)
        state_ref[...] = jnp.zeros_like(state_ref)

    p = _load_params(prm_refs)
    cst = _rwkv_consts()
    n_chunks = tb // RWKV_CHUNK
    group = min(n_chunks, RWKV_GROUP)
    n_groups = n_chunks // group
    slab = group * RWKV_CHUNK
    ri = _iota((slab, slab), 0)
    ci = _iota((slab, slab), 1)
    chunk_tril = ((ri // RWKV_CHUNK == ci // RWKV_CHUNK) & (ri >= ci)).astype(F32)
    first_row = _iota((slab, A_PROJ), 0) == 0

    def prepare(n, gi):
        r0 = gi * slab
        pa = pa_ref[n, r0:r0 + slab, :]
        before = carry_ref[n] if gi == 0 else pa_ref[n, r0 - 1:r0, :]
        prev = jnp.where(first_row, before, pltpu.roll(pa, 1, axis=0))
        r, k, v, logd, av, bv, g = _rwkv_prep(pa + (prev - pa) * p["mu"], p)
        yield
        cum = _mm_sel_left(chunk_tril, logd)
        yield
        return dict(r=r, k=k, v=v, logd=logd, cum=cum, av=av, bv=bv, g=g)

    def setup(s, c):
        rows = slice(c * RWKV_CHUNK, (c + 1) * RWKV_CHUNK)
        return _rwkv_chunk_setup(s["r"][rows], s["k"][rows], s["v"][rows], s["logd"][rows], s["cum"][rows],
                                 s["av"][rows], s["bv"][rows], cst)

    def apply_group(state, group_setups):
        ys = []
        for chunk in group_setups:
            y, state = yield from _rwkv_chunk_apply(chunk, state, cst[0])
            ys.append(y)
        return jnp.concatenate(ys, axis=0), state

    states = [state_ref[n] for n in range(nseq)]
    slabs = {0: _interleave([prepare(n, 0) for n in range(nseq)])}
    setups = {}
    for gi in range(n_groups + 1):
        jobs = []
        if gi >= 1:
            jobs += [apply_group(states[n], setups[gi - 1][n]) for n in range(nseq)]
        if gi < n_groups:
            jobs += [setup(slabs[gi][n], c) for n in range(nseq) for c in range(group)]
        if gi + 1 < n_groups:
            jobs += [prepare(n, gi + 1) for n in range(nseq)]
        done = _interleave(jobs)
        if gi >= 1:
            for n in range(nseq):
                y, states[n] = done[n]
                s = slabs[gi - 1][n]
                ya_ref[n, (gi - 1) * slab:gi * slab, :] = _rwkv_post(y, s["r"], s["k"], s["v"], s["g"], p)
            done = done[nseq:]
        if gi < n_groups:
            setups[gi] = [done[n * group:(n + 1) * group] for n in range(nseq)]
            done = done[nseq * group:]
        if gi + 1 < n_groups:
            slabs[gi + 1] = done
    for n in range(nseq):
        state_ref[n] = states[n]
        last = pa_ref[n, tb - 1:tb, :]
        carry_ref[n] = last
        shift_ref[n] = last

    @pl.when(i == pl.num_programs(0) - 1)
    def _():
        for n in range(nseq):
            for h in range(A_HEADS):
                sl = slice(h * HEAD_DIM, (h + 1) * HEAD_DIM)
                wkv_ref[n, h] = state_ref[n, sl, sl]


def _rwkv_param_specs(layer):
    shapes = {"mu": (1, A_PROJ), "w2": (LANES, A_WIDTH), "a2": (LANES, A_WIDTH), "g2": (A_GATE_LORA, A_WIDTH)}
    return [_layer_spec(layer, shapes.get(name, (1, A_WIDTH))) for name in _RWKV_PARAMS]


def _rwkv_prompt(proj, prm, layer, tb):
    nseq, t, _ = proj.shape
    return pl.pallas_call(
        _rwkv_prompt_kernel,
        grid=(t // tb,),
        in_specs=[pl.BlockSpec((nseq, tb, A_PROJ), lambda i: (0, i, 0))] + _rwkv_param_specs(layer),
        out_specs=[
            pl.BlockSpec((nseq, tb, A_WIDTH), lambda i: (0, i, 0)),
            pl.BlockSpec((nseq, 1, A_PROJ), lambda i: (0, 0, 0)),
            pl.BlockSpec((nseq, A_HEADS, HEAD_DIM, HEAD_DIM), lambda i: (0, 0, 0, 0)),
        ],
        out_shape=[
            jax.ShapeDtypeStruct((nseq, t, A_WIDTH), F32),
            jax.ShapeDtypeStruct((nseq, 1, A_PROJ), F32),
            jax.ShapeDtypeStruct((nseq, A_HEADS, HEAD_DIM, HEAD_DIM), F32),
        ],
        scratch_shapes=[pltpu.VMEM((nseq, 1, A_PROJ), F32), pltpu.VMEM((nseq, A_WIDTH, A_WIDTH), F32)],
        compiler_params=pltpu.CompilerParams(
            dimension_semantics=("arbitrary",), vmem_limit_bytes=VMEM_LIMIT_BYTES),
        name="rwkv_prompt",
    )(proj, *[prm[name] for name in _RWKV_PARAMS])


def _rwkv_decode_kernel(pa_ref, shift_ref, wkv0_ref, *rest):
    prm_refs = rest[:len(_RWKV_PARAMS)]
    ya_ref, wkv_ref, cols_ref, rows_ref, y_ref = rest[-5:]
    h = pl.program_id(0)
    p = _load_params(prm_refs)

    @pl.when(h == 0)
    def _():
        pa = pa_ref[...]
        r, k, v, logd, av, bv, g = _rwkv_prep(pa + (shift_ref[...] - pa) * p["mu"], p)
        for idx, x in enumerate((r, k, v, jnp.exp(logd), av, bv)):
            cols_ref[idx] = x.T
        for idx, x in enumerate((r, k, v, g)):
            rows_ref[idx] = x

    hs = pl.ds(pl.multiple_of(h * HEAD_DIM, HEAD_DIM), HEAD_DIM)
    r_h, k_h, v_h, d_h, a_h, b_h = [cols_ref[idx, hs, :] for idx in range(6)]
    ys = []
    for vi in range(HEAD_DIM):
        s = wkv0_ref[vi]
        u = jnp.sum(s * a_h, axis=0, keepdims=True)
        s = s * d_h + u * b_h + v_h[vi:vi + 1, :] * k_h
        wkv_ref[vi] = s
        ys.append(jnp.sum(s * r_h, axis=0, keepdims=True))
    y_ref[hs, :] = jnp.concatenate(ys, axis=0)

    @pl.when(h == A_HEADS - 1)
    def _():
        ya_ref[...] = _rwkv_post(y_ref[...].T, rows_ref[0], rows_ref[1], rows_ref[2], rows_ref[3], p)


def _all_layer_buffer(buf, n_in, out_idx):
    if isinstance(buf, jax.ShapeDtypeStruct):
        return [], [], {}
    return [pl.BlockSpec(memory_space=pl.ANY)], [buf], {n_in: out_idx}


def _rwkv_decode(proj, shift_prev, wkv0_t, prm, layer, wkv_all_t):
    nd = proj.shape[0]
    state_spec = pl.BlockSpec((None, None, HEAD_DIM, HEAD_DIM, nd), lambda h: (layer, h, 0, 0, 0))
    buf_specs, buf_args, aliases = _all_layer_buffer(wkv_all_t, 3 + len(_RWKV_PARAMS), 1)
    return pl.pallas_call(
        _rwkv_decode_kernel,
        grid=(A_HEADS,),
        in_specs=[
            pl.BlockSpec((nd, A_PROJ), lambda h: (0, 0)),
            pl.BlockSpec((None, nd, A_PROJ), lambda h: (layer, 0, 0)),
            state_spec,
        ] + _rwkv_param_specs(layer) + buf_specs,
        out_specs=[pl.BlockSpec((nd, A_WIDTH), lambda h: (0, 0)), state_spec],
        out_shape=[jax.ShapeDtypeStruct((nd, A_WIDTH), F32), jax.ShapeDtypeStruct(wkv_all_t.shape, F32)],
        input_output_aliases=aliases,
        scratch_shapes=[pltpu.VMEM((6, A_WIDTH, nd), F32), pltpu.VMEM((4, nd, A_WIDTH), F32),
                        pltpu.VMEM((A_WIDTH, nd), F32)],
        compiler_params=pltpu.CompilerParams(
            dimension_semantics=("arbitrary",), vmem_limit_bytes=VMEM_LIMIT_BYTES),
        name="rwkv_decode",
    )(proj, shift_prev, wkv0_t, *[prm[name] for name in _RWKV_PARAMS], *buf_args)


def _place_q_heads(qn):
    lane_half = _iota((qn.shape[0], B_KV_WIDTH), 1) // HEAD_DIM
    placed = []
    for h in range(B_HEADS):
        kh, g = divmod(h, B_GROUP)
        chunk = qn[:, kh * B_KV_WIDTH:(kh + 1) * B_KV_WIDTH]
        if g != kh:
            chunk = pltpu.roll(chunk, HEAD_DIM, axis=1)
        placed.append(jnp.where(lane_half == kh, chunk, 0.0))
    return placed


def _gather_o_heads(o_heads):
    lane_half = _iota(o_heads[0].shape, 1) // HEAD_DIM
    chunks = []
    for kh in range(B_KV_HEADS):
        parts = []
        for g in range(B_GROUP):
            o = o_heads[kh * B_GROUP + g]
            parts.append(o if g == kh else pltpu.roll(o, HEAD_DIM, axis=1))
        chunks.append(jnp.where(lane_half == 0, parts[0], parts[1]))
    return jnp.concatenate(chunks, axis=1)


def _swa_prompt_kernel(layer, sinks_ref, pb_ref, gq_ref, gk_ref, yb_ref, nk_ref, nv_ref, kprev_ref, vprev_ref):
    j = pl.program_id(1)
    tq = pb_ref.shape[0]
    blk = SWA_BLOCK

    @pl.when(j == 0)
    def _():
        kprev_ref[...] = jnp.zeros_like(kprev_ref)
        vprev_ref[...] = jnp.zeros_like(vprev_ref)

    scale = HEAD_DIM ** -0.5
    rows = _iota((B_GROUP * blk, 2 * blk), 0)
    cols = _iota((B_GROUP * blk, 2 * blk), 1)
    qi = rows % blk
    band = (cols >= qi) & (cols <= qi + WINDOW)
    n_sub = tq // blk
    x = pb_ref[...]
    qn = _head_rms_norm(x[:, 0:B_WIDTH], gq_ref[...])
    kn = _head_rms_norm(x[:, B_WIDTH:B_WIDTH + B_KV_WIDTH], gk_ref[...])
    v = x[:, B_WIDTH + B_KV_WIDTH:B_PROJ]
    kfull = jnp.concatenate([kprev_ref[...], kn], axis=0)
    vfull = jnp.concatenate([vprev_ref[...], v], axis=0)
    placed = _place_q_heads(qn)

    def attend(sb, kh):
        q2 = jnp.concatenate([pl_h[sb * blk:(sb + 1) * blk] for pl_h in placed[kh * B_GROUP:(kh + 1) * B_GROUP]],
                             axis=0)
        s = _mm_nt(q2, kfull[sb * blk:(sb + 2) * blk])
        yield
        first_key = jnp.where(j * n_sub + sb > 0, 0, blk)
        s = jnp.where(band & (cols >= first_key), s * scale, -jnp.inf)
        sink = jnp.where(rows[:, 0:1] < blk, sinks_ref[layer, kh * B_GROUP], sinks_ref[layer, kh * B_GROUP + 1])
        m = jnp.maximum(jnp.max(s, axis=-1, keepdims=True), sink)
        pr = jnp.exp(s - m)
        den = jnp.sum(pr, axis=-1, keepdims=True) + jnp.exp(sink - m)
        o = _mm(pr, vfull[sb * blk:(sb + 2) * blk])
        yield
        return o / den

    outs = _interleave([attend(sb, kh) for sb in range(n_sub) for kh in range(B_KV_HEADS)])
    for sb in range(n_sub):
        heads = [outs[sb * B_KV_HEADS + kh][g * blk:(g + 1) * blk]
                 for kh in range(B_KV_HEADS) for g in range(B_GROUP)]
        yb_ref[sb * blk:(sb + 1) * blk, :] = _gather_o_heads(heads)
    kprev_ref[...] = kn[tq - blk:tq]
    vprev_ref[...] = v[tq - blk:tq]
    nk_ref[...] = kn[tq - blk:tq]
    nv_ref[...] = v[tq - blk:tq]


def _swa_prompt(proj, gq, gk, sinks, layer, tq):
    nseq, t, _ = proj.shape
    col = A_PROJ // B_PROJ
    return pl.pallas_call(
        functools.partial(_swa_prompt_kernel, layer),
        grid=(nseq, t // tq),
        in_specs=[
            pl.BlockSpec(memory_space=pltpu.SMEM),
            pl.BlockSpec((None, tq, B_PROJ), lambda n, j: (n, j, col)),
            _layer_spec(layer, (1, B_WIDTH)),
            _layer_spec(layer, (1, B_KV_WIDTH)),
        ],
        out_specs=[
            pl.BlockSpec((None, tq, B_WIDTH), lambda n, j: (n, j, 0)),
            pl.BlockSpec((None, WINDOW, B_KV_WIDTH), lambda n, j: (n, 0, 0)),
            pl.BlockSpec((None, WINDOW, B_KV_WIDTH), lambda n, j: (n, 0, 0)),
        ],
        out_shape=[
            jax.ShapeDtypeStruct((nseq, t, B_WIDTH), F32),
            jax.ShapeDtypeStruct((nseq, WINDOW, B_KV_WIDTH), F32),
            jax.ShapeDtypeStruct((nseq, WINDOW, B_KV_WIDTH), F32),
        ],
        scratch_shapes=[pltpu.VMEM((SWA_BLOCK, B_KV_WIDTH), F32), pltpu.VMEM((SWA_BLOCK, B_KV_WIDTH), F32)],
        compiler_params=pltpu.CompilerParams(
            dimension_semantics=("arbitrary", "arbitrary"), vmem_limit_bytes=VMEM_LIMIT_BYTES),
        name="swa_prompt",
    )(sinks, proj, gq, gk)


def _swa_decode_kernel(layer, sinks_ref, pb_ref, ck_ref, cv_ref, gq_ref, gk_ref, *rest):
    yb_ref, nk_ref, nv_ref = rest[-3:]
    x = pb_ref[...]
    nb = x.shape[0]
    qn = _head_rms_norm(x[:, 0:B_WIDTH], gq_ref[...])
    kn = _head_rms_norm(x[:, B_WIDTH:B_WIDTH + B_KV_WIDTH], gk_ref[...])
    v = x[:, B_WIDTH + B_KV_WIDTH:B_PROJ]
    placed = _place_q_heads(qn)
    scale = HEAD_DIM ** -0.5
    head = _iota((B_HEADS, 1), 0)
    sink = jnp.zeros((B_HEADS, 1), F32)
    for h in range(B_HEADS):
        sink = jnp.where(head == h, sinks_ref[layer, h], sink)
    last_key = _iota((B_KV_WIDTH, WINDOW), 1) == WINDOW - 1
    k_cols = _column_broadcasts(kn, WINDOW)
    v_cols = _column_broadcasts(v, WINDOW)

    def attend(j):
        kc = ck_ref[j]
        vc = cv_ref[j]
        group = slice(j * WINDOW, (j + 1) * WINDOW)
        q4 = jnp.concatenate([pl_h[j:j + 1] for pl_h in placed], axis=0)
        s = _mm(q4, kc) * scale
        nk_ref[j] = jnp.where(last_key, k_cols[:, group], pltpu.roll(kc, WINDOW - 1, axis=1))
        nv_ref[j] = jnp.where(last_key, v_cols[:, group], pltpu.roll(vc, WINDOW - 1, axis=1))
        yield
        s_new = jnp.sum(q4 * kn[j:j + 1], axis=-1, keepdims=True) * scale
        m = jnp.maximum(jnp.maximum(jnp.max(s, axis=-1, keepdims=True), s_new), sink)
        pr = jnp.exp(s - m)
        p_new = jnp.exp(s_new - m)
        den = jnp.sum(pr, axis=-1, keepdims=True) + p_new + jnp.exp(sink - m)
        o = _mm_nt(pr, vc)
        yield
        return (o + p_new * v[j:j + 1]) / den

    o_rows = _interleave([attend(j) for j in range(nb)])
    outs = [jnp.concatenate([o[h:h + 1] for o in o_rows], axis=0) for h in range(B_HEADS)]
    yb_ref[...] = _gather_o_heads(outs)


def _swa_decode(proj, cache_k, cache_v, gq, gk, sinks, layer, k_all, v_all):
    nd = proj.shape[0]
    nb = 2 * DEC_BLOCK if nd % (2 * DEC_BLOCK) == 0 else DEC_BLOCK
    col = A_PROJ // B_PROJ
    cache_spec = pl.BlockSpec((None, nb, WINDOW, B_KV_WIDTH), lambda i: (layer, i, 0, 0))
    k_specs, k_args, k_alias = _all_layer_buffer(k_all, 6, 1)
    v_specs, v_args, v_alias = _all_layer_buffer(v_all, 7, 2)
    return pl.pallas_call(
        functools.partial(_swa_decode_kernel, layer),
        grid=(nd // nb,),
        in_specs=[
            pl.BlockSpec(memory_space=pltpu.SMEM),
            pl.BlockSpec((nb, B_PROJ), lambda i: (i, col)),
            cache_spec, cache_spec,
            _layer_spec(layer, (1, B_WIDTH)),
            _layer_spec(layer, (1, B_KV_WIDTH)),
        ] + k_specs + v_specs,
        out_specs=[pl.BlockSpec((nb, B_WIDTH), lambda i: (i, 0)), cache_spec, cache_spec],
        out_shape=[
            jax.ShapeDtypeStruct((nd, B_WIDTH), F32),
            jax.ShapeDtypeStruct(k_all.shape, F32),
            jax.ShapeDtypeStruct(v_all.shape, F32),
        ],
        input_output_aliases={**k_alias, **v_alias},
        compiler_params=pltpu.CompilerParams(
            dimension_semantics=("arbitrary",), vmem_limit_bytes=VMEM_LIMIT_BYTES),
        name="swa_decode",
    )(sinks, proj, cache_k, cache_v, gq, gk, *k_args, *v_args)


def _head_expand(width_per_head):
    n = C_HEADS * width_per_head
    return (_iota((LANES, n), 0) == _iota((LANES, n), 1) // width_per_head).astype(F32)


def _ssd_prompt_kernel(z_ref, xbc_ref, dt_ref, cw_ref, cb_ref, dtb_ref, aneg_ref, dskip_ref, ng_ref,
                       yc_ref, tail_ref, hout_ref, xpad_ref, h_ref):
    j = pl.program_id(1)
    tb = xbc_ref.shape[0]

    @pl.when(j == 0)
    def _():
        xpad_ref[0:SUBLANES, :] = jnp.zeros((SUBLANES, CONV_DIM), F32)
        h_ref[...] = jnp.zeros_like(h_ref)

    ch = SSD_CHUNK
    n_chunks = tb // ch
    dt = _softplus(dt_ref[...] + dtb_ref[...])
    dt_x = _mm_sel_right(dt, _head_expand(HEAD_DIM))
    ri = _iota((tb, tb), 0)
    ci = _iota((tb, tb), 1)
    chunk_tril = ((ri // ch == ci // ch) & (ri >= ci)).astype(F32)
    acum = _mm_sel_left(chunk_tril, dt * aneg_ref[...])
    acum_x = _mm_sel_right(acum, _head_expand(D_STATE))

    xpad_ref[SUBLANES:SUBLANES + tb, :] = xbc_ref[...]
    cw = cw_ref[...]
    conv = cb_ref[...]
    for k in range(SSM_CONV):
        conv = conv + cw[SSM_CONV - 1 - k:SSM_CONV - k] * xpad_ref[SUBLANES - k:SUBLANES - k + tb, :]
    last_rows = xbc_ref[tb - SUBLANES:tb, :]
    tail_ref[...] = last_rows
    xpad_ref[0:SUBLANES, :] = last_rows
    act = _silu(conv)
    x = act[:, 0:C_WIDTH]
    bm = act[:, C_WIDTH:C_WIDTH + C_GROUPS * D_STATE]
    cm = act[:, C_WIDTH + C_GROUPS * D_STATE:CONV_DIM]
    xdt = x * dt_x
    causal = _iota((ch, ch), 0) >= _iota((ch, ch), 1)
    lane_lo = _iota((ch, 2 * HEAD_DIM), 1) < HEAD_DIM
    row_lo = _iota((2 * HEAD_DIM, D_STATE), 0) < HEAD_DIM
    heads_per_group = C_HEADS // C_GROUPS
    n_pairs = C_HEADS // 2
    head_lanes = lambda e: slice(e * D_STATE, (e + 1) * D_STATE)
    pair_lanes = lambda pr: slice(pr * 2 * HEAD_DIM, (pr + 1) * 2 * HEAD_DIM)
    group_lanes = lambda pr: slice(((2 * pr) // heads_per_group) * D_STATE, ((2 * pr) // heads_per_group + 1) * D_STATE)

    def chunk_local(c):
        rows = slice(c * ch, (c + 1) * ch)
        acx = acum_x[rows]
        ac_t = acum[rows].T
        last = acx[ch - 1:ch]
        decay_to_end = jnp.exp(last - acx)
        cbs = [_mm_nt(cm[rows, g * D_STATE:(g + 1) * D_STATE], bm[rows, g * D_STATE:(g + 1) * D_STATE])
               for g in range(C_GROUPS)]
        yield
        y_diag, states = [], []
        for pr in range(n_pairs):
            e0, e1 = 2 * pr, 2 * pr + 1
            xp = xdt[rows, pair_lanes(pr)]
            yd = []
            for e in (e0, e1):
                seg = acx[:, head_lanes(e)] - ac_t[e:e + 1, :]
                lmat = jnp.exp(jnp.where(causal, seg, -jnp.inf))
                yd.append(_mm(cbs[(2 * pr) // heads_per_group] * lmat, xp))
            y_diag.append(jnp.where(lane_lo, yd[0], yd[1]))
            scale = jnp.where(lane_lo, decay_to_end[:, head_lanes(e0)], decay_to_end[:, head_lanes(e1)])
            states.append(_mm_tn(xp * scale, bm[rows, group_lanes(pr)]))
        yield
        chunk_decay = jnp.exp(last)
        cds = [jnp.where(row_lo, chunk_decay[:, head_lanes(2 * pr)], chunk_decay[:, head_lanes(2 * pr + 1)])
               for pr in range(n_pairs)]
        return y_diag, states, cds, jnp.exp(acx)

    local = _interleave([chunk_local(c) for c in range(n_chunks)])
    h = [h_ref[pair_lanes(pr), :] for pr in range(n_pairs)]
    y_chunks = []
    for c in range(n_chunks):
        rows = slice(c * ch, (c + 1) * ch)
        y_diag, states, cds, ea = local[c]
        pairs = []
        for pr in range(n_pairs):
            y_off = _mm_nt(cm[rows, group_lanes(pr)], h[pr])
            pairs.append(y_diag[pr] + y_off * jnp.where(lane_lo, ea[:, head_lanes(2 * pr)], ea[:, head_lanes(2 * pr + 1)]))
            h[pr] = h[pr] * cds[pr] + states[pr]
        y_chunks.append(jnp.concatenate(pairs, axis=1))
    for pr in range(n_pairs):
        h_ref[pair_lanes(pr), :] = h[pr]
        hout_ref[pair_lanes(pr), :] = h[pr]

    y = (jnp.concatenate(y_chunks, axis=0) + dskip_ref[...] * x) * _silu(z_ref[...])
    yc_ref[...] = _rms_norm(y, ng_ref[...])


def _ssd_prompt(proj, prm, layer, tb):
    nseq, t, _ = proj.shape
    return pl.pallas_call(
        _ssd_prompt_kernel,
        grid=(nseq, t // tb),
        in_specs=[
            pl.BlockSpec((None, tb, C_WIDTH), lambda n, j: (n, j, Z_OFF // C_WIDTH)),
            pl.BlockSpec((None, tb, CONV_DIM), lambda n, j: (n, j, XBC_OFF // CONV_DIM)),
            pl.BlockSpec((None, tb, LANES), lambda n, j: (n, j, DT_OFF // LANES)),
            _layer_spec(layer, (SSM_CONV, CONV_DIM)),
            _layer_spec(layer, (1, CONV_DIM)),
            _layer_spec(layer, (1, LANES)),
            _layer_spec(layer, (1, LANES)),
            _layer_spec(layer, (1, C_WIDTH)),
            _layer_spec(layer, (1, C_WIDTH)),
        ],
        out_specs=[
            pl.BlockSpec((None, tb, C_WIDTH), lambda n, j: (n, j, 0)),
            pl.BlockSpec((None, SUBLANES, CONV_DIM), lambda n, j: (n, 0, 0)),
            pl.BlockSpec((None, C_WIDTH, D_STATE), lambda n, j: (n, 0, 0)),
        ],
        out_shape=[
            jax.ShapeDtypeStruct((nseq, t, C_WIDTH), F32),
            jax.ShapeDtypeStruct((nseq, SUBLANES, CONV_DIM), F32),
            jax.ShapeDtypeStruct((nseq, C_WIDTH, D_STATE), F32),
        ],
        scratch_shapes=[
            pltpu.VMEM((SUBLANES + tb, CONV_DIM), F32),
            pltpu.VMEM((C_WIDTH, D_STATE), F32),
        ],
        compiler_params=pltpu.CompilerParams(
            dimension_semantics=("arbitrary", "arbitrary"), vmem_limit_bytes=VMEM_LIMIT_BYTES),
        name="ssd_prompt",
    )(proj, proj, proj, prm["conv_w"], prm["conv_b"], prm["dt_bias"], prm["a_neg"], prm["d_skip"], prm["norm_g"])


def _ssd_decode_kernel(z_ref, xbc_ref, dt_ref, buf_ref, h0_ref, cw_ref, cb_ref, dtb_ref, aneg_ref, dskip_ref, ng_ref,
                       *rest):
    yc_ref, nbuf_ref, hout_ref = rest[-3:]
    xbc = xbc_ref[...]
    nb = xbc.shape[0]
    cw = cw_ref[...]
    conv = cb_ref[...] + cw[SSM_CONV - 1:SSM_CONV] * xbc
    for k in range(SSM_CONV - 1):
        conv = conv + cw[k:k + 1] * buf_ref[k]
        nbuf_ref[k] = buf_ref[k + 1] if k + 1 < SSM_CONV - 1 else xbc
    act = _silu(conv)
    x = act[:, 0:C_WIDTH]
    bm = act[:, C_WIDTH:C_WIDTH + C_GROUPS * D_STATE]
    cm = act[:, C_WIDTH + C_GROUPS * D_STATE:CONV_DIM]
    dt = _softplus(dt_ref[...] + dtb_ref[...])
    decay = _mm_sel_right(jnp.exp(dt * aneg_ref[...]), _head_expand(D_STATE))
    xdt = x * _mm_sel_right(dt, _head_expand(HEAD_DIM))
    heads_per_group = C_HEADS // C_GROUPS
    x_cols = _column_broadcasts(xdt, D_STATE)
    hc = []
    for j in range(nb):
        for e in range(C_HEADS):
            gs = slice((e // heads_per_group) * D_STATE, (e // heads_per_group + 1) * D_STATE)
            hs = slice(e * HEAD_DIM, (e + 1) * HEAD_DIM)
            h = (h0_ref[j, hs, :] * decay[j:j + 1, e * D_STATE:(e + 1) * D_STATE]
                 + x_cols[hs, j * D_STATE:(j + 1) * D_STATE] * bm[j:j + 1, gs])
            hout_ref[j, hs, :] = h
            hc.append(h * cm[j:j + 1, gs])
    y_cols = _row_sums(hc, D_STATE)
    lane = _iota((C_WIDTH, nb), 1)
    y_t = jnp.zeros((C_WIDTH, nb), F32)
    for j in range(nb):
        y_t = jnp.where(lane == j, y_cols[j * C_WIDTH:(j + 1) * C_WIDTH, 0:nb], y_t)
    y = _mm_nt_sel_left(_eye(nb), y_t)
    y = (y + dskip_ref[...] * x) * _silu(z_ref[...])
    yc_ref[...] = _rms_norm(y, ng_ref[...])


def _ssd_decode(proj, conv_buf_t, h0, prm, layer, h_all):
    nd = proj.shape[0]
    nb = 2 * DEC_BLOCK if nd % (2 * DEC_BLOCK) == 0 else DEC_BLOCK
    state_spec = pl.BlockSpec((None, nb, C_WIDTH, D_STATE), lambda i: (layer, i, 0, 0))
    buf_specs, buf_args, aliases = _all_layer_buffer(h_all, 11, 2)
    return pl.pallas_call(
        _ssd_decode_kernel,
        grid=(nd // nb,),
        in_specs=[
            pl.BlockSpec((nb, C_WIDTH), lambda i: (i, Z_OFF // C_WIDTH)),
            pl.BlockSpec((nb, CONV_DIM), lambda i: (i, XBC_OFF // CONV_DIM)),
            pl.BlockSpec((nb, LANES), lambda i: (i, DT_OFF // LANES)),
            pl.BlockSpec((None, SSM_CONV - 1, nb, CONV_DIM), lambda i: (layer, 0, i, 0)),
            state_spec,
            _layer_spec(layer, (SSM_CONV, CONV_DIM)),
            _layer_spec(layer, (1, CONV_DIM)),
            _layer_spec(layer, (1, LANES)),
            _layer_spec(layer, (1, LANES)),
            _layer_spec(layer, (1, C_WIDTH)),
            _layer_spec(layer, (1, C_WIDTH)),
        ] + buf_specs,
        out_specs=[
            pl.BlockSpec((nb, C_WIDTH), lambda i: (i, 0)),
            pl.BlockSpec((SSM_CONV - 1, nb, CONV_DIM), lambda i: (0, i, 0)),
            state_spec,
        ],
        out_shape=[
            jax.ShapeDtypeStruct((nd, C_WIDTH), F32),
            jax.ShapeDtypeStruct((SSM_CONV - 1, nd, CONV_DIM), F32),
            jax.ShapeDtypeStruct(h_all.shape, F32),
        ],
        input_output_aliases=aliases,
        compiler_params=pltpu.CompilerParams(
            dimension_semantics=("arbitrary",), vmem_limit_bytes=VMEM_LIMIT_BYTES),
        name="ssd_decode",
    )(proj, proj, proj, conv_buf_t, h0, prm["conv_w"], prm["conv_b"], prm["dt_bias"], prm["a_neg"],
      prm["d_skip"], prm["norm_g"], *buf_args)


def _out_ffn_core(x, mix, g1, sh2, sc2, g2, gn, wo_ref, wu_ref, cw_ref, cb_ref, wd_ref, conv_fn):
    x1 = x + g1 * _mm(mix, wo_ref[...])
    h = (_rms_norm(x1, gn) * (1.0 + sc2) + sh2).astype(BF16)
    acc = jnp.zeros(x.shape, F32)
    for c in range(D_FF // FF_CHUNK):
        cs = slice(c * FF_CHUNK, (c + 1) * FF_CHUNK)
        gate = jnp.dot(h, wu_ref[:, cs], preferred_element_type=F32)
        val = jnp.dot(h, wu_ref[:, D_FF + c * FF_CHUNK:D_FF + (c + 1) * FF_CHUNK], preferred_element_type=F32)
        gate = conv_fn(gate, cs, cw_ref[:, cs], cb_ref[:, cs])
        acc = acc + _mm(_silu(gate) * val, wd_ref[cs, :])
    return x1 + g2 * acc


def _out_ffn_prompt_kernel(tiles_per_seq, x_ref, ya_ref, yb_ref, yc_ref, g1_ref, sh2_ref, sc2_ref, g2_ref, gn_ref,
                           wo_ref, wu_ref, cw_ref, cb_ref, wd_ref, o_ref, tail_ref, carry_ref):
    i = pl.program_id(0)
    tm = x_ref.shape[0]

    @pl.when(i % tiles_per_seq == 0)
    def _():
        carry_ref[...] = jnp.zeros_like(carry_ref)

    def conv_fn(gate, cs, cw, cb):
        tail = carry_ref[:, cs]
        out = cb + cw[FFN_CONV - 1:FFN_CONV] * gate
        for k in range(1, FFN_CONV):
            out = out + cw[FFN_CONV - 1 - k:FFN_CONV - k] * _shift_rows(gate, k, tail)
        last_rows = gate[tm - SUBLANES:tm, :]
        carry_ref[:, cs] = last_rows
        tail_ref[:, cs] = last_rows
        return out

    mix = jnp.concatenate([ya_ref[...], yb_ref[...], yc_ref[...]], axis=1)
    o_ref[...] = _out_ffn_core(x_ref[...], mix, g1_ref[...], sh2_ref[...], sc2_ref[...], g2_ref[...], gn_ref[...],
                               wo_ref, wu_ref, cw_ref, cb_ref, wd_ref, conv_fn)


def _out_ffn_decode_kernel(x_ref, ya_ref, yb_ref, yc_ref, g1_ref, sh2_ref, sc2_ref, g2_ref, gn_ref,
                           wo_ref, wu_ref, cw_ref, cb_ref, wd_ref, buf_ref, o_ref, nbuf_ref):
    def conv_fn(gate, cs, cw, cb):
        out = cb + cw[FFN_CONV - 1:FFN_CONV] * gate
        for k in range(FFN_CONV - 1):
            out = out + cw[k:k + 1] * buf_ref[k, :, cs]
            nbuf_ref[k, :, cs] = buf_ref[k + 1, :, cs] if k + 1 < FFN_CONV - 1 else gate
        return out

    mix = jnp.concatenate([ya_ref[...], yb_ref[...], yc_ref[...]], axis=1)
    o_ref[...] = _out_ffn_core(x_ref[...], mix, g1_ref[...], sh2_ref[...], sc2_ref[...], g2_ref[...], gn_ref[...],
                               wo_ref, wu_ref, cw_ref, cb_ref, wd_ref, conv_fn)


def _out_ffn(x, ya, yb, yc, mod, prm, layer, tm, per_row, tiles_per_seq, conv_buf_t=None):
    rows = x.shape[0]
    nseq = rows // (tm * tiles_per_seq) if not per_row else None
    row_spec = lambda w: pl.BlockSpec((tm, w), lambda i: (i, 0))
    in_specs = [row_spec(D_MODEL), row_spec(A_WIDTH), row_spec(B_WIDTH), row_spec(C_WIDTH)]
    in_specs += [_mod_spec(layer, per_row, tiles_per_seq, c) for c in (2, 3, 4, 5)]
    in_specs += [
        _layer_spec(layer, (1, D_MODEL)),
        _layer_spec(layer, (D_MODEL, D_MODEL), single_buffer=True),
        _layer_spec(layer, (D_MODEL, 2 * D_FF), single_buffer=True),
        _layer_spec(layer, (FFN_CONV, D_FF)),
        _layer_spec(layer, (1, D_FF)),
        _layer_spec(layer, (D_FF, D_MODEL), single_buffer=True),
    ]
    args = [x, ya, yb, yc, mod, mod, mod, mod, prm["norm_ffn_g"], prm["w_out"], prm["w_up"], prm["ffn_conv_w"],
            prm["ffn_conv_b"], prm["w_down"]]
    params = pltpu.CompilerParams(dimension_semantics=("arbitrary",), vmem_limit_bytes=VMEM_LIMIT_BYTES)
    if per_row:
        buf_shape = (FFN_CONV - 1, rows, D_FF)
        return pl.pallas_call(
            _out_ffn_decode_kernel,
            grid=(rows // tm,),
            in_specs=in_specs + [pl.BlockSpec((None,) + buf_shape, lambda i: (layer, 0, i, 0))],
            out_specs=[row_spec(D_MODEL), pl.BlockSpec(buf_shape, lambda i: (0, i, 0))],
            out_shape=[jax.ShapeDtypeStruct((rows, D_MODEL), F32), jax.ShapeDtypeStruct(buf_shape, F32)],
            compiler_params=params,
            name="out_ffn_decode",
        )(*args, conv_buf_t)
    return pl.pallas_call(
        functools.partial(_out_ffn_prompt_kernel, tiles_per_seq),
        grid=(rows // tm,),
        in_specs=in_specs,
        out_specs=[row_spec(D_MODEL), pl.BlockSpec((None, SUBLANES, D_FF), lambda i: (i // tiles_per_seq, 0, 0))],
        out_shape=[jax.ShapeDtypeStruct((rows, D_MODEL), F32), jax.ShapeDtypeStruct((nseq, SUBLANES, D_FF), F32)],
        scratch_shapes=[pltpu.VMEM((SUBLANES, D_FF), F32)],
        compiler_params=params,
        name="out_ffn_prompt",
    )(*args)


def _row_param(p, width=None):
    if width is not None and p.shape[-1] != width:
        p = jnp.pad(p, ((0, 0), (0, width - p.shape[-1])))
    return p.reshape(p.shape[0], 1, p.shape[-1])


def _largest_tile(total, cap, quantum):
    tile = min(total, cap)
    while total % tile or tile % quantum:
        tile -= quantum
    return tile


def kernel(x_prompt, x_sample, c_prompt, c_sample, state_rwkv_shift, state_rwkv_wkv, cache_swa_k, cache_swa_v, state_ssm_conv, state_ssm, state_ffn_conv, ada_w, ada_b, norm_mix_g, norm_ffn_g, w_in, w_out, rwkv_mu, rwkv_w0, rwkv_w2, rwkv_a0, rwkv_a2, rwkv_g2, rwkv_k_k, rwkv_k_a, rwkv_r_k, rwkv_ln_g, rwkv_ln_b, attn_q_norm_g, attn_k_norm_g, attn_sinks, ssm_conv_w, ssm_conv_b, ssm_dt_bias, ssm_a_log, ssm_d, ssm_norm_g, ffn_w_up, ffn_conv_w, ffn_conv_b, ffn_w_down):
    depth = w_in.shape[0]
    nseq, t, _ = x_prompt.shape
    nd = x_sample.shape[0]
    assert x_sample.shape[1] == 1 and nd % DEC_BLOCK == 0 and t % SSD_CHUNK == 0

    tm = _largest_tile(t, 512, SUBLANES)
    tiles_per_seq = t // tm
    tb_rwkv = _largest_tile(t, 4 * RWKV_GROUP * RWKV_CHUNK, RWKV_CHUNK)
    tq_swa = _largest_tile(t, 1024, SWA_BLOCK)
    tb_ssd = _largest_tile(t, 512, SSD_CHUNK)
    tm_in = _largest_tile(t, 1024, SUBLANES)

    w_in_p = jnp.pad(w_in, ((0, 0), (0, 0), (0, PROJ_PAD - PROJ))).astype(BF16)
    rwkv_prm = {
        "mu": _row_param(rwkv_mu), "w0": _row_param(rwkv_w0), "a0": _row_param(rwkv_a0),
        "w2": jnp.pad(rwkv_w2, ((0, 0), (0, LANES - A_DECAY_LORA), (0, 0))),
        "a2": jnp.pad(rwkv_a2, ((0, 0), (A_DECAY_LORA, 0), (0, 0))),
        "g2": rwkv_g2,
        "k_k": _row_param(rwkv_k_k), "k_a": _row_param(rwkv_k_a),
        "r_k": _row_param(rwkv_r_k.reshape(depth, A_WIDTH)),
        "ln_g": _row_param(rwkv_ln_g), "ln_b": _row_param(rwkv_ln_b),
    }
    gq = _row_param(jnp.tile(attn_q_norm_g, (1, B_HEADS)))
    gk = _row_param(jnp.tile(attn_k_norm_g, (1, B_KV_HEADS)))
    ssd_prm = {
        "conv_w": ssm_conv_w, "conv_b": _row_param(ssm_conv_b),
        "dt_bias": _row_param(ssm_dt_bias, LANES),
        "a_neg": _row_param(-jnp.exp(ssm_a_log), LANES),
        "d_skip": _row_param(jnp.repeat(ssm_d, HEAD_DIM, axis=1)),
        "norm_g": _row_param(ssm_norm_g),
    }
    ffn_prm = {
        "norm_ffn_g": _row_param(norm_ffn_g), "w_out": w_out.astype(BF16), "w_up": ffn_w_up.astype(BF16),
        "ffn_conv_w": ffn_conv_w, "ffn_conv_b": _row_param(ffn_conv_b), "w_down": ffn_w_down.astype(BF16),
    }
    norm_mix = _row_param(norm_mix_g)

    c_all = jnp.concatenate([jnp.pad(c_prompt, ((0, SUBLANES - nseq), (0, 0))), c_sample], axis=0)
    mod = _ada(c_all, ada_w, ada_b)
    mod_p = mod[:, 0:nseq].reshape(depth, nseq, 1, N_MOD * D_MODEL)
    mod_s = mod[:, SUBLANES:]

    to_keys_minor = lambda c: jnp.transpose(c, (0, 1, 3, 4, 2)).reshape(depth, nd, B_KV_WIDTH, WINDOW)
    from_keys_minor = lambda c: jnp.transpose(c.reshape(depth, nd, B_KV_HEADS, HEAD_DIM, WINDOW), (0, 1, 4, 2, 3))
    cache_k = to_keys_minor(cache_swa_k)
    cache_v = to_keys_minor(cache_swa_v)
    wkv0_t = jnp.transpose(state_rwkv_wkv, (0, 2, 3, 4, 1))
    ssm_conv_t = jnp.swapaxes(state_ssm_conv, 1, 2)
    ssm_h0 = state_ssm.reshape(depth, nd, C_WIDTH, D_STATE)
    ffn_conv_t = jnp.swapaxes(state_ffn_conv, 1, 2)

    xp = x_prompt.reshape(nseq * t, D_MODEL)
    xs = x_sample.reshape(nd, D_MODEL)
    prompt_new = [[] for _ in range(7)]
    sample_new = [[] for _ in range(3)]
    s_wkv = jax.ShapeDtypeStruct(wkv0_t.shape, F32)
    s_k = jax.ShapeDtypeStruct(cache_k.shape, F32)
    s_v = jax.ShapeDtypeStruct(cache_v.shape, F32)
    s_ssm = jax.ShapeDtypeStruct(ssm_h0.shape, F32)
    for l in range(depth):
        proj = _in_proj(xp, mod_p, norm_mix, w_in_p, l, tm_in, 0, t // tm_in).reshape(nseq, t, PROJ_PAD)
        ya, p_shift, p_wkv = _rwkv_prompt(proj, rwkv_prm, l, tb_rwkv)
        yb, p_k, p_v = _swa_prompt(proj, gq, gk, attn_sinks, l, tq_swa)
        yc, p_conv, p_ssm = _ssd_prompt(proj, ssd_prm, l, tb_ssd)
        xp, p_ffn = _out_ffn(xp, ya.reshape(nseq * t, A_WIDTH), yb.reshape(nseq * t, B_WIDTH),
                             yc.reshape(nseq * t, C_WIDTH), mod_p, ffn_prm, l, tm, 0, tiles_per_seq)
        new_p = (p_shift.reshape(nseq, A_PROJ), p_wkv,
                 p_k.reshape(nseq, WINDOW, B_KV_HEADS, HEAD_DIM), p_v.reshape(nseq, WINDOW, B_KV_HEADS, HEAD_DIM),
                 p_conv[:, SUBLANES - (SSM_CONV - 1):], p_ssm.reshape(nseq, C_HEADS, HEAD_DIM, D_STATE),
                 p_ffn[:, SUBLANES - (FFN_CONV - 1):])
        proj_s = _in_proj(xs, mod_s, norm_mix, w_in_p, l, nd, nd, 1)
        ya, s_wkv = _rwkv_decode(proj_s, state_rwkv_shift, wkv0_t, rwkv_prm, l, s_wkv)
        yb, s_k, s_v = _swa_decode(proj_s, cache_k, cache_v, gq, gk, attn_sinks, l, s_k, s_v)
        yc, s_conv_t, s_ssm = _ssd_decode(proj_s, ssm_conv_t, ssm_h0, ssd_prm, l, s_ssm)
        xs, s_ffn_t = _out_ffn(xs, ya, yb, yc, mod_s, ffn_prm, l, nd, nd, 1, ffn_conv_t)
        new_s = (proj_s[:, 0:A_PROJ], jnp.swapaxes(s_conv_t, 0, 1), jnp.swapaxes(s_ffn_t, 0, 1))
        for lst, arr in zip(prompt_new, new_p):
            lst.append(arr)
        for lst, arr in zip(sample_new, new_s):
            lst.append(arr)
    outs_p = [jnp.stack(lst) for lst in prompt_new]
    s_shift, s_conv, s_ffn = [jnp.stack(lst) for lst in sample_new]
    outs_s = (s_shift, jnp.transpose(s_wkv, (0, 4, 1, 2, 3)), from_keys_minor(s_k), from_keys_minor(s_v), s_conv,
              s_ssm.reshape(depth, nd, C_HEADS, HEAD_DIM, D_STATE), s_ffn)
    return (xp.reshape(nseq, t, D_MODEL), xs.reshape(nd, 1, D_MODEL), *outs_p, *outs_s)
```
